```python
import math
import jax, jax.numpy as jnp
from jax import lax
import numpy as np

D_MODEL = 1024
BATCH = 8
SEQ = 2048
DEPTH = 1

MEM_LEN = 256
MEM_HEADS = 4
MEM_HEAD_DIM = D_MODEL // MEM_HEADS
A_HEADS = 8
A_HEAD_DIM = 64
A_WIDTH = A_HEADS * A_HEAD_DIM
MOBA_BLOCK = 256
MOBA_TOPK = 3
B_HEADS = 4
B_QK_DIM = 64
B_V_DIM = 2 * B_QK_DIM
B_QK_WIDTH = B_HEADS * 2 * B_QK_DIM
B_WIDTH = B_HEADS * B_V_DIM
MIX_WIDTH = A_WIDTH + B_WIDTH
IN_WIDTH = 3 * A_WIDTH + 2 * B_QK_WIDTH + B_WIDTH
IN_SPLITS = [A_WIDTH, 2 * A_WIDTH, 3 * A_WIDTH, 3 * A_WIDTH + B_QK_WIDTH, 3 * A_WIDTH + 2 * B_QK_WIDTH]
N_ATT_HEADS = A_HEADS + B_HEADS
REL_BUCKETS = 32
REL_MAX_DIST = 128
D_FF = -(-8 * D_MODEL // (3 * 256)) * 256
Q_BLOCK = 128
EPS = 1e-6
NEG_INF = -1e30

kernel_name = "hybrid_moba_diffattn_parallel_heads"


def rms_norm(x, g):
    xf = x.astype(jnp.float32)
    y = xf * lax.rsqrt(jnp.mean(xf * xf, axis=-1, keepdims=True) + EPS)
    return (y * g.astype(jnp.float32)).astype(x.dtype)


def rel_bucket(dist):
    n = jnp.maximum(dist, 0)
    max_exact = REL_BUCKETS // 2
    log_ratio = jnp.log(jnp.maximum(n, max_exact).astype(jnp.float32) / max_exact) / math.log(REL_MAX_DIST / max_exact)
    large = max_exact + (log_ratio * (REL_BUCKETS - max_exact)).astype(jnp.int32)
    large = jnp.minimum(large, REL_BUCKETS - 1)
    return jnp.where(n < max_exact, n, large)


def moba_attention(q, k, v, table_a):
    B_, S, H, dh = q.shape
    nb = -(-S // MOBA_BLOCK)
    pad = nb * MOBA_BLOCK - S
    topk = min(MOBA_TOPK, nb)
    n_chunks = S // Q_BLOCK
    scale = dh ** -0.5
    bias_ht = table_a.T.astype(jnp.float32)
    h_i = jnp.arange(H)[:, None, None]
    blk_ids = jnp.arange(nb)
    key_off = jnp.arange(MOBA_BLOCK)

    def per_batch(args):
        q1, k1, v1 = args
        kp = jnp.pad(k1, ((0, pad), (0, 0), (0, 0))).reshape(nb, MOBA_BLOCK, H, dh).transpose(2, 0, 1, 3)
        vp = jnp.pad(v1, ((0, pad), (0, 0), (0, 0))).reshape(nb, MOBA_BLOCK, H, dh).transpose(2, 0, 1, 3)
        kmean = jnp.mean(kp.astype(jnp.float32), axis=2)
        qc = q1.reshape(n_chunks, Q_BLOCK, H, dh).transpose(0, 2, 1, 3)

        def per_chunk(args2):
            qblk, c = args2
            t = c * Q_BLOCK + jnp.arange(Q_BLOCK)
            own = t // MOBA_BLOCK
            gate = jnp.einsum('hqd,hnd->hqn', qblk.astype(jnp.float32), kmean)
            gate = jnp.where(blk_ids[None, None, :] < own[None, :, None], gate, NEG_INF)
            _, sel = lax.top_k(gate, topk)
            idx = jnp.concatenate([sel, jnp.broadcast_to(own[None, :, None], (H, Q_BLOCK, 1))], axis=-1)
            k_sel = kp[h_i, idx]
            v_sel = vp[h_i, idx]
            logits = jnp.einsum('hqd,hqtld->hqtl', qblk, k_sel, preferred_element_type=jnp.float32) * scale
            kpos = idx[..., None] * MOBA_BLOCK + key_off
            dist = t[None, :, None, None] - kpos
            logits = logits + bias_ht[h_i[..., None], rel_bucket(dist)]
            slot_ok = jnp.concatenate([jnp.arange(topk)[None, :] < own[:, None],
                                       jnp.ones((Q_BLOCK, 1), dtype=bool)], axis=1)
            mask = slot_ok[None, :, :, None] & (dist >= 0)
            logits = jnp.where(mask, logits, NEG_INF)
            p = jax.nn.softmax(logits.reshape(H, Q_BLOCK, -1), axis=-1).reshape(logits.shape)
            out = jnp.einsum('hqtl,hqtld->hqd', p.astype(v_sel.dtype), v_sel, preferred_element_type=jnp.float32)
            return out.astype(q.dtype)

        outs = lax.map(per_chunk, (qc, jnp.arange(n_chunks)))
        return outs.transpose(0, 2, 1, 3).reshape(S, H, dh)

    return lax.map(per_batch, (q, k, v))


def diff_attention(q, k, v, table_b, lam, subln_g, lambda_init):
    B_, S, H, _, dq = q.shape
    n_chunks = S // Q_BLOCK
    scale = dq ** -0.5
    kt = k.transpose(0, 2, 3, 1, 4)
    vt = v.transpose(0, 2, 1, 3)
    qc = q.reshape(B_, n_chunks, Q_BLOCK, H, 2, dq).transpose(1, 0, 3, 4, 2, 5)
    bias_ht = table_b.T.astype(jnp.float32)
    s_pos = jnp.arange(S)

    def per_chunk(args):
        qblk, c = args
        t = c * Q_BLOCK + jnp.arange(Q_BLOCK)
        dist = t[:, None] - s_pos[None, :]
        bias = bias_ht[:, rel_bucket(dist)]
        logits = jnp.einsum('bhmqd,bhmsd->bhmqs', qblk, kt, preferred_element_type=jnp.float32) * scale
        logits = jnp.where(dist >= 0, logits + bias[None, :, None], NEG_INF)
        p = jax.nn.softmax(logits, axis=-1)
        attn = p[:, :, 0] - lam * p[:, :, 1]
        return jnp.einsum('bhqs,bhsd->bhqd', attn.astype(v.dtype), vt, preferred_element_type=jnp.float32)

    outs = lax.map(per_chunk, (qc, jnp.arange(n_chunks)))
    o = outs.transpose(1, 0, 3, 2, 4).reshape(B_, S, H, -1)
    o = rms_norm(o, subln_g) * (1.0 - lambda_init)
    return o.astype(v.dtype)


def memory_cross_attention(x, mem, norm_g, mem_g, wq, wk, wv, wo):
    B_, S, _ = x.shape
    M = mem.shape[1]
    h = rms_norm(x, norm_g)
    m = rms_norm(mem, mem_g)
    q = (h @ wq).reshape(B_, S, MEM_HEADS, MEM_HEAD_DIM)
    k = (m @ wk).reshape(B_, M, MEM_HEADS, MEM_HEAD_DIM)
    v = (m @ wv).reshape(B_, M, MEM_HEADS, MEM_HEAD_DIM)
    logits = jnp.einsum('bshd,bmhd->bhsm', q, k, preferred_element_type=jnp.float32) * (MEM_HEAD_DIM ** -0.5)
    p = jax.nn.softmax(logits, axis=-1)
    o = jnp.einsum('bhsm,bmhd->bshd', p.astype(v.dtype), v, preferred_element_type=jnp.float32).astype(x.dtype)
    return o.reshape(B_, S, MEM_HEADS * MEM_HEAD_DIM) @ wo


def swiglu(x, norm_g, w_gate, w_up, w_down):
    h = rms_norm(x, norm_g)
    return (jax.nn.silu(h @ w_gate) * (h @ w_up)) @ w_down


def setup_inputs(seed: int = 0) -> dict:
    key = jax.random.key(seed)
    ks = jax.random.split(key, 24)
    f32 = jnp.float32

    def w(k, shape, fan_in):
        return jax.random.normal(k, shape, f32) * (fan_in ** -0.5)

    def gain(k, shape):
        return 1.0 + 0.02 * jax.random.normal(k, shape, f32)

    return {
        "x": jax.random.normal(ks[0], (BATCH, SEQ, D_MODEL), f32),
        "mem": jax.random.normal(ks[1], (BATCH, MEM_LEN, D_MODEL), f32),
        "mix_norm_g": gain(ks[2], (DEPTH, D_MODEL)),
        "w_in": w(ks[3], (DEPTH, D_MODEL, IN_WIDTH), D_MODEL),
        "moba_out_g": gain(ks[4], (DEPTH, A_WIDTH)),
        "diff_lambda": 0.1 * jax.random.normal(ks[5], (DEPTH, 4, B_QK_DIM), f32),
        "diff_subln_g": gain(ks[6], (DEPTH, B_V_DIM)),
        "w_out": w(ks[7], (DEPTH, MIX_WIDTH, D_MODEL), MIX_WIDTH),
        "rel_bias_table": 0.5 * jax.random.normal(ks[8], (REL_BUCKETS, N_ATT_HEADS), f32),
        "cross_norm_g": gain(ks[9], (DEPTH, D_MODEL)),
        "mem_norm_g": gain(ks[10], (DEPTH, D_MODEL)),
        "w_cq": w(ks[11], (DEPTH, D_MODEL, MEM_HEADS * MEM_HEAD_DIM), D_MODEL),
        "w_ck": w(ks[12], (DEPTH, D_MODEL, MEM_HEADS * MEM_HEAD_DIM), D_MODEL),
        "w_cv": w(ks[13], (DEPTH, D_MODEL, MEM_HEADS * MEM_HEAD_DIM), D_MODEL),
        "w_co": w(ks[14], (DEPTH, MEM_HEADS * MEM_HEAD_DIM, D_MODEL), MEM_HEADS * MEM_HEAD_DIM),
        "ffn_norm_g": gain(ks[15], (DEPTH, D_MODEL)),
        "w_gate": w(ks[16], (DEPTH, D_MODEL, D_FF), D_MODEL),
        "w_up": w(ks[17], (DEPTH, D_MODEL, D_FF), D_MODEL),
        "w_down": w(ks[18], (DEPTH, D_FF, D_MODEL), D_FF),
        "final_norm_g": gain(ks[19], (D_MODEL,)),
    }


def reference(x, mem, mix_norm_g, w_in, moba_out_g, diff_lambda, diff_subln_g, w_out, rel_bias_table,
              cross_norm_g, mem_norm_g, w_cq, w_ck, w_cv, w_co, ffn_norm_g, w_gate, w_up, w_down,
              final_norm_g):
    B_, S, _ = x.shape
    table_a = rel_bias_table[:, :A_HEADS]
    table_b = rel_bias_table[:, A_HEADS:]
    for l in range(DEPTH):
        h = rms_norm(x, mix_norm_g[l])
        proj = jnp.einsum('bsd,de->bse', h, w_in[l])
        qa, ka, va, qb, kb, vb = jnp.split(proj, IN_SPLITS, axis=-1)
        oa = moba_attention(qa.reshape(B_, S, A_HEADS, A_HEAD_DIM),
                            ka.reshape(B_, S, A_HEADS, A_HEAD_DIM),
                            va.reshape(B_, S, A_HEADS, A_HEAD_DIM), table_a)
        oa = rms_norm(oa.reshape(B_, S, A_WIDTH), moba_out_g[l])
        lambda_init = 0.8 - 0.6 * math.exp(-0.3 * l)
        lp = diff_lambda[l].astype(jnp.float32)
        lam = jnp.exp(jnp.sum(lp[0] * lp[1])) - jnp.exp(jnp.sum(lp[2] * lp[3])) + lambda_init
        ob = diff_attention(qb.reshape(B_, S, B_HEADS, 2, B_QK_DIM),
                            kb.reshape(B_, S, B_HEADS, 2, B_QK_DIM),
                            vb.reshape(B_, S, B_HEADS, B_V_DIM),
                            table_b, lam, diff_subln_g[l], lambda_init)
        ob = ob.reshape(B_, S, B_WIDTH)
        mixed = jnp.concatenate([oa, ob], axis=-1)
        x = x + jnp.einsum('bse,ed->bsd', mixed, w_out[l])
        x = x + memory_cross_attention(x, mem, cross_norm_g[l], mem_norm_g[l], w_cq[l], w_ck[l], w_cv[l], w_co[l])
        x = x + swiglu(x, ffn_norm_g[l], w_gate[l], w_up[l], w_down[l])
    return rms_norm(x, final_norm_g)
```

```python
import functools
import math

import numpy as np
import jax
import jax.numpy as jnp
from jax import lax
from jax.experimental import pallas as pl
from jax.experimental.pallas import tpu as pltpu

F32 = jnp.float32
BF16 = jnp.bfloat16

A_HEADS = 8
A_HEAD_DIM = 64
A_WIDTH = A_HEADS * A_HEAD_DIM
MOBA_BLOCK = 256
MOBA_TOPK = 3
B_HEADS = 4
B_QK_DIM = 64
B_V_DIM = 2 * B_QK_DIM
B_WIDTH = B_HEADS * B_V_DIM
MEM_HEADS = 4
REL_BUCKETS = 32
REL_MAX_DIST = 128
EPS = 1e-6
NEG_INF = -1e30
LAMBDA_INIT = 0.8 - 0.6 * math.exp(-0.3 * 0)
QK_SCALE = A_HEAD_DIM ** -0.5

LANES = 128
SUBLANES = 8
VMEM_LIMIT_BYTES = 56 * 1024 * 1024

BLK = MOBA_BLOCK
TM_PROJ = 512
TM_CROSS = 256
FF_CHUNK = 256
PROJ_CHUNK = 512


def _dot(a, b):
    return jnp.dot(a, b, preferred_element_type=F32)


def _dot_nt(a, b):
    return lax.dot_general(a, b, (((1,), (1,)), ((), ())), preferred_element_type=F32)


def _rms(x, g):
    return x * lax.rsqrt(jnp.mean(x * x, axis=-1, keepdims=True) + EPS) * g


def _params(n_axes):
    return pltpu.CompilerParams(dimension_semantics=("arbitrary",) * n_axes,
                                vmem_limit_bytes=VMEM_LIMIT_BYTES)


def _resident(shape):
    return pl.BlockSpec(shape, lambda *_: (0,) * len(shape), pipeline_mode=pl.Buffered(1))


def _rel_bucket_np(dist):
    n = np.maximum(dist, 0)
    max_exact = REL_BUCKETS // 2
    ratio = np.maximum(n, max_exact).astype(np.float32) / np.float32(max_exact)
    log_ratio = np.log(ratio) / np.float32(math.log(REL_MAX_DIST / max_exact))
    large = max_exact + (log_ratio * np.float32(REL_BUCKETS - max_exact)).astype(np.int32)
    large = np.minimum(large, REL_BUCKETS - 1)
    return np.where(n < max_exact, n, large).astype(np.int32)


def _bucket_tiles():
    i = np.arange(BLK)[:, None]
    j = np.arange(BLK)[None, :]
    diag = np.where(i - j >= 0, _rel_bucket_np(i - j), -1)
    near = _rel_bucket_np(BLK + i - j)
    return np.stack([diag, near]).astype(np.int32)


def _bias_kernel(tab_ref, idx_ref, o_ref):
    h = pl.program_id(0)
    far = tab_ref[REL_BUCKETS - 1, h]
    idx = idx_ref[...]
    acc = jnp.zeros(idx.shape, F32)
    for b in range(REL_BUCKETS - 1):
        acc = jnp.where(idx == b, tab_ref[b, h] - far, acc)
    o_ref[0] = jnp.where(idx < 0, NEG_INF, acc)


def _bias_tiles(table):
    n_heads = table.shape[1]
    idx = jnp.asarray(_bucket_tiles())
    return pl.pallas_call(
        _bias_kernel,
        grid=(n_heads,),
        in_specs=[pl.BlockSpec(memory_space=pltpu.SMEM),
                  pl.BlockSpec((2, BLK, BLK), lambda h: (0, 0, 0))],
        out_specs=pl.BlockSpec((1, 2, BLK, BLK), lambda h: (h, 0, 0, 0)),
        out_shape=jax.ShapeDtypeStruct((n_heads, 2, BLK, BLK), F32),
        compiler_params=_params(1),
        name="rel_bias_tiles",
    )(table, idx)


def _inproj_kernel(x_ref, g_ref, w_ref, o_ref, *, q_chunks):
    hb = _rms(x_ref[...], g_ref[...]).astype(BF16)
    for j in range(w_ref.shape[1] // PROJ_CHUNK):
        cols = slice(j * PROJ_CHUNK, (j + 1) * PROJ_CHUNK)
        acc = _dot(hb, w_ref[:, cols])
        if j in q_chunks:
            acc = acc * QK_SCALE
        o_ref[:, cols] = acc.astype(BF16)


def _in_projection(x2d, g, w_bf16):
    t, d = x2d.shape
    n = w_bf16.shape[1]
    assert t % TM_PROJ == 0 and n % PROJ_CHUNK == 0
    q_chunks = (0, (3 * A_WIDTH) // PROJ_CHUNK)
    return pl.pallas_call(
        functools.partial(_inproj_kernel, q_chunks=q_chunks),
        grid=(t // TM_PROJ,),
        in_specs=[pl.BlockSpec((TM_PROJ, d), lambda i: (i, 0)),
                  _resident((1, d)),
                  _resident((d, n))],
        out_specs=pl.BlockSpec((TM_PROJ, n), lambda i: (i, 0)),
        out_shape=jax.ShapeDtypeStruct((t, n), BF16),
        compiler_params=_params(1),
        name="in_projection",
    )(x2d, g, w_bf16)


def _half_mask(shape, half):
    lane = lax.broadcasted_iota(jnp.int32, shape, 1)
    return lax.shift_right_logical(lane, 6) == half


def _score_pass(i, qh, k_ref, cols, bias_ref, h, s_ref, mx_ref, row_bias):
    mx_ref[...] = jnp.full(mx_ref.shape, -jnp.inf, F32)

    def scores(n):
        kn = k_ref[pl.ds(pl.multiple_of(n * BLK, BLK), BLK), cols]
        s = _dot_nt(qh, kn)
        rb = row_bias(n)
        return s if rb is None else s + rb

    def put(n, s):
        s_ref[n] = s
        mx_ref[...] = jnp.maximum(mx_ref[...], jnp.maximum(s[:, :LANES], s[:, LANES:]))

    def far_body(n, carry):
        put(n, scores(n))
        return carry

    lax.fori_loop(0, jnp.maximum(i - 1, 0), far_body, 0)

    @pl.when(i >= 1)
    def _near():
        put(i - 1, scores(i - 1) + bias_ref[h, 1])

    kd = k_ref[pl.ds(pl.multiple_of(i * BLK, BLK), BLK), cols]
    put(i, _dot_nt(qh, kd) + bias_ref[h, 0])


def _value_pass(i, v_ref, cols, s_ref, mx_ref, ls_ref, acc_ref):
    m = jnp.max(mx_ref[...], axis=1, keepdims=True)
    ls_ref[...] = jnp.zeros(ls_ref.shape, F32)
    acc_ref[...] = jnp.zeros(acc_ref.shape, F32)

    def body(n, carry):
        p = jnp.exp(s_ref[n] - m)
        ls_ref[...] += p[:, :LANES] + p[:, LANES:]
        vn = v_ref[pl.ds(pl.multiple_of(n * BLK, BLK), BLK), cols]
        acc_ref[...] += _dot(p.astype(BF16), vn)
        return carry

    lax.fori_loop(0, i + 1, body, 0)
    l = jnp.sum(ls_ref[...], axis=1, keepdims=True)
    return acc_ref[...] / l


def _moba_kernel(q_ref, k_ref, v_ref, bias_ref, g_ref, o_ref,
                 kmt_ref, sel_ref, s_ref, mx_ref, ls_ref, acc_ref, out_ref, *, topk):
    i = pl.program_id(1)
    nb = k_ref.shape[0] // BLK

    @pl.when(i == 0)
    def _block_means():
        row = lax.broadcasted_iota(jnp.int32, (A_HEADS, A_WIDTH), 0)
        lane = lax.broadcasted_iota(jnp.int32, (A_HEADS, A_WIDTH), 1)
        head_mask = lax.shift_right_logical(lane, 6) == row
        kmt_ref[...] = jnp.zeros(kmt_ref.shape, F32)
        for n in range(nb):
            kb = k_ref[n * BLK:(n + 1) * BLK, :].astype(F32)
            km = jnp.sum(kb, axis=0, keepdims=True) * (1.0 / BLK)
            piece = jnp.where(head_mask, km, 0.0)
            kmt_ref[n * A_HEADS:(n + 1) * A_HEADS, :] = piece
            kmt_ref[64 + n * A_HEADS:64 + (n + 1) * A_HEADS, :] = piece

    kmt = kmt_ref[...]
    kmt_hi = kmt.astype(BF16)
    kmt_lo = (kmt - kmt_hi.astype(F32)).astype(BF16)
    q_all = q_ref[...]
    gate = _dot_nt(q_all, kmt_hi) + _dot_nt(q_all, kmt_lo)
    lane = lax.broadcasted_iota(jnp.int32, gate.shape, 1)
    blk = lax.shift_right_logical(lane, 3) & 7
    rank = jnp.zeros(gate.shape, jnp.int32)
    for r in range(1, 8):
        other = pltpu.roll(gate, 8 * r, 1)
        ob = (blk + (8 - r)) & 7
        beats = (ob < i) & ((other > gate) | ((other == gate) & (ob < blk)))
        rank = rank + beats.astype(jnp.int32)
    selected = (blk < i) & (rank < topk)
    sel_bias = jnp.where(selected, 0.0, NEG_INF)
    for n in range(nb):
        sel_ref[n] = pltpu.roll(sel_bias, (LANES - 8 * n) % LANES, 1)

    for p in range(A_WIDTH // LANES):
        cols = slice(p * LANES, (p + 1) * LANES)
        qp = q_ref[:, cols]
        halves = []
        for half in range(2):
            h = 2 * p + half
            qh = jnp.where(_half_mask(qp.shape, half), qp, jnp.zeros_like(qp))
            _score_pass(i, qh, k_ref, cols, bias_ref, h, s_ref, mx_ref,
                        lambda n, h=h: sel_ref[n][:, h:h + 1])
            halves.append(_value_pass(i, v_ref, cols, s_ref, mx_ref, ls_ref, acc_ref))
        out_ref[:, cols] = jnp.where(_half_mask(halves[0].shape, 0), halves[0], halves[1])

    o_ref[...] = _rms(out_ref[...], g_ref[...]).astype(BF16)


def _moba_attention(proj, bias, g, batch, seq):
    nq = seq // BLK
    assert seq % BLK == 0 and nq <= 8
    topk = min(MOBA_TOPK, nq)
    return pl.pallas_call(
        functools.partial(_moba_kernel, topk=topk),
        grid=(batch, nq),
        in_specs=[pl.BlockSpec((BLK, A_WIDTH), lambda b, i: (b * nq + i, 0)),
                  pl.BlockSpec((seq, A_WIDTH), lambda b, i: (b, 1)),
                  pl.BlockSpec((seq, A_WIDTH), lambda b, i: (b, 2)),
                  pl.BlockSpec((A_HEADS, 2, BLK, BLK), lambda b, i: (0, 0, 0, 0),
                               pipeline_mode=pl.Buffered(1)),
                  _resident((1, A_WIDTH))],
        out_specs=pl.BlockSpec((BLK, A_WIDTH), lambda b, i: (b * nq + i, 0)),
        out_shape=jax.ShapeDtypeStruct((batch * seq, A_WIDTH), BF16),
        scratch_shapes=[pltpu.VMEM((LANES, A_WIDTH), F32),
                        pltpu.VMEM((nq, BLK, LANES), F32),
                        pltpu.VMEM((nq, BLK, BLK), F32),
                        pltpu.VMEM((BLK, LANES), F32),
                        pltpu.VMEM((BLK, LANES), F32),
                        pltpu.VMEM((BLK, LANES), F32),
                        pltpu.VMEM((BLK, A_WIDTH), F32)],
        compiler_params=_params(2),
        name="moba_attention",
    )(proj, proj, proj, bias, g)


def _diff_kernel(q_ref, k_ref, v_ref, bias_ref, lam_ref, g_ref, o_ref,
                 s_ref, mx_ref, ls_ref, acc_ref):
    i = pl.program_id(1)
    lp = lam_ref[...]
    lam = (jnp.exp(jnp.sum(lp[0:1] * lp[1:2], axis=1, keepdims=True))
           - jnp.exp(jnp.sum(lp[2:3] * lp[3:4], axis=1, keepdims=True)) + LAMBDA_INIT)
    for h in range(B_HEADS):
        cols = slice(h * LANES, (h + 1) * LANES)
        qp = q_ref[:, cols]
        maps = []
        for half in range(2):
            qh = jnp.where(_half_mask(qp.shape, half), qp, jnp.zeros_like(qp))
            _score_pass(i, qh, k_ref, cols, bias_ref, h, s_ref, mx_ref, lambda n: None)
            maps.append(_value_pass(i, v_ref, cols, s_ref, mx_ref, ls_ref, acc_ref))
        o = maps[0] - lam * maps[1]
        o_ref[:, cols] = (_rms(o, g_ref[...]) * (1.0 - LAMBDA_INIT)).astype(BF16)


def _diff_attention(proj, bias, lam, g, batch, seq):
    nq = seq // BLK
    first = (3 * A_WIDTH) // B_WIDTH
    return pl.pallas_call(
        _diff_kernel,
        grid=(batch, nq),
        in_specs=[pl.BlockSpec((BLK, B_WIDTH), lambda b, i: (b * nq + i, first)),
                  pl.BlockSpec((seq, B_WIDTH), lambda b, i: (b, first + 1)),
                  pl.BlockSpec((seq, B_WIDTH), lambda b, i: (b, first + 2)),
                  pl.BlockSpec((B_HEADS, 2, BLK, BLK), lambda b, i: (A_HEADS // B_HEADS, 0, 0, 0),
                               pipeline_mode=pl.Buffered(1)),
                  _resident(lam.shape),
                  _resident((1, B_V_DIM))],
        out_specs=pl.BlockSpec((BLK, B_WIDTH), lambda b, i: (b * nq + i, 0)),
        out_shape=jax.ShapeDtypeStruct((batch * seq, B_WIDTH), BF16),
        scratch_shapes=[pltpu.VMEM((nq, BLK, BLK), F32),
                        pltpu.VMEM((BLK, LANES), F32),
                        pltpu.VMEM((BLK, LANES), F32),
                        pltpu.VMEM((BLK, LANES), F32)],
        compiler_params=_params(2),
        name="diff_attention",
    )(proj, proj, proj, bias, lam, g)


def _memkv_kernel(mem_ref, g_ref, wk_ref, wv_ref, k_ref, v_ref):
    m = _rms(mem_ref[...], g_ref[...]).astype(BF16)
    k_ref[...] = _dot(m, wk_ref[...]).astype(BF16)
    v_ref[...] = _dot(m, wv_ref[...]).astype(BF16)


def _memory_kv(mem2d, g, wk, wv, mem_len):
    t, d = mem2d.shape
    n = wk.shape[1]
    return pl.pallas_call(
        _memkv_kernel,
        grid=(t // mem_len,),
        in_specs=[pl.BlockSpec((mem_len, d), lambda b: (b, 0)),
                  _resident((1, d)), _resident((d, n)), _resident((d, n))],
        out_specs=[pl.BlockSpec((mem_len, n), lambda b: (b, 0))] * 2,
        out_shape=[jax.ShapeDtypeStruct((t, n), BF16)] * 2,
        compiler_params=_params(1),
        name="memory_kv",
    )(mem2d, g, wk, wv)


def _cross_kernel(x_ref, oa_ref, ob_ref, wo_ref, g_ref, wq_ref, kc_ref, vc_ref, wco_ref, o_ref,
                  *, scale):
    x1 = (x_ref[...] + _dot(oa_ref[...], wo_ref[:A_WIDTH, :]) + _dot(ob_ref[...], wo_ref[A_WIDTH:, :]))
    hb = _rms(x1, g_ref[...]).astype(BF16)
    q = (_dot(hb, wq_ref[...]) * scale).astype(BF16)
    hd = q.shape[1] // MEM_HEADS
    heads = []
    for h in range(MEM_HEADS):
        cols = slice(h * hd, (h + 1) * hd)
        s = _dot_nt(q[:, cols], kc_ref[:, cols])
        p = jnp.exp(s - jnp.max(s, axis=1, keepdims=True))
        l = jnp.sum(p, axis=1, keepdims=True)
        heads.append((_dot(p.astype(BF16), vc_ref[:, cols]) / l).astype(BF16))
    o = jnp.concatenate(heads, axis=1)
    o_ref[...] = x1 + _dot(o, wco_ref[...])


def _outproj_cross(x2d, oa, ob, wo, g, wq, kc, vc, wco, seq, mem_len):
    t, d = x2d.shape
    per_seq = seq // TM_CROSS
    assert seq % TM_CROSS == 0
    hd = wq.shape[1] // MEM_HEADS
    scale = hd ** -0.5
    assert math.frexp(scale)[0] == 0.5
    return pl.pallas_call(
        functools.partial(_cross_kernel, scale=scale),
        grid=(t // TM_CROSS,),
        in_specs=[pl.BlockSpec((TM_CROSS, d), lambda i: (i, 0)),
                  pl.BlockSpec((TM_CROSS, A_WIDTH), lambda i: (i, 0)),
                  pl.BlockSpec((TM_CROSS, B_WIDTH), lambda i: (i, 0)),
                  _resident(wo.shape), _resident((1, d)), _resident(wq.shape),
                  pl.BlockSpec((mem_len, kc.shape[1]), lambda i: (i // per_seq, 0)),
                  pl.BlockSpec((mem_len, vc.shape[1]), lambda i: (i // per_seq, 0)),
                  _resident(wco.shape)],
        out_specs=pl.BlockSpec((TM_CROSS, d), lambda i: (i, 0)),
        out_shape=jax.ShapeDtypeStruct((t, d), F32),
        compiler_params=_params(1),
        name="outproj_cross_attention",
    )(x2d, oa, ob, wo, g, wq, kc, vc, wco)


def _ffn_kernel(x_ref, g_ref, wg_ref, wu_ref, wd_ref, gf_ref, o_ref, acc_ref):
    x = x_ref[...]
    hb = _rms(x, g_ref[...]).astype(BF16)
    acc_ref[...] = x
    for c in range(wg_ref.shape[1] // FF_CHUNK):
        cols = slice(c * FF_CHUNK, (c + 1) * FF_CHUNK)
        a = (jax.nn.silu(_dot(hb, wg_ref[:, cols])) * _dot(hb, wu_ref[:, cols])).astype(BF16)
        acc_ref[...] += _dot(a, wd_ref[cols, :])
    o_ref[...] = _rms(acc_ref[...], gf_ref[...])


def _swiglu_final(x2d, g, wg, wu, wd, gf):
    t, d = x2d.shape
    assert t % TM_PROJ == 0 and wg.shape[1] % FF_CHUNK == 0
    return pl.pallas_call(
        _ffn_kernel,
        grid=(t // TM_PROJ,),
        in_specs=[pl.BlockSpec((TM_PROJ, d), lambda i: (i, 0)),
                  _resident((1, d)), _resident(wg.shape), _resident(wu.shape), _resident(wd.shape),
                  _resident((1, d))],
        out_specs=pl.BlockSpec((TM_PROJ, d), lambda i: (i, 0)),
        out_shape=jax.ShapeDtypeStruct((t, d), F32),
        scratch_shapes=[pltpu.VMEM((TM_PROJ, d), F32)],
        compiler_params=_params(1),
        name="swiglu_final_norm",
    )(x2d, g, wg, wu, wd, gf)


def kernel(x, mem, mix_norm_g, w_in, moba_out_g, diff_lambda, diff_subln_g, w_out, rel_bias_table,
           cross_norm_g, mem_norm_g, w_cq, w_ck, w_cv, w_co, ffn_norm_g, w_gate, w_up, w_down,
           final_norm_g):
    batch, seq, d = x.shape
    mem_len = mem.shape[1]
    assert mix_norm_g.shape[0] == 1, "single-layer trunk"
    x2d = x.reshape(batch * seq, d)
    mem2d = mem.reshape(batch * mem_len, d)
    row = lambda v: v.reshape(1, -1).astype(F32)
    wb = lambda w: w[0].astype(BF16)

    bias = _bias_tiles(rel_bias_table.astype(F32))
    proj = _in_projection(x2d, row(mix_norm_g[0]), wb(w_in))
    oa = _moba_attention(proj, bias, row(moba_out_g[0]), batch, seq)
    ob = _diff_attention(proj, bias, diff_lambda[0].astype(F32), row(diff_subln_g[0]), batch, seq)
    kc, vc = _memory_kv(mem2d, row(mem_norm_g[0]), wb(w_ck), wb(w_cv), mem_len)
    x2 = _outproj_cross(x2d, oa, ob, wb(w_out), row(cross_norm_g[0]), wb(w_cq), kc, vc, wb(w_co),
                        seq, mem_len)
    out = _swiglu_final(x2, row(ffn_norm_g[0]), wb(w_gate), wb(w_up), wb(w_down), row(final_norm_g))
    return out.reshape(batch, seq, d)
```

```python
import functools
import math

import numpy as np
import jax
import jax.numpy as jnp
from jax import lax
from jax.experimental import pallas as pl
from jax.experimental.pallas import tpu as pltpu

F32 = jnp.float32
BF16 = jnp.bfloat16

A_HEADS = 8
A_HEAD_DIM = 64
A_WIDTH = A_HEADS * A_HEAD_DIM
MOBA_BLOCK = 256
MOBA_TOPK = 3
B_HEADS = 4
B_QK_DIM = 64
B_V_DIM = 2 * B_QK_DIM
B_WIDTH = B_HEADS * B_V_DIM
MEM_HEADS = 4
REL_BUCKETS = 32
REL_MAX_DIST = 128
EPS = 1e-6
NEG_INF = -1e30
LAMBDA_INIT = 0.8 - 0.6 * math.exp(-0.3 * 0)
QK_SCALE = A_HEAD_DIM ** -0.5

LANES = 128
SUBLANES = 8
VMEM_LIMIT_BYTES = 56 * 1024 * 1024

BLK = MOBA_BLOCK
TM_PROJ = 512
TM_CROSS = 256
FF_CHUNK = 256
PROJ_CHUNK = 512


def _dot(a, b):
    return jnp.dot(a, b, preferred_element_type=F32)


def _dot_nt(a, b):
    return lax.dot_general(a, b, (((1,), (1,)), ((), ())), preferred_element_type=F32)


def _rms(x, g):
    return x * lax.rsqrt(jnp.mean(x * x, axis=-1, keepdims=True) + EPS) * g


def _params(n_axes):
    return pltpu.CompilerParams(dimension_semantics=("arbitrary",) * n_axes,
                                vmem_limit_bytes=VMEM_LIMIT_BYTES)


def _resident(shape):
    return pl.BlockSpec(shape, lambda *_: (0,) * len(shape), pipeline_mode=pl.Buffered(1))


def _rel_bucket_np(dist):
    n = np.maximum(dist, 0)
    max_exact = REL_BUCKETS // 2
    ratio = np.maximum(n, max_exact).astype(np.float32) / np.float32(max_exact)
    log_ratio = np.log(ratio) / np.float32(math.log(REL_MAX_DIST / max_exact))
    large = max_exact + (log_ratio * np.float32(REL_BUCKETS - max_exact)).astype(np.int32)
    large = np.minimum(large, REL_BUCKETS - 1)
    return np.where(n < max_exact, n, large).astype(np.int32)


def _bucket_tiles():
    i = np.arange(BLK)[:, None]
    j = np.arange(BLK)[None, :]
    diag = np.where(i - j >= 0, _rel_bucket_np(i - j), -1)
    near = _rel_bucket_np(BLK + i - j)
    return np.stack([diag, near]).astype(np.int32)


def _bias_kernel(tab_ref, idx_ref, o_ref):
    h = pl.program_id(0)
    far = tab_ref[REL_BUCKETS - 1, h]
    idx = idx_ref[...]
    acc = jnp.zeros(idx.shape, F32)
    for b in range(REL_BUCKETS - 1):
        acc = jnp.where(idx == b, tab_ref[b, h] - far, acc)
    o_ref[0] = jnp.where(idx < 0, NEG_INF, acc)


def _bias_tiles(table):
    n_heads = table.shape[1]
    idx = jnp.asarray(_bucket_tiles())
    return pl.pallas_call(
        _bias_kernel,
        grid=(n_heads,),
        in_specs=[pl.BlockSpec(memory_space=pltpu.SMEM),
                  pl.BlockSpec((2, BLK, BLK), lambda h: (0, 0, 0))],
        out_specs=pl.BlockSpec((1, 2, BLK, BLK), lambda h: (h, 0, 0, 0)),
        out_shape=jax.ShapeDtypeStruct((n_heads, 2, BLK, BLK), F32),
        compiler_params=_params(1),
        name="rel_bias_tiles",
    )(table, idx)


def _inproj_kernel(x_ref, g_ref, w_ref, o_ref, *, q_chunks):
    hb = _rms(x_ref[...], g_ref[...]).astype(BF16)
    for j in range(w_ref.shape[1] // PROJ_CHUNK):
        cols = slice(j * PROJ_CHUNK, (j + 1) * PROJ_CHUNK)
        acc = _dot(hb, w_ref[:, cols])
        if j in q_chunks:
            acc = acc * QK_SCALE
        o_ref[:, cols] = acc.astype(BF16)


def _in_projection(x2d, g, w_bf16):
    t, d = x2d.shape
    n = w_bf16.shape[1]
    assert t % TM_PROJ == 0 and n % PROJ_CHUNK == 0
    q_chunks = (0, (3 * A_WIDTH) // PROJ_CHUNK)
    return pl.pallas_call(
        functools.partial(_inproj_kernel, q_chunks=q_chunks),
        grid=(t // TM_PROJ,),
        in_specs=[pl.BlockSpec((TM_PROJ, d), lambda i: (i, 0)),
                  _resident((1, d)),
                  _resident((d, n))],
        out_specs=pl.BlockSpec((TM_PROJ, n), lambda i: (i, 0)),
        out_shape=jax.ShapeDtypeStruct((t, n), BF16),
        compiler_params=_params(1),
        name="in_projection",
    )(x2d, g, w_bf16)


N_STREAMS = 8


def _half_mask(shape, half):
    lane = lax.broadcasted_iota(jnp.int32, shape, 1)
    return lax.shift_right_logical(lane, 6) == half


def _block_rows(n):
    return pl.ds(pl.multiple_of(n * BLK, BLK), BLK)


def _two_pass_attention(i, q_tile, k_tile, v_tile, bias_tile, s_ref, mx_ref, acc_ref):
    streams = range(N_STREAMS)

    def score(h, n, bias=None):
        s = _dot_nt(q_tile(h), k_tile(h, n))
        if bias is not None:
            s = s + bias
        s_ref[h, n] = s
        return jnp.maximum(s[:, :LANES], s[:, LANES:])

    for h in streams:
        mx_ref[h] = jnp.full(mx_ref.shape[1:], -jnp.inf, F32)

    def far_body(n, carry):
        for h in streams:
            mx_ref[h] = jnp.maximum(mx_ref[h], score(h, n))
        return carry

    lax.fori_loop(0, jnp.maximum(i - 1, 0), far_body, 0)

    @pl.when(i >= 1)
    def _near():
        for h in streams:
            mx_ref[h] = jnp.maximum(mx_ref[h], score(h, i - 1, bias_tile(h, 1)))

    for h in streams:
        mx = jnp.maximum(mx_ref[h], score(h, i, bias_tile(h, 0)))
        mx_ref[h] = jnp.broadcast_to(jnp.max(mx, axis=1, keepdims=True), mx.shape)
        acc_ref[h] = jnp.zeros(acc_ref.shape[1:], F32)

    def value_body(n, carry):
        for h in streams:
            m = mx_ref[h]
            p = jnp.concatenate([jnp.exp(s_ref[h, n, :, :LANES] - m),
                                 jnp.exp(s_ref[h, n, :, LANES:] - m)], axis=1)
            acc_ref[h] += _dot(p.astype(BF16), v_tile(h, n))
        return carry

    lax.fori_loop(0, i + 1, value_body, 0)


def _moba_kernel(q_ref, k_ref, v_ref, bias_ref, g_ref, o_ref,
                 kmt_ref, kaug_ref, vaug_ref, qaug_ref, s_ref, mx_ref, acc_ref, *, topk):
    i = pl.program_id(1)
    seq = k_ref.shape[0]
    nb = seq // BLK
    n_pairs = A_WIDTH // LANES

    @pl.when(i == 0)
    def _per_batch():
        row = lax.broadcasted_iota(jnp.int32, (A_HEADS, A_WIDTH), 0)
        lane = lax.broadcasted_iota(jnp.int32, (A_HEADS, A_WIDTH), 1)
        head_mask = lax.shift_right_logical(lane, 6) == row
        kmt_ref[...] = jnp.zeros(kmt_ref.shape, F32)
        for n in range(nb):
            kb = k_ref[n * BLK:(n + 1) * BLK, :].astype(F32)
            km = jnp.sum(kb, axis=0, keepdims=True) * (1.0 / BLK)
            piece = jnp.where(head_mask, km, 0.0)
            kmt_ref[n * A_HEADS:(n + 1) * A_HEADS, :] = piece
            kmt_ref[64 + n * A_HEADS:64 + (n + 1) * A_HEADS, :] = piece
        key_blk = lax.shift_right_logical(lax.broadcasted_iota(jnp.int32, (seq, LANES), 0), 8)
        lane = lax.broadcasted_iota(jnp.int32, (seq, LANES), 1)
        for p in range(n_pairs):
            cols = slice(p * LANES, (p + 1) * LANES)
            kp = k_ref[:, cols].astype(F32)
            vp = v_ref[:, cols].astype(F32)
            for half in range(2):
                h = 2 * p + half
                own = lax.shift_right_logical(lane, 6) == half
                code = jnp.where((lane & 63) == key_blk * A_HEADS + h, 1.0, 0.0)
                kaug_ref[h] = jnp.where(own, kp, code).astype(BF16)
                vaug_ref[h] = jnp.where(own, vp, 1.0).astype(BF16)

    kmt = kmt_ref[...]
    kmt_hi = kmt.astype(BF16)
    kmt_lo = (kmt - kmt_hi.astype(F32)).astype(BF16)
    q_all = q_ref[...]
    gate = _dot_nt(q_all, kmt_hi) + _dot_nt(q_all, kmt_lo)
    lane = lax.broadcasted_iota(jnp.int32, gate.shape, 1)
    blk = lax.shift_right_logical(lane, 3) & 7
    rank = jnp.zeros(gate.shape, jnp.int32)
    for r in range(1, 8):
        other = pltpu.roll(gate, 8 * r, 1)
        ob = (blk + (8 - r)) & 7
        beats = (ob < i) & ((other > gate) | ((other == gate) & (ob < blk)))
        rank = rank + beats.astype(jnp.int32)
    keep = ((blk < i) & (rank < topk)) | (blk == i)
    sel_bias = jnp.where(keep, 0.0, NEG_INF).astype(BF16)
    for p in range(n_pairs):
        qp = q_ref[:, p * LANES:(p + 1) * LANES]
        for half in range(2):
            qaug_ref[2 * p + half] = jnp.where(_half_mask(qp.shape, half), qp, sel_bias)

    _two_pass_attention(
        i,
        q_tile=lambda h: qaug_ref[h],
        k_tile=lambda h, n: kaug_ref[h, _block_rows(n), :],
        v_tile=lambda h, n: vaug_ref[h, _block_rows(n), :],
        bias_tile=lambda h, which: bias_ref[h, which],
        s_ref=s_ref, mx_ref=mx_ref, acc_ref=acc_ref)

    pairs = []
    for p in range(n_pairs):
        halves = []
        for half in range(2):
            acc = acc_ref[2 * p + half]
            halves.append(acc / pltpu.roll(acc, LANES // 2, 1))
        pairs.append(jnp.where(_half_mask(halves[0].shape, 0), halves[0], halves[1]))
    o = jnp.concatenate(pairs, axis=1)
    o_ref[...] = _rms(o, g_ref[...]).astype(BF16)


def _moba_attention(proj, bias, g, batch, seq):
    nq = seq // BLK
    assert seq % BLK == 0 and nq <= 8 and A_HEADS == N_STREAMS
    topk = min(MOBA_TOPK, nq)
    return pl.pallas_call(
        functools.partial(_moba_kernel, topk=topk),
        grid=(batch, nq),
        in_specs=[pl.BlockSpec((BLK, A_WIDTH), lambda b, i: (b * nq + i, 0)),
                  pl.BlockSpec((seq, A_WIDTH), lambda b, i: (b, 1)),
                  pl.BlockSpec((seq, A_WIDTH), lambda b, i: (b, 2)),
                  pl.BlockSpec((A_HEADS, 2, BLK, BLK), lambda b, i: (0, 0, 0, 0),
                               pipeline_mode=pl.Buffered(1)),
                  _resident((1, A_WIDTH))],
        out_specs=pl.BlockSpec((BLK, A_WIDTH), lambda b, i: (b * nq + i, 0)),
        out_shape=jax.ShapeDtypeStruct((batch * seq, A_WIDTH), BF16),
        scratch_shapes=[pltpu.VMEM((LANES, A_WIDTH), F32),
                        pltpu.VMEM((A_HEADS, seq, LANES), BF16),
                        pltpu.VMEM((A_HEADS, seq, LANES), BF16),
                        pltpu.VMEM((A_HEADS, BLK, LANES), BF16),
                        pltpu.VMEM((A_HEADS, nq, BLK, BLK), F32),
                        pltpu.VMEM((A_HEADS, BLK, LANES), F32),
                        pltpu.VMEM((A_HEADS, BLK, LANES), F32)],
        compiler_params=_params(2),
        name="moba_attention",
    )(proj, proj, proj, bias, g)


def _diff_kernel(q_ref, k_ref, v_ref, bias_ref, lam_ref, g_ref, o_ref,
                 vaug_ref, qaug_ref, s_ref, mx_ref, acc_ref):
    i = pl.program_id(1)

    @pl.when(i == 0)
    def _per_batch():
        for h in range(B_HEADS):
            vaug_ref[h, :, :B_V_DIM] = v_ref[:, h * B_V_DIM:(h + 1) * B_V_DIM]
            vaug_ref[h, :, B_V_DIM:] = jnp.ones((v_ref.shape[0], LANES), BF16)

    for h in range(B_HEADS):
        qp = q_ref[:, h * LANES:(h + 1) * LANES]
        for half in range(2):
            qaug_ref[2 * h + half] = jnp.where(_half_mask(qp.shape, half), qp, jnp.zeros_like(qp))

    _two_pass_attention(
        i,
        q_tile=lambda s: qaug_ref[s],
        k_tile=lambda s, n: k_ref[_block_rows(n), (s // 2) * LANES:(s // 2 + 1) * LANES],
        v_tile=lambda s, n: vaug_ref[s // 2, _block_rows(n), :],
        bias_tile=lambda s, which: bias_ref[s // 2, which],
        s_ref=s_ref, mx_ref=mx_ref, acc_ref=acc_ref)

    lp = lam_ref[...]
    lam = (jnp.exp(jnp.sum(lp[0:1] * lp[1:2], axis=1, keepdims=True))
           - jnp.exp(jnp.sum(lp[2:3] * lp[3:4], axis=1, keepdims=True)) + LAMBDA_INIT)
    for h in range(B_HEADS):
        a0 = acc_ref[2 * h]
        a1 = acc_ref[2 * h + 1]
        o = a0[:, :B_V_DIM] / a0[:, B_V_DIM:] - lam * (a1[:, :B_V_DIM] / a1[:, B_V_DIM:])
        o_ref[:, h * B_V_DIM:(h + 1) * B_V_DIM] = (_rms(o, g_ref[...]) * (1.0 - LAMBDA_INIT)).astype(BF16)


def _diff_attention(proj, bias, lam, g, batch, seq):
    nq = seq // BLK
    assert 2 * B_HEADS == N_STREAMS and B_V_DIM == LANES
    first = (3 * A_WIDTH) // B_WIDTH
    return pl.pallas_call(
        _diff_kernel,
        grid=(batch, nq),
        in_specs=[pl.BlockSpec((BLK, B_WIDTH), lambda b, i: (b * nq + i, first)),
                  pl.BlockSpec((seq, B_WIDTH), lambda b, i: (b, first + 1)),
                  pl.BlockSpec((seq, B_WIDTH), lambda b, i: (b, first + 2)),
                  pl.BlockSpec((B_HEADS, 2, BLK, BLK), lambda b, i: (A_HEADS // B_HEADS, 0, 0, 0),
                               pipeline_mode=pl.Buffered(1)),
                  _resident(lam.shape),
                  _resident((1, B_V_DIM))],
        out_specs=pl.BlockSpec((BLK, B_WIDTH), lambda b, i: (b * nq + i, 0)),
        out_shape=jax.ShapeDtypeStruct((batch * seq, B_WIDTH), BF16),
        scratch_shapes=[pltpu.VMEM((B_HEADS, seq, 2 * LANES), BF16),
                        pltpu.VMEM((N_STREAMS, BLK, LANES), BF16),
                        pltpu.VMEM((N_STREAMS, nq, BLK, BLK), F32),
                        pltpu.VMEM((N_STREAMS, BLK, LANES), F32),
                        pltpu.VMEM((N_STREAMS, BLK, 2 * LANES), F32)],
        compiler_params=_params(2),
        name="diff_attention",
    )(proj, proj, proj, bias, lam, g)


def _memkv_kernel(mem_ref, g_ref, wk_ref, wv_ref, k_ref, v_ref):
    m = _rms(mem_ref[...], g_ref[...]).astype(BF16)
    k_ref[...] = _dot(m, wk_ref[...]).astype(BF16)
    v_ref[...] = _dot(m, wv_ref[...]).astype(BF16)


def _memory_kv(mem2d, g, wk, wv, mem_len):
    t, d = mem2d.shape
    n = wk.shape[1]
    return pl.pallas_call(
        _memkv_kernel,
        grid=(t // mem_len,),
        in_specs=[pl.BlockSpec((mem_len, d), lambda b: (b, 0)),
                  _resident((1, d)), _resident((d, n)), _resident((d, n))],
        out_specs=[pl.BlockSpec((mem_len, n), lambda b: (b, 0))] * 2,
        out_shape=[jax.ShapeDtypeStruct((t, n), BF16)] * 2,
        compiler_params=_params(1),
        name="memory_kv",
    )(mem2d, g, wk, wv)


def _cross_kernel(x_ref, oa_ref, ob_ref, wo_ref, g_ref, wq_ref, kc_ref, vc_ref, wco_ref, o_ref,
                  *, scale):
    x1 = (x_ref[...] + _dot(oa_ref[...], wo_ref[:A_WIDTH, :]) + _dot(ob_ref[...], wo_ref[A_WIDTH:, :]))
    hb = _rms(x1, g_ref[...]).astype(BF16)
    q = (_dot(hb, wq_ref[...]) * scale).astype(BF16)
    hd = q.shape[1] // MEM_HEADS
    heads = []
    for h in range(MEM_HEADS):
        cols = slice(h * hd, (h + 1) * hd)
        s = _dot_nt(q[:, cols], kc_ref[:, cols])
        p = jnp.exp(s - jnp.max(s, axis=1, keepdims=True))
        l = jnp.sum(p, axis=1, keepdims=True)
        heads.append((_dot(p.astype(BF16), vc_ref[:, cols]) / l).astype(BF16))
    o = jnp.concatenate(heads, axis=1)
    o_ref[...] = x1 + _dot(o, wco_ref[...])


def _outproj_cross(x2d, oa, ob, wo, g, wq, kc, vc, wco, seq, mem_len):
    t, d = x2d.shape
    per_seq = seq // TM_CROSS
    assert seq % TM_CROSS == 0
    hd = wq.shape[1] // MEM_HEADS
    scale = hd ** -0.5
    assert math.frexp(scale)[0] == 0.5
    return pl.pallas_call(
        functools.partial(_cross_kernel, scale=scale),
        grid=(t // TM_CROSS,),
        in_specs=[pl.BlockSpec((TM_CROSS, d), lambda i: (i, 0)),
                  pl.BlockSpec((TM_CROSS, A_WIDTH), lambda i: (i, 0)),
                  pl.BlockSpec((TM_CROSS, B_WIDTH), lambda i: (i, 0)),
                  _resident(wo.shape), _resident((1, d)), _resident(wq.shape),
                  pl.BlockSpec((mem_len, kc.shape[1]), lambda i: (i // per_seq, 0)),
                  pl.BlockSpec((mem_len, vc.shape[1]), lambda i: (i // per_seq, 0)),
                  _resident(wco.shape)],
        out_specs=pl.BlockSpec((TM_CROSS, d), lambda i: (i, 0)),
        out_shape=jax.ShapeDtypeStruct((t, d), F32),
        compiler_params=_params(1),
        name="outproj_cross_attention",
    )(x2d, oa, ob, wo, g, wq, kc, vc, wco)


def _ffn_kernel(x_ref, g_ref, wg_ref, wu_ref, wd_ref, gf_ref, o_ref, acc_ref):
    x = x_ref[...]
    hb = _rms(x, g_ref[...]).astype(BF16)
    acc_ref[...] = x
    for c in range(wg_ref.shape[1] // FF_CHUNK):
        cols = slice(c * FF_CHUNK, (c + 1) * FF_CHUNK)
        a = (jax.nn.silu(_dot(hb, wg_ref[:, cols])) * _dot(hb, wu_ref[:, cols])).astype(BF16)
        acc_ref[...] += _dot(a, wd_ref[cols, :])
    o_ref[...] = _rms(acc_ref[...], gf_ref[...])


def _swiglu_final(x2d, g, wg, wu, wd, gf):
    t, d = x2d.shape
    assert t % TM_PROJ == 0 and wg.shape[1] % FF_CHUNK == 0
    return pl.pallas_call(
        _ffn_kernel,
        grid=(t // TM_PROJ,),
        in_specs=[pl.BlockSpec((TM_PROJ, d), lambda i: (i, 0)),
                  _resident((1, d)), _resident(wg.shape), _resident(wu.shape), _resident(wd.shape),
                  _resident((1, d))],
        out_specs=pl.BlockSpec((TM_PROJ, d), lambda i: (i, 0)),
        out_shape=jax.ShapeDtypeStruct((t, d), F32),
        scratch_shapes=[pltpu.VMEM((TM_PROJ, d), F32)],
        compiler_params=_params(1),
        name="swiglu_final_norm",
    )(x2d, g, wg, wu, wd, gf)


def kernel(x, mem, mix_norm_g, w_in, moba_out_g, diff_lambda, diff_subln_g, w_out, rel_bias_table,
           cross_norm_g, mem_norm_g, w_cq, w_ck, w_cv, w_co, ffn_norm_g, w_gate, w_up, w_down,
           final_norm_g):
    batch, seq, d = x.shape
    mem_len = mem.shape[1]
    assert mix_norm_g.shape[0] == 1, "single-layer trunk"
    x2d = x.reshape(batch * seq, d)
    mem2d = mem.reshape(batch * mem_len, d)
    row = lambda v: v.reshape(1, -1).astype(F32)
    wb = lambda w: w[0].astype(BF16)

    bias = _bias_tiles(rel_bias_table.astype(F32))
    proj = _in_projection(x2d, row(mix_norm_g[0]), wb(w_in))
    oa = _moba_attention(proj, bias, row(moba_out_g[0]), batch, seq)
    ob = _diff_attention(proj, bias, diff_lambda[0].astype(F32), row(diff_subln_g[0]), batch, seq)
    kc, vc = _memory_kv(mem2d, row(mem_norm_g[0]), wb(w_ck), wb(w_cv), mem_len)
    x2 = _outproj_cross(x2d, oa, ob, wb(w_out), row(cross_norm_g[0]), wb(w_cq), kc, vc, wb(w_co),
                        seq, mem_len)
    out = _swiglu_final(x2, row(ffn_norm_g[0]), wb(w_gate), wb(w_up), wb(w_down), row(final_norm_g))
    return out.reshape(batch, seq, d)
```

```python
import functools
import math

import numpy as np
import jax
import jax.numpy as jnp
from jax import lax
from jax.experimental import pallas as pl
from jax.experimental.pallas import tpu as pltpu

F32 = jnp.float32
BF16 = jnp.bfloat16

A_HEADS = 8
A_HEAD_DIM = 64
A_WIDTH = A_HEADS * A_HEAD_DIM
MOBA_BLOCK = 256
MOBA_TOPK = 3
B_HEADS = 4
B_QK_DIM = 64
B_V_DIM = 2 * B_QK_DIM
B_WIDTH = B_HEADS * B_V_DIM
MEM_HEADS = 4
REL_BUCKETS = 32
REL_MAX_DIST = 128
EPS = 1e-6
NEG_INF = -1e30
LAMBDA_INIT = 0.8 - 0.6 * math.exp(-0.3 * 0)
QK_SCALE = A_HEAD_DIM ** -0.5
LOG2E = math.log2(math.e)

LANES = 128
SUBLANES = 8
VMEM_LIMIT_BYTES = 56 * 1024 * 1024

BLK = MOBA_BLOCK
TM_PROJ = 512
TM_CROSS = 256
FF_CHUNK = 256
PROJ_CHUNK = 512


def _dot(a, b):
    return jnp.dot(a, b, preferred_element_type=F32)


def _dot_nt(a, b):
    return lax.dot_general(a, b, (((1,), (1,)), ((), ())), preferred_element_type=F32)


def _rms(x, g):
    return x * lax.rsqrt(jnp.mean(x * x, axis=-1, keepdims=True) + EPS) * g


def _params(n_axes):
    return pltpu.CompilerParams(dimension_semantics=("arbitrary",) * n_axes,
                                vmem_limit_bytes=VMEM_LIMIT_BYTES)


def _resident(shape):
    return pl.BlockSpec(shape, lambda *_: (0,) * len(shape), pipeline_mode=pl.Buffered(1))


def _rel_bucket_np(dist):
    n = np.maximum(dist, 0)
    max_exact = REL_BUCKETS // 2
    ratio = np.maximum(n, max_exact).astype(np.float32) / np.float32(max_exact)
    log_ratio = np.log(ratio) / np.float32(math.log(REL_MAX_DIST / max_exact))
    large = max_exact + (log_ratio * np.float32(REL_BUCKETS - max_exact)).astype(np.int32)
    large = np.minimum(large, REL_BUCKETS - 1)
    return np.where(n < max_exact, n, large).astype(np.int32)


def _bucket_tiles():
    i = np.arange(BLK)[:, None]
    j = np.arange(BLK)[None, :]
    diag = np.where(i - j >= 0, _rel_bucket_np(i - j), -1)
    near = _rel_bucket_np(BLK + i - j)
    return np.stack([diag, near]).astype(np.int32)


def _bias_kernel(tab_ref, idx_ref, o_ref):
    h = pl.program_id(0)
    far = tab_ref[REL_BUCKETS - 1, h]
    idx = idx_ref[...]
    acc = jnp.zeros(idx.shape, F32)
    for b in range(REL_BUCKETS - 1):
        acc = jnp.where(idx == b, (tab_ref[b, h] - far) * LOG2E, acc)
    o_ref[0] = jnp.where(idx < 0, NEG_INF, acc)


def _bias_tiles(table):
    n_heads = table.shape[1]
    idx = jnp.asarray(_bucket_tiles())
    return pl.pallas_call(
        _bias_kernel,
        grid=(n_heads,),
        in_specs=[pl.BlockSpec(memory_space=pltpu.SMEM),
                  pl.BlockSpec((2, BLK, BLK), lambda h: (0, 0, 0))],
        out_specs=pl.BlockSpec((1, 2, BLK, BLK), lambda h: (h, 0, 0, 0)),
        out_shape=jax.ShapeDtypeStruct((n_heads, 2, BLK, BLK), F32),
        compiler_params=_params(1),
        name="rel_bias_tiles",
    )(table, idx)


def _inproj_kernel(x_ref, g_ref, w_ref, o_ref, *, q_chunks):
    hb = _rms(x_ref[...], g_ref[...]).astype(BF16)
    for j in range(w_ref.shape[1] // PROJ_CHUNK):
        cols = slice(j * PROJ_CHUNK, (j + 1) * PROJ_CHUNK)
        acc = _dot(hb, w_ref[:, cols])
        if j in q_chunks:
            acc = acc * (QK_SCALE * LOG2E)
        o_ref[:, cols] = acc.astype(BF16)


def _in_projection(x2d, g, w_bf16):
    t, d = x2d.shape
    n = w_bf16.shape[1]
    assert t % TM_PROJ == 0 and n % PROJ_CHUNK == 0
    q_chunks = (0, (3 * A_WIDTH) // PROJ_CHUNK)
    return pl.pallas_call(
        functools.partial(_inproj_kernel, q_chunks=q_chunks),
        grid=(t // TM_PROJ,),
        in_specs=[pl.BlockSpec((TM_PROJ, d), lambda i: (i, 0)),
                  _resident((1, d)),
                  _resident((d, n))],
        out_specs=pl.BlockSpec((TM_PROJ, n), lambda i: (i, 0)),
        out_shape=jax.ShapeDtypeStruct((t, n), BF16),
        compiler_params=_params(1),
        name="in_projection",
    )(x2d, g, w_bf16)


N_STREAMS = 8


def _half_mask(shape, half):
    lane = lax.broadcasted_iota(jnp.int32, shape, 1)
    return lax.shift_right_logical(lane, 6) == half


def _block_rows(n):
    return pl.ds(pl.multiple_of(n * BLK, BLK), BLK)


def _for_blocks(lo, hi, body):
    count = jnp.maximum(hi - lo, 0)

    def trip(t, carry):
        body(lo + 2 * t)
        body(lo + 2 * t + 1)
        return carry

    lax.fori_loop(0, lax.shift_right_logical(count, 1), trip, 0)

    @pl.when((count & 1) == 1)
    def _odd():
        body(hi - 1)


def _two_pass_attention(i, q_tile, k_tile, v_tile, bias_tile, s_ref, mx_ref, acc_ref):
    streams = range(N_STREAMS)

    def score(h, n, which=None):
        s = _dot_nt(q_tile(h), k_tile(h, n))
        if which is not None:
            s = s + bias_tile(h, which)
        s_ref[h, n] = s
        return jnp.maximum(s[:, :LANES], s[:, LANES:])

    def diagonal():
        for h in streams:
            mx_ref[h] = score(h, i, 0)

    def past(n, which=None):
        for h in streams:
            mx_ref[h] = jnp.maximum(mx_ref[h], score(h, n, which))

    @pl.when(i == 0)
    def _first_tile():
        diagonal()

    @pl.when(i >= 1)
    def _later_tiles():
        diagonal()
        past(i - 1, 1)

    _for_blocks(0, i - 1, past)

    def value(h, n):
        m = mx_ref[h]
        p = jnp.concatenate([jnp.exp2(s_ref[h, n, :, :LANES] - m),
                             jnp.exp2(s_ref[h, n, :, LANES:] - m)], axis=1)
        return _dot(p.astype(BF16), v_tile(h, n))

    for h in streams:
        mx = mx_ref[h]
        mx_ref[h] = jnp.broadcast_to(jnp.max(mx, axis=1, keepdims=True), mx.shape)
        acc_ref[h] = value(h, i)

    def past_values(n):
        for h in streams:
            acc_ref[h] += value(h, n)

    _for_blocks(0, i, past_values)


def _moba_kernel(q_ref, k_ref, v_ref, bias_ref, g_ref, o_ref,
                 kmt_ref, kaug_ref, vaug_ref, qaug_ref, s_ref, mx_ref, acc_ref, *, topk):
    i = pl.program_id(1)
    seq = k_ref.shape[0]
    nb = seq // BLK
    n_pairs = A_WIDTH // LANES

    @pl.when(i == 0)
    def _per_batch():
        row = lax.broadcasted_iota(jnp.int32, (A_HEADS, A_WIDTH), 0)
        lane = lax.broadcasted_iota(jnp.int32, (A_HEADS, A_WIDTH), 1)
        head_mask = lax.shift_right_logical(lane, 6) == row
        kmt_ref[...] = jnp.zeros(kmt_ref.shape, F32)
        for n in range(nb):
            kb = k_ref[n * BLK:(n + 1) * BLK, :].astype(F32)
            km = jnp.sum(kb, axis=0, keepdims=True) * (1.0 / BLK)
            kmt_ref[n * A_HEADS:(n + 1) * A_HEADS, :] = jnp.where(head_mask, km, 0.0)
        lane = lax.broadcasted_iota(jnp.int32, (1, LANES), 1)
        for p in range(n_pairs):
            cols = slice(p * LANES, (p + 1) * LANES)
            for half in range(2):
                h = 2 * p + half
                own = lax.shift_right_logical(lane, 6) == half
                own_one = jnp.where(own, 1.0, 0.0).astype(BF16)
                vaug_ref[h] = v_ref[:, cols] * own_one + jnp.where(own, 0.0, 1.0).astype(BF16)
                for n in range(nb):
                    rows = slice(n * BLK, (n + 1) * BLK)
                    code = jnp.where((~own) & ((lane & 63) == n * A_HEADS + h), 1.0, 0.0).astype(BF16)
                    kaug_ref[h, rows, :] = k_ref[rows, cols] * own_one + code

    kmt = kmt_ref[...]
    kmt_hi = kmt.astype(BF16)
    kmt_lo = (kmt - kmt_hi.astype(F32)).astype(BF16)
    q_all = q_ref[...]
    gate = _dot_nt(kmt_hi, q_all) + _dot_nt(kmt_lo, q_all)
    slabs = [gate[n * A_HEADS:(n + 1) * A_HEADS, :] for n in range(8)]
    rivals = [jnp.where(n < i, slabs[n], -jnp.inf) for n in range(8)]
    sel_rows = []
    for n in range(8):
        rank = jnp.zeros(slabs[n].shape, F32)
        for m in range(8):
            if m != n:
                beats = (rivals[m] >= slabs[n]) if m < n else (rivals[m] > slabs[n])
                rank = rank + jnp.where(beats, 1.0, 0.0)
        past_bias = jnp.where(rank < topk, 0.0, NEG_INF)
        sel_rows.append(jnp.where(n < i, past_bias, 0.0))
    sel_bias = jnp.concatenate(sel_rows + sel_rows, axis=0).T.astype(BF16)
    for p in range(n_pairs):
        qp = q_ref[:, p * LANES:(p + 1) * LANES]
        for half in range(2):
            qaug_ref[2 * p + half] = jnp.where(_half_mask(qp.shape, half), qp, sel_bias)

    _two_pass_attention(
        i,
        q_tile=lambda h: qaug_ref[h],
        k_tile=lambda h, n: kaug_ref[h, _block_rows(n), :],
        v_tile=lambda h, n: vaug_ref[h, _block_rows(n), :],
        bias_tile=lambda h, which: bias_ref[h, which],
        s_ref=s_ref, mx_ref=mx_ref, acc_ref=acc_ref)

    pairs = []
    for p in range(n_pairs):
        a0 = acc_ref[2 * p]
        a1 = acc_ref[2 * p + 1]
        first = _half_mask(a0.shape, 0)
        num = jnp.where(first, a0, a1)
        den = pltpu.roll(jnp.where(first, a1, a0), LANES // 2, 1)
        pairs.append(num / den)
    o = jnp.concatenate(pairs, axis=1)
    o_ref[...] = _rms(o, g_ref[...]).astype(BF16)


def _moba_attention(proj, bias, g, batch, seq):
    nq = seq // BLK
    assert seq % BLK == 0 and nq <= 8 and A_HEADS == N_STREAMS
    topk = min(MOBA_TOPK, nq)
    return pl.pallas_call(
        functools.partial(_moba_kernel, topk=topk),
        grid=(batch, nq),
        in_specs=[pl.BlockSpec((BLK, A_WIDTH), lambda b, i: (b * nq + i, 0)),
                  pl.BlockSpec((seq, A_WIDTH), lambda b, i: (b, 1)),
                  pl.BlockSpec((seq, A_WIDTH), lambda b, i: (b, 2)),
                  pl.BlockSpec((A_HEADS, 2, BLK, BLK), lambda b, i: (0, 0, 0, 0),
                               pipeline_mode=pl.Buffered(1)),
                  _resident((1, A_WIDTH))],
        out_specs=pl.BlockSpec((BLK, A_WIDTH), lambda b, i: (b * nq + i, 0)),
        out_shape=jax.ShapeDtypeStruct((batch * seq, A_WIDTH), BF16),
        scratch_shapes=[pltpu.VMEM((8 * A_HEADS, A_WIDTH), F32),
                        pltpu.VMEM((A_HEADS, seq, LANES), BF16),
                        pltpu.VMEM((A_HEADS, seq, LANES), BF16),
                        pltpu.VMEM((A_HEADS, BLK, LANES), BF16),
                        pltpu.VMEM((A_HEADS, nq, BLK, BLK), F32),
                        pltpu.VMEM((A_HEADS, BLK, LANES), F32),
                        pltpu.VMEM((A_HEADS, BLK, LANES), F32)],
        compiler_params=_params(2),
        name="moba_attention",
    )(proj, proj, proj, bias, g)


def _diff_kernel(q_ref, k_ref, v_ref, bias_ref, lam_ref, g_ref, o_ref,
                 vaug_ref, qaug_ref, s_ref, mx_ref, acc_ref):
    i = pl.program_id(1)

    @pl.when(i == 0)
    def _per_batch():
        for h in range(B_HEADS):
            vaug_ref[h, :, :B_V_DIM] = v_ref[:, h * B_V_DIM:(h + 1) * B_V_DIM]
            vaug_ref[h, :, B_V_DIM:] = jnp.ones((v_ref.shape[0], LANES), BF16)

    for h in range(B_HEADS):
        qp = q_ref[:, h * LANES:(h + 1) * LANES]
        for half in range(2):
            qaug_ref[2 * h + half] = jnp.where(_half_mask(qp.shape, half), qp, jnp.zeros_like(qp))

    _two_pass_attention(
        i,
        q_tile=lambda s: qaug_ref[s],
        k_tile=lambda s, n: k_ref[_block_rows(n), (s // 2) * LANES:(s // 2 + 1) * LANES],
        v_tile=lambda s, n: vaug_ref[s // 2, _block_rows(n), :],
        bias_tile=lambda s, which: bias_ref[s // 2, which],
        s_ref=s_ref, mx_ref=mx_ref, acc_ref=acc_ref)

    lp = lam_ref[...]
    lam = (jnp.exp(jnp.sum(lp[0:1] * lp[1:2], axis=1, keepdims=True))
           - jnp.exp(jnp.sum(lp[2:3] * lp[3:4], axis=1, keepdims=True)) + LAMBDA_INIT)
    for h in range(B_HEADS):
        a0 = acc_ref[2 * h]
        a1 = acc_ref[2 * h + 1]
        n0, l0 = a0[:, :B_V_DIM], a0[:, B_V_DIM:]
        n1, l1 = a1[:, :B_V_DIM], a1[:, B_V_DIM:]
        o = (n0 * l1 - lam * (n1 * l0)) / (l0 * l1)
        o_ref[:, h * B_V_DIM:(h + 1) * B_V_DIM] = (_rms(o, g_ref[...]) * (1.0 - LAMBDA_INIT)).astype(BF16)


def _diff_attention(proj, bias, lam, g, batch, seq):
    nq = seq // BLK
    assert 2 * B_HEADS == N_STREAMS and B_V_DIM == LANES
    first = (3 * A_WIDTH) // B_WIDTH
    return pl.pallas_call(
        _diff_kernel,
        grid=(batch, nq),
        in_specs=[pl.BlockSpec((BLK, B_WIDTH), lambda b, i: (b * nq + i, first)),
                  pl.BlockSpec((seq, B_WIDTH), lambda b, i: (b, first + 1)),
                  pl.BlockSpec((seq, B_WIDTH), lambda b, i: (b, first + 2)),
                  pl.BlockSpec((B_HEADS, 2, BLK, BLK), lambda b, i: (A_HEADS // B_HEADS, 0, 0, 0),
                               pipeline_mode=pl.Buffered(1)),
                  _resident(lam.shape),
                  _resident((1, B_V_DIM))],
        out_specs=pl.BlockSpec((BLK, B_WIDTH), lambda b, i: (b * nq + i, 0)),
        out_shape=jax.ShapeDtypeStruct((batch * seq, B_WIDTH), BF16),
        scratch_shapes=[pltpu.VMEM((B_HEADS, seq, 2 * LANES), BF16),
                        pltpu.VMEM((N_STREAMS, BLK, LANES), BF16),
                        pltpu.VMEM((N_STREAMS, nq, BLK, BLK), F32),
                        pltpu.VMEM((N_STREAMS, BLK, LANES), F32),
                        pltpu.VMEM((N_STREAMS, BLK, 2 * LANES), F32)],
        compiler_params=_params(2),
        name="diff_attention",
    )(proj, proj, proj, bias, lam, g)


def _memkv_kernel(mem_ref, g_ref, wk_ref, wv_ref, k_ref, v_ref):
    m = _rms(mem_ref[...], g_ref[...]).astype(BF16)
    k_ref[...] = _dot(m, wk_ref[...]).astype(BF16)
    v_ref[...] = _dot(m, wv_ref[...]).astype(BF16)


def _memory_kv(mem2d, g, wk, wv, mem_len):
    t, d = mem2d.shape
    n = wk.shape[1]
    return pl.pallas_call(
        _memkv_kernel,
        grid=(t // mem_len,),
        in_specs=[pl.BlockSpec((mem_len, d), lambda b: (b, 0)),
                  _resident((1, d)), _resident((d, n)), _resident((d, n))],
        out_specs=[pl.BlockSpec((mem_len, n), lambda b: (b, 0))] * 2,
        out_shape=[jax.ShapeDtypeStruct((t, n), BF16)] * 2,
        compiler_params=_params(1),
        name="memory_kv",
    )(mem2d, g, wk, wv)


def _cross_kernel(x_ref, oa_ref, ob_ref, wo_ref, g_ref, wq_ref, kc_ref, vc_ref, wco_ref, o_ref,
                  *, scale):
    x1 = (x_ref[...] + _dot(oa_ref[...], wo_ref[:A_WIDTH, :]) + _dot(ob_ref[...], wo_ref[A_WIDTH:, :]))
    hb = _rms(x1, g_ref[...]).astype(BF16)
    q = (_dot(hb, wq_ref[...]) * scale).astype(BF16)
    hd = q.shape[1] // MEM_HEADS
    heads = []
    for h in range(MEM_HEADS):
        cols = slice(h * hd, (h + 1) * hd)
        s = _dot_nt(q[:, cols], kc_ref[:, cols])
        p = jnp.exp2(s - jnp.max(s, axis=1, keepdims=True))
        l = jnp.sum(p, axis=1, keepdims=True)
        heads.append((_dot(p.astype(BF16), vc_ref[:, cols]) / l).astype(BF16))
    o = jnp.concatenate(heads, axis=1)
    o_ref[...] = x1 + _dot(o, wco_ref[...])


def _outproj_cross(x2d, oa, ob, wo, g, wq, kc, vc, wco, seq, mem_len):
    t, d = x2d.shape
    per_seq = seq // TM_CROSS
    assert seq % TM_CROSS == 0
    hd = wq.shape[1] // MEM_HEADS
    scale = hd ** -0.5 * LOG2E
    return pl.pallas_call(
        functools.partial(_cross_kernel, scale=scale),
        grid=(t // TM_CROSS,),
        in_specs=[pl.BlockSpec((TM_CROSS, d), lambda i: (i, 0)),
                  pl.BlockSpec((TM_CROSS, A_WIDTH), lambda i: (i, 0)),
                  pl.BlockSpec((TM_CROSS, B_WIDTH), lambda i: (i, 0)),
                  _resident(wo.shape), _resident((1, d)), _resident(wq.shape),
                  pl.BlockSpec((mem_len, kc.shape[1]), lambda i: (i // per_seq, 0)),
                  pl.BlockSpec((mem_len, vc.shape[1]), lambda i: (i // per_seq, 0)),
                  _resident(wco.shape)],
        out_specs=pl.BlockSpec((TM_CROSS, d), lambda i: (i, 0)),
        out_shape=jax.ShapeDtypeStruct((t, d), F32),
        compiler_params=_params(1),
        name="outproj_cross_attention",
    )(x2d, oa, ob, wo, g, wq, kc, vc, wco)


def _ffn_kernel(x_ref, g_ref, wg_ref, wu_ref, wd_ref, gf_ref, o_ref, acc_ref):
    x = x_ref[...]
    hb = _rms(x, g_ref[...]).astype(BF16)
    acc_ref[...] = x
    for c in range(wg_ref.shape[1] // FF_CHUNK):
        cols = slice(c * FF_CHUNK, (c + 1) * FF_CHUNK)
        a = (jax.nn.silu(_dot(hb, wg_ref[:, cols])) * _dot(hb, wu_ref[:, cols])).astype(BF16)
        acc_ref[...] += _dot(a, wd_ref[cols, :])
    o_ref[...] = _rms(acc_ref[...], gf_ref[...])


def _swiglu_final(x2d, g, wg, wu, wd, gf):
    t, d = x2d.shape
    assert t % TM_PROJ == 0 and wg.shape[1] % FF_CHUNK == 0
    return pl.pallas_call(
        _ffn_kernel,
        grid=(t // TM_PROJ,),
        in_specs=[pl.BlockSpec((TM_PROJ, d), lambda i: (i, 0)),
                  _resident((1, d)), _resident(wg.shape), _resident(wu.shape), _resident(wd.shape),
                  _resident((1, d))],
        out_specs=pl.BlockSpec((TM_PROJ, d), lambda i: (i, 0)),
        out_shape=jax.ShapeDtypeStruct((t, d), F32),
        scratch_shapes=[pltpu.VMEM((TM_PROJ, d), F32)],
        compiler_params=_params(1),
        name="swiglu_final_norm",
    )(x2d, g, wg, wu, wd, gf)


def kernel(x, mem, mix_norm_g, w_in, moba_out_g, diff_lambda, diff_subln_g, w_out, rel_bias_table,
           cross_norm_g, mem_norm_g, w_cq, w_ck, w_cv, w_co, ffn_norm_g, w_gate, w_up, w_down,
           final_norm_g):
    batch, seq, d = x.shape
    mem_len = mem.shape[1]
    assert mix_norm_g.shape[0] == 1, "single-layer trunk"
    x2d = x.reshape(batch * seq, d)
    mem2d = mem.reshape(batch * mem_len, d)
    row = lambda v: v.reshape(1, -1).astype(F32)
    wb = lambda w: w[0].astype(BF16)

    bias = _bias_tiles(rel_bias_table.astype(F32))
    proj = _in_projection(x2d, row(mix_norm_g[0]), wb(w_in))
    oa = _moba_attention(proj, bias, row(moba_out_g[0]), batch, seq)
    ob = _diff_attention(proj, bias, diff_lambda[0].astype(F32), row(diff_subln_g[0]), batch, seq)
    kc, vc = _memory_kv(mem2d, row(mem_norm_g[0]), wb(w_ck), wb(w_cv), mem_len)
    x2 = _outproj_cross(x2d, oa, ob, wb(w_out), row(cross_norm_g[0]), wb(w_cq), kc, vc, wb(w_co),
                        seq, mem_len)
    out = _swiglu_final(x2, row(ffn_norm_g[0]), wb(w_gate), wb(w_up), wb(w_down), row(final_norm_g))
    return out.reshape(batch, seq, d)
```

```python
import functools
import math

import numpy as np
import jax
import jax.numpy as jnp
from jax import lax
from jax.experimental import pallas as pl
from jax.experimental.pallas import tpu as pltpu

F32 = jnp.float32
BF16 = jnp.bfloat16

A_HEADS = 8
A_HEAD_DIM = 64
A_WIDTH = A_HEADS * A_HEAD_DIM
MOBA_BLOCK = 256
MOBA_TOPK = 3
B_HEADS = 4
B_QK_DIM = 64
B_V_DIM = 2 * B_QK_DIM
B_WIDTH = B_HEADS * B_V_DIM
MEM_HEADS = 4
REL_BUCKETS = 32
REL_MAX_DIST = 128
EPS = 1e-6
NEG_INF = -1e30
LAMBDA_INIT = 0.8 - 0.6 * math.exp(-0.3 * 0)
QK_SCALE = A_HEAD_DIM ** -0.5
LOG2E = math.log2(math.e)

LANES = 128
SUBLANES = 8
VMEM_LIMIT_BYTES = 56 * 1024 * 1024

BLK = MOBA_BLOCK
TM_PROJ = 512
TM_CROSS = 256
FF_CHUNK = 256
PROJ_CHUNK = 512


def _dot(a, b):
    return jnp.dot(a, b, preferred_element_type=F32)


def _dot_nt(a, b):
    return lax.dot_general(a, b, (((1,), (1,)), ((), ())), preferred_element_type=F32)


def _rms(x, g):
    return x * lax.rsqrt(jnp.mean(x * x, axis=-1, keepdims=True) + EPS) * g


def _params(n_axes):
    return pltpu.CompilerParams(dimension_semantics=("arbitrary",) * n_axes,
                                vmem_limit_bytes=VMEM_LIMIT_BYTES)


def _resident(shape):
    return pl.BlockSpec(shape, lambda *_: (0,) * len(shape), pipeline_mode=pl.Buffered(1))


def _rel_bucket_np(dist):
    n = np.maximum(dist, 0)
    max_exact = REL_BUCKETS // 2
    ratio = np.maximum(n, max_exact).astype(np.float32) / np.float32(max_exact)
    log_ratio = np.log(ratio) / np.float32(math.log(REL_MAX_DIST / max_exact))
    large = max_exact + (log_ratio * np.float32(REL_BUCKETS - max_exact)).astype(np.int32)
    large = np.minimum(large, REL_BUCKETS - 1)
    return np.where(n < max_exact, n, large).astype(np.int32)


def _bucket_tiles():
    i = np.arange(BLK)[:, None]
    j = np.arange(BLK)[None, :]
    diag = np.where(i - j >= 0, _rel_bucket_np(i - j), -1)
    near = _rel_bucket_np(BLK + i - j)
    return np.stack([diag, near]).astype(np.int32)


def _bias_kernel(tab_ref, idx_ref, o_ref):
    h = pl.program_id(0)
    far = tab_ref[REL_BUCKETS - 1, h]
    idx = idx_ref[...]
    acc = jnp.zeros(idx.shape, F32)
    for b in range(REL_BUCKETS - 1):
        acc = jnp.where(idx == b, (tab_ref[b, h] - far) * LOG2E, acc)
    o_ref[0] = jnp.where(idx < 0, NEG_INF, acc)


def _bias_tiles(table):
    n_heads = table.shape[1]
    idx = jnp.asarray(_bucket_tiles())
    return pl.pallas_call(
        _bias_kernel,
        grid=(n_heads,),
        in_specs=[pl.BlockSpec(memory_space=pltpu.SMEM),
                  pl.BlockSpec((2, BLK, BLK), lambda h: (0, 0, 0))],
        out_specs=pl.BlockSpec((1, 2, BLK, BLK), lambda h: (h, 0, 0, 0)),
        out_shape=jax.ShapeDtypeStruct((n_heads, 2, BLK, BLK), F32),
        compiler_params=_params(1),
        name="rel_bias_tiles",
    )(table, idx)


def _inproj_kernel(x_ref, g_ref, w_ref, o_ref, *, q_chunks):
    hb = _rms(x_ref[...], g_ref[...]).astype(BF16)
    for j in range(w_ref.shape[1] // PROJ_CHUNK):
        cols = slice(j * PROJ_CHUNK, (j + 1) * PROJ_CHUNK)
        acc = _dot(hb, w_ref[:, cols])
        if j in q_chunks:
            acc = acc * (QK_SCALE * LOG2E)
        o_ref[:, cols] = acc.astype(BF16)


def _in_projection(x2d, g, w_bf16):
    t, d = x2d.shape
    n = w_bf16.shape[1]
    assert t % TM_PROJ == 0 and n % PROJ_CHUNK == 0
    q_chunks = (0, (3 * A_WIDTH) // PROJ_CHUNK)
    return pl.pallas_call(
        functools.partial(_inproj_kernel, q_chunks=q_chunks),
        grid=(t // TM_PROJ,),
        in_specs=[pl.BlockSpec((TM_PROJ, d), lambda i: (i, 0)),
                  _resident((1, d)),
                  _resident((d, n))],
        out_specs=pl.BlockSpec((TM_PROJ, n), lambda i: (i, 0)),
        out_shape=jax.ShapeDtypeStruct((t, n), BF16),
        compiler_params=_params(1),
        name="in_projection",
    )(x2d, g, w_bf16)


N_STREAMS = 8


def _half_mask(shape, half):
    lane = lax.broadcasted_iota(jnp.int32, shape, 1)
    return lax.shift_right_logical(lane, 6) == half


def _block_rows(n):
    return slice(n * BLK, (n + 1) * BLK)


def _two_pass_attention(j, q_diag, q_past, k_tile, v_tile, bias_tile, s_ref):
    blocks = list(range(j, -1, -1))
    row_max = [None] * N_STREAMS
    results = [None] * N_STREAMS

    def score_step(h, n):
        def run():
            q = q_diag(h) if n == j else q_past(h)
            s = _dot_nt(q, k_tile(h, n))
            if n >= j - 1:
                s = s + bias_tile(h, j - n)
            s_ref[h, n] = s
            mx = jnp.maximum(s[:, :LANES], s[:, LANES:])
            mx = mx if row_max[h] is None else jnp.maximum(row_max[h], mx)
            if n == 0:
                mx = jnp.broadcast_to(jnp.max(mx, axis=1, keepdims=True), mx.shape)
            row_max[h] = mx
        return run

    def value_step(h, n):
        def run():
            m = row_max[h]
            p = jnp.concatenate([jnp.exp2(s_ref[h, n, :, :LANES] - m),
                                 jnp.exp2(s_ref[h, n, :, LANES:] - m)], axis=1)
            pv = _dot(p.astype(BF16), v_tile(h, n))
            results[h] = pv if results[h] is None else results[h] + pv
        return run

    lead = 2
    for h in range(min(lead, N_STREAMS)):
        for n in blocks:
            score_step(h, n)()
    for h in range(N_STREAMS):
        ahead = h + lead
        for n in blocks:
            value_step(h, n)()
            if ahead < N_STREAMS:
                score_step(ahead, n)()
    return results


def _moba_tile(j, q_ref, bias_ref, g_ref, o_ref, kmt_ref, kaug_ref, vaug_ref, qaug_ref, s_ref, topk):
    n_pairs = A_WIDTH // LANES

    def own_block_query(h):
        qp = q_ref[:, (h // 2) * LANES:(h // 2 + 1) * LANES]
        return jnp.where(_half_mask(qp.shape, h % 2), qp, jnp.zeros_like(qp))

    if j > topk:
        kmt = kmt_ref[...]
        kmt_hi = kmt.astype(BF16)
        kmt_lo = (kmt - kmt_hi.astype(F32)).astype(BF16)
        q_all = q_ref[...]
        gate = _dot_nt(kmt_hi, q_all) + _dot_nt(kmt_lo, q_all)
        slabs = [gate[n * A_HEADS:(n + 1) * A_HEADS, :] for n in range(j)]
        sel_rows = []
        for n in range(j):
            rank = jnp.zeros(slabs[n].shape, F32)
            for m in range(j):
                if m != n:
                    beats = (slabs[m] >= slabs[n]) if m < n else (slabs[m] > slabs[n])
                    rank = rank + jnp.where(beats, 1.0, 0.0)
            sel_rows.append(jnp.where(rank < topk, 0.0, NEG_INF))
        sel_rows.append(jnp.zeros(((8 - j) * A_HEADS, BLK), F32))
        sel_bias = jnp.concatenate(sel_rows + sel_rows, axis=0).T.astype(BF16)
        for p in range(n_pairs):
            qp = q_ref[:, p * LANES:(p + 1) * LANES]
            for half in range(2):
                qaug_ref[2 * p + half] = jnp.where(_half_mask(qp.shape, half), qp, sel_bias)
        past_query = lambda h: qaug_ref[h]
    else:
        past_query = own_block_query

    acc = _two_pass_attention(
        j,
        q_diag=own_block_query,
        q_past=past_query,
        k_tile=lambda h, n: kaug_ref[h, _block_rows(n), :],
        v_tile=lambda h, n: vaug_ref[h, _block_rows(n), :],
        bias_tile=lambda h, which: bias_ref[h, which],
        s_ref=s_ref)

    pairs = []
    for p in range(n_pairs):
        a0, a1 = acc[2 * p], acc[2 * p + 1]
        first = _half_mask(a0.shape, 0)
        num = jnp.where(first, a0, a1)
        den = pltpu.roll(jnp.where(first, a1, a0), LANES // 2, 1)
        pairs.append(num / den)
    o = jnp.concatenate(pairs, axis=1)
    o_ref[...] = _rms(o, g_ref[...]).astype(BF16)


def _moba_kernel(q_ref, k_ref, v_ref, bias_ref, g_ref, o_ref,
                 kmt_ref, kaug_ref, vaug_ref, qaug_ref, s_ref, *, topk):
    i = pl.program_id(1)
    nb = k_ref.shape[0] // BLK
    n_pairs = A_WIDTH // LANES

    @pl.when(i == 0)
    def _per_batch():
        row = lax.broadcasted_iota(jnp.int32, (A_HEADS, A_WIDTH), 0)
        lane = lax.broadcasted_iota(jnp.int32, (A_HEADS, A_WIDTH), 1)
        head_mask = lax.shift_right_logical(lane, 6) == row
        kmt_ref[...] = jnp.zeros(kmt_ref.shape, F32)
        for n in range(nb):
            kb = k_ref[n * BLK:(n + 1) * BLK, :].astype(F32)
            km = jnp.sum(kb, axis=0, keepdims=True) * (1.0 / BLK)
            kmt_ref[n * A_HEADS:(n + 1) * A_HEADS, :] = jnp.where(head_mask, km, 0.0)
        lane = lax.broadcasted_iota(jnp.int32, (1, LANES), 1)
        for p in range(n_pairs):
            cols = slice(p * LANES, (p + 1) * LANES)
            for half in range(2):
                h = 2 * p + half
                own = lax.shift_right_logical(lane, 6) == half
                own_one = jnp.where(own, 1.0, 0.0).astype(BF16)
                vaug_ref[h] = v_ref[:, cols] * own_one + jnp.where(own, 0.0, 1.0).astype(BF16)
                for n in range(nb):
                    rows = slice(n * BLK, (n + 1) * BLK)
                    code = jnp.where((~own) & ((lane & 63) == n * A_HEADS + h), 1.0, 0.0).astype(BF16)
                    kaug_ref[h, rows, :] = k_ref[rows, cols] * own_one + code

    for j in range(nb):
        @pl.when(i == j)
        def _tile(j=j):
            _moba_tile(j, q_ref, bias_ref, g_ref, o_ref, kmt_ref, kaug_ref, vaug_ref, qaug_ref, s_ref, topk)


def _moba_attention(proj, bias, g, batch, seq):
    nq = seq // BLK
    assert seq % BLK == 0 and nq <= 8 and A_HEADS == N_STREAMS
    topk = min(MOBA_TOPK, nq)
    return pl.pallas_call(
        functools.partial(_moba_kernel, topk=topk),
        grid=(batch, nq),
        in_specs=[pl.BlockSpec((BLK, A_WIDTH), lambda b, i: (b * nq + i, 0)),
                  pl.BlockSpec((seq, A_WIDTH), lambda b, i: (b, 1)),
                  pl.BlockSpec((seq, A_WIDTH), lambda b, i: (b, 2)),
                  pl.BlockSpec((A_HEADS, 2, BLK, BLK), lambda b, i: (0, 0, 0, 0),
                               pipeline_mode=pl.Buffered(1)),
                  _resident((1, A_WIDTH))],
        out_specs=pl.BlockSpec((BLK, A_WIDTH), lambda b, i: (b * nq + i, 0)),
        out_shape=jax.ShapeDtypeStruct((batch * seq, A_WIDTH), BF16),
        scratch_shapes=[pltpu.VMEM((8 * A_HEADS, A_WIDTH), F32),
                        pltpu.VMEM((A_HEADS, seq, LANES), BF16),
                        pltpu.VMEM((A_HEADS, seq, LANES), BF16),
                        pltpu.VMEM((A_HEADS, BLK, LANES), BF16),
                        pltpu.VMEM((A_HEADS, nq, BLK, BLK), F32)],
        compiler_params=_params(2),
        name="moba_attention",
    )(proj, proj, proj, bias, g)


def _diff_tile(j, q_ref, k_ref, bias_ref, lam_ref, g_ref, o_ref, vaug_ref, s_ref):
    def map_query(s):
        qp = q_ref[:, (s // 2) * LANES:(s // 2 + 1) * LANES]
        return jnp.where(_half_mask(qp.shape, s % 2), qp, jnp.zeros_like(qp))

    acc = _two_pass_attention(
        j,
        q_diag=map_query,
        q_past=map_query,
        k_tile=lambda s, n: k_ref[_block_rows(n), (s // 2) * LANES:(s // 2 + 1) * LANES],
        v_tile=lambda s, n: vaug_ref[s // 2, _block_rows(n), :],
        bias_tile=lambda s, which: bias_ref[s // 2, which],
        s_ref=s_ref)

    lp = lam_ref[...]
    lam = (jnp.exp(jnp.sum(lp[0:1] * lp[1:2], axis=1, keepdims=True))
           - jnp.exp(jnp.sum(lp[2:3] * lp[3:4], axis=1, keepdims=True)) + LAMBDA_INIT)
    for h in range(B_HEADS):
        a0, a1 = acc[2 * h], acc[2 * h + 1]
        n0, l0 = a0[:, :B_V_DIM], a0[:, B_V_DIM:]
        n1, l1 = a1[:, :B_V_DIM], a1[:, B_V_DIM:]
        o = (n0 * l1 - lam * (n1 * l0)) / (l0 * l1)
        o_ref[:, h * B_V_DIM:(h + 1) * B_V_DIM] = (
            _rms(o, g_ref[...]) * (1.0 - LAMBDA_INIT)).astype(BF16)


def _diff_kernel(q_ref, k_ref, v_ref, bias_ref, lam_ref, g_ref, o_ref, vaug_ref, s_ref):
    i = pl.program_id(1)

    @pl.when(i == 0)
    def _per_batch():
        for h in range(B_HEADS):
            vaug_ref[h, :, :B_V_DIM] = v_ref[:, h * B_V_DIM:(h + 1) * B_V_DIM]
            vaug_ref[h, :, B_V_DIM:] = jnp.ones((v_ref.shape[0], LANES), BF16)

    for j in range(k_ref.shape[0] // BLK):
        @pl.when(i == j)
        def _tile(j=j):
            _diff_tile(j, q_ref, k_ref, bias_ref, lam_ref, g_ref, o_ref, vaug_ref, s_ref)


def _diff_attention(proj, bias, lam, g, batch, seq):
    nq = seq // BLK
    assert 2 * B_HEADS == N_STREAMS and B_V_DIM == LANES
    first = (3 * A_WIDTH) // B_WIDTH
    return pl.pallas_call(
        _diff_kernel,
        grid=(batch, nq),
        in_specs=[pl.BlockSpec((BLK, B_WIDTH), lambda b, i: (b * nq + i, first)),
                  pl.BlockSpec((seq, B_WIDTH), lambda b, i: (b, first + 1)),
                  pl.BlockSpec((seq, B_WIDTH), lambda b, i: (b, first + 2)),
                  pl.BlockSpec((B_HEADS, 2, BLK, BLK), lambda b, i: (A_HEADS // B_HEADS, 0, 0, 0),
                               pipeline_mode=pl.Buffered(1)),
                  _resident(lam.shape),
                  _resident((1, B_V_DIM))],
        out_specs=pl.BlockSpec((BLK, B_WIDTH), lambda b, i: (b * nq + i, 0)),
        out_shape=jax.ShapeDtypeStruct((batch * seq, B_WIDTH), BF16),
        scratch_shapes=[pltpu.VMEM((B_HEADS, seq, 2 * LANES), BF16),
                        pltpu.VMEM((N_STREAMS, nq, BLK, BLK), F32)],
        compiler_params=_params(2),
        name="diff_attention",
    )(proj, proj, proj, bias, lam, g)


def _memkv_kernel(mem_ref, g_ref, wk_ref, wv_ref, k_ref, v_ref):
    m = _rms(mem_ref[...], g_ref[...]).astype(BF16)
    k_ref[...] = _dot(m, wk_ref[...]).astype(BF16)
    v_ref[...] = _dot(m, wv_ref[...]).astype(BF16)


def _memory_kv(mem2d, g, wk, wv, mem_len):
    t, d = mem2d.shape
    n = wk.shape[1]
    return pl.pallas_call(
        _memkv_kernel,
        grid=(t // mem_len,),
        in_specs=[pl.BlockSpec((mem_len, d), lambda b: (b, 0)),
                  _resident((1, d)), _resident((d, n)), _resident((d, n))],
        out_specs=[pl.BlockSpec((mem_len, n), lambda b: (b, 0))] * 2,
        out_shape=[jax.ShapeDtypeStruct((t, n), BF16)] * 2,
        compiler_params=_params(1),
        name="memory_kv",
    )(mem2d, g, wk, wv)


def _cross_kernel(x_ref, oa_ref, ob_ref, wo_ref, g_ref, wq_ref, kc_ref, vc_ref, wco_ref, o_ref,
                  *, scale):
    x1 = (x_ref[...] + _dot(oa_ref[...], wo_ref[:A_WIDTH, :]) + _dot(ob_ref[...], wo_ref[A_WIDTH:, :]))
    hb = _rms(x1, g_ref[...]).astype(BF16)
    q = (_dot(hb, wq_ref[...]) * scale).astype(BF16)
    hd = q.shape[1] // MEM_HEADS
    heads = []
    for h in range(MEM_HEADS):
        cols = slice(h * hd, (h + 1) * hd)
        s = _dot_nt(q[:, cols], kc_ref[:, cols])
        p = jnp.exp2(s - jnp.max(s, axis=1, keepdims=True))
        l = jnp.sum(p, axis=1, keepdims=True)
        heads.append((_dot(p.astype(BF16), vc_ref[:, cols]) / l).astype(BF16))
    o = jnp.concatenate(heads, axis=1)
    o_ref[...] = x1 + _dot(o, wco_ref[...])


def _outproj_cross(x2d, oa, ob, wo, g, wq, kc, vc, wco, seq, mem_len):
    t, d = x2d.shape
    per_seq = seq // TM_CROSS
    assert seq % TM_CROSS == 0
    hd = wq.shape[1] // MEM_HEADS
    scale = hd ** -0.5 * LOG2E
    return pl.pallas_call(
        functools.partial(_cross_kernel, scale=scale),
        grid=(t // TM_CROSS,),
        in_specs=[pl.BlockSpec((TM_CROSS, d), lambda i: (i, 0)),
                  pl.BlockSpec((TM_CROSS, A_WIDTH), lambda i: (i, 0)),
                  pl.BlockSpec((TM_CROSS, B_WIDTH), lambda i: (i, 0)),
                  _resident(wo.shape), _resident((1, d)), _resident(wq.shape),
                  pl.BlockSpec((mem_len, kc.shape[1]), lambda i: (i // per_seq, 0)),
                  pl.BlockSpec((mem_len, vc.shape[1]), lambda i: (i // per_seq, 0)),
                  _resident(wco.shape)],
        out_specs=pl.BlockSpec((TM_CROSS, d), lambda i: (i, 0)),
        out_shape=jax.ShapeDtypeStruct((t, d), F32),
        compiler_params=_params(1),
        name="outproj_cross_attention",
    )(x2d, oa, ob, wo, g, wq, kc, vc, wco)


def _ffn_kernel(x_ref, g_ref, wg_ref, wu_ref, wd_ref, gf_ref, o_ref, acc_ref):
    x = x_ref[...]
    hb = _rms(x, g_ref[...]).astype(BF16)
    acc_ref[...] = x
    for c in range(wg_ref.shape[1] // FF_CHUNK):
        cols = slice(c * FF_CHUNK, (c + 1) * FF_CHUNK)
        a = (jax.nn.silu(_dot(hb, wg_ref[:, cols])) * _dot(hb, wu_ref[:, cols])).astype(BF16)
        acc_ref[...] += _dot(a, wd_ref[cols, :])
    o_ref[...] = _rms(acc_ref[...], gf_ref[...])


def _swiglu_final(x2d, g, wg, wu, wd, gf):
    t, d = x2d.shape
    assert t % TM_PROJ == 0 and wg.shape[1] % FF_CHUNK == 0
    return pl.pallas_call(
        _ffn_kernel,
        grid=(t // TM_PROJ,),
        in_specs=[pl.BlockSpec((TM_PROJ, d), lambda i: (i, 0)),
                  _resident((1, d)), _resident(wg.shape), _resident(wu.shape), _resident(wd.shape),
                  _resident((1, d))],
        out_specs=pl.BlockSpec((TM_PROJ, d), lambda i: (i, 0)),
        out_shape=jax.ShapeDtypeStruct((t, d), F32),
        scratch_shapes=[pltpu.VMEM((TM_PROJ, d), F32)],
        compiler_params=_params(1),
        name="swiglu_final_norm",
    )(x2d, g, wg, wu, wd, gf)


def kernel(x, mem, mix_norm_g, w_in, moba_out_g, diff_lambda, diff_subln_g, w_out, rel_bias_table,
           cross_norm_g, mem_norm_g, w_cq, w_ck, w_cv, w_co, ffn_norm_g, w_gate, w_up, w_down,
           final_norm_g):
    batch, seq, d = x.shape
    mem_len = mem.shape[1]
    assert mix_norm_g.shape[0] == 1, "single-layer trunk"
    x2d = x.reshape(batch * seq, d)
    mem2d = mem.reshape(batch * mem_len, d)
    row = lambda v: v.reshape(1, -1).astype(F32)
    wb = lambda w: w[0].astype(BF16)

    bias = _bias_tiles(rel_bias_table.astype(F32))
    proj = _in_projection(x2d, row(mix_norm_g[0]), wb(w_in))
    oa = _moba_attention(proj, bias, row(moba_out_g[0]), batch, seq)
    ob = _diff_attention(proj, bias, diff_lambda[0].astype(F32), row(diff_subln_g[0]), batch, seq)
    kc, vc = _memory_kv(mem2d, row(mem_norm_g[0]), wb(w_ck), wb(w_cv), mem_len)
    x2 = _outproj_cross(x2d, oa, ob, wb(w_out), row(cross_norm_g[0]), wb(w_cq), kc, vc, wb(w_co),
                        seq, mem_len)
    out = _swiglu_final(x2, row(ffn_norm_g[0]), wb(w_gate), wb(w_up), wb(w_down), row(final_norm_g))
    return out.reshape(batch, seq, d)
```

```python
import functools
import math

import numpy as np
import jax
import jax.numpy as jnp
from jax import lax
from jax.experimental import pallas as pl
from jax.experimental.pallas import tpu as pltpu

F32 = jnp.float32
BF16 = jnp.bfloat16

A_HEADS = 8
A_HEAD_DIM = 64
A_WIDTH = A_HEADS * A_HEAD_DIM
MOBA_BLOCK = 256
MOBA_TOPK = 3
B_HEADS = 4
B_QK_DIM = 64
B_V_DIM = 2 * B_QK_DIM
B_WIDTH = B_HEADS * B_V_DIM
MEM_HEADS = 4
REL_BUCKETS = 32
REL_MAX_DIST = 128
EPS = 1e-6
NEG_INF = -1e30
LAMBDA_INIT = 0.8 - 0.6 * math.exp(-0.3 * 0)
QK_SCALE = A_HEAD_DIM ** -0.5
LOG2E = math.log2(math.e)

LANES = 128
SUBLANES = 8
VMEM_LIMIT_BYTES = 56 * 1024 * 1024

BLK = MOBA_BLOCK
TM_PROJ = 1024
TM_CROSS = 512
FF_CHUNK = 256
PROJ_CHUNK = 512


def _dot(a, b):
    return jnp.dot(a, b, preferred_element_type=F32)


def _dot_nt(a, b):
    return lax.dot_general(a, b, (((1,), (1,)), ((), ())), preferred_element_type=F32)


def _rms(x, g):
    return x * lax.rsqrt(jnp.mean(x * x, axis=-1, keepdims=True) + EPS) * g


def _params(n_axes):
    return pltpu.CompilerParams(dimension_semantics=("arbitrary",) * n_axes,
                                vmem_limit_bytes=VMEM_LIMIT_BYTES)


def _resident(shape):
    return pl.BlockSpec(shape, lambda *_: (0,) * len(shape), pipeline_mode=pl.Buffered(1))


def _rel_bucket_np(dist):
    n = np.maximum(dist, 0)
    max_exact = REL_BUCKETS // 2
    ratio = np.maximum(n, max_exact).astype(np.float32) / np.float32(max_exact)
    log_ratio = np.log(ratio) / np.float32(math.log(REL_MAX_DIST / max_exact))
    large = max_exact + (log_ratio * np.float32(REL_BUCKETS - max_exact)).astype(np.int32)
    large = np.minimum(large, REL_BUCKETS - 1)
    return np.where(n < max_exact, n, large).astype(np.int32)


HALF = BLK // 2


def _bucket_tiles():
    i = np.arange(HALF)[:, None]
    j = np.arange(HALF)[None, :]
    band = np.where(i - j >= 0, _rel_bucket_np(i - j), -1)
    corner = _rel_bucket_np(HALF + i - j)
    return np.stack([band, corner]).astype(np.int32)


def _bias_kernel(tab_ref, idx_ref, o_ref):
    assert HALF == REL_MAX_DIST
    idx = idx_ref[...]
    zero = jnp.zeros((HALF, HALF), F32)
    lo, hi = slice(0, HALF), slice(HALF, BLK)
    for h in range(o_ref.shape[0]):
        far = tab_ref[REL_BUCKETS - 1, h]
        acc = jnp.zeros(idx.shape, F32)
        for b in range(REL_BUCKETS - 1):
            acc = jnp.where(idx == b, (tab_ref[b, h] - far) * LOG2E, acc)
        band = jnp.where(idx[0] < 0, NEG_INF, acc[0])
        corner = acc[1]
        o_ref[h, 0, lo, lo] = band
        o_ref[h, 0, lo, hi] = jnp.full((HALF, HALF), NEG_INF, F32)
        o_ref[h, 0, hi, lo] = corner
        o_ref[h, 0, hi, hi] = band
        o_ref[h, 1, lo, lo] = zero
        o_ref[h, 1, lo, hi] = corner
        o_ref[h, 1, hi, lo] = zero
        o_ref[h, 1, hi, hi] = zero


def _bias_tiles(table):
    n_heads = table.shape[1]
    idx = jnp.asarray(_bucket_tiles())
    return pl.pallas_call(
        _bias_kernel,
        in_specs=[pl.BlockSpec(memory_space=pltpu.SMEM),
                  pl.BlockSpec(memory_space=pltpu.VMEM)],
        out_specs=pl.BlockSpec(memory_space=pltpu.VMEM),
        out_shape=jax.ShapeDtypeStruct((n_heads, 2, BLK, BLK), F32),
        compiler_params=pltpu.CompilerParams(vmem_limit_bytes=VMEM_LIMIT_BYTES),
        name="rel_bias_tiles",
    )(table, idx)


def _inproj_kernel(x_ref, g_ref, w_ref, o_ref, *, q_chunks):
    hb = _rms(x_ref[...], g_ref[...]).astype(BF16)
    for j in range(w_ref.shape[1] // PROJ_CHUNK):
        cols = slice(j * PROJ_CHUNK, (j + 1) * PROJ_CHUNK)
        acc = _dot(hb, w_ref[:, cols])
        if j in q_chunks:
            acc = acc * (QK_SCALE * LOG2E)
        o_ref[:, cols] = acc.astype(BF16)


def _in_projection(x2d, g, w_bf16):
    t, d = x2d.shape
    n = w_bf16.shape[1]
    assert t % TM_PROJ == 0 and n % PROJ_CHUNK == 0
    q_chunks = (0, (3 * A_WIDTH) // PROJ_CHUNK)
    return pl.pallas_call(
        functools.partial(_inproj_kernel, q_chunks=q_chunks),
        grid=(t // TM_PROJ,),
        in_specs=[pl.BlockSpec((TM_PROJ, d), lambda i: (i, 0)),
                  _resident((1, d)),
                  _resident((d, n))],
        out_specs=pl.BlockSpec((TM_PROJ, n), lambda i: (i, 0)),
        out_shape=jax.ShapeDtypeStruct((t, n), BF16),
        compiler_params=_params(1),
        name="in_projection",
    )(x2d, g, w_bf16)


N_STREAMS = 8


def _half_mask(shape, half):
    lane = lax.broadcasted_iota(jnp.int32, shape, 1)
    return lax.shift_right_logical(lane, 6) == half


def _block_rows(n):
    return slice(n * BLK, (n + 1) * BLK)


def _two_pass_attention(j, q_diag, q_past, k_tile, v_tile, bias_tile, s_ref, on_done, diag_first):
    blocks = list(range(j, -1, -1))
    row_max = [None] * N_STREAMS
    results = [None] * N_STREAMS

    def score_step(h, n):
        def run():
            q = q_diag(h) if n == j else q_past(h)
            s = _dot_nt(q, k_tile(h, n))
            if n >= j - 1:
                s = s + bias_tile(h, j - n)
            s_ref[h, n] = s
            mx = jnp.maximum(s[:, :LANES], s[:, LANES:])
            mx = mx if row_max[h] is None else jnp.maximum(row_max[h], mx)
            if n == 0:
                mx = jnp.broadcast_to(jnp.max(mx, axis=1, keepdims=True), mx.shape)
            row_max[h] = mx
        return run

    def value_step(h, n):
        def run():
            m = row_max[h]
            p = jnp.concatenate([jnp.exp2(s_ref[h, n, :, :LANES] - m),
                                 jnp.exp2(s_ref[h, n, :, LANES:] - m)], axis=1)
            pv = _dot(p.astype(BF16), v_tile(h, n))
            results[h] = pv if results[h] is None else results[h] + pv
        return run

    lead = 2
    first = N_STREAMS if diag_first else lead
    rest = blocks[1:] if diag_first else blocks
    for h in range(first):
        score_step(h, j)()
    for h in range(lead):
        for n in blocks[1:]:
            score_step(h, n)()
    for h in range(N_STREAMS):
        ahead = rest if h + lead < N_STREAMS else []
        for k, n in enumerate(blocks):
            value_step(h, n)()
            if k < len(ahead):
                score_step(h + lead, ahead[k])()
        on_done(h, results[h])


def _moba_tile(j, q_ref, bias_ref, g_ref, o_ref, kmt_ref, kaug_ref, vaug_ref, qaug_ref, s_ref, topk):
    n_pairs = A_WIDTH // LANES

    def own_block_query(h):
        qp = q_ref[:, (h // 2) * LANES:(h // 2 + 1) * LANES]
        return jnp.where(_half_mask(qp.shape, h % 2), qp, jnp.zeros_like(qp))

    if j > topk:
        kmt = kmt_ref[...]
        kmt_hi = kmt.astype(BF16)
        kmt_lo = (kmt - kmt_hi.astype(F32)).astype(BF16)
        q_all = q_ref[...]
        gate = _dot_nt(kmt_hi, q_all) + _dot_nt(kmt_lo, q_all)
        slabs = [gate[n * A_HEADS:(n + 1) * A_HEADS, :] for n in range(j)]
        sel_rows = []
        for n in range(j):
            rank = jnp.zeros(slabs[n].shape, F32)
            for m in range(j):
                if m != n:
                    beats = (slabs[m] >= slabs[n]) if m < n else (slabs[m] > slabs[n])
                    rank = rank + jnp.where(beats, 1.0, 0.0)
            sel_rows.append(jnp.where(rank < topk, 0.0, NEG_INF))
        sel_rows.append(jnp.zeros(((8 - j) * A_HEADS, BLK), F32))
        sel_bias = jnp.concatenate(sel_rows + sel_rows, axis=0).T.astype(BF16)
        for p in range(n_pairs):
            qp = q_ref[:, p * LANES:(p + 1) * LANES]
            for half in range(2):
                qaug_ref[2 * p + half] = jnp.where(_half_mask(qp.shape, half), qp, sel_bias)
        past_query = lambda h: qaug_ref[h]
    else:
        past_query = own_block_query

    acc = [None] * N_STREAMS
    pairs = []

    def head_done(h, result):
        acc[h] = result
        if h % 2 == 1:
            a0, a1 = acc[h - 1], acc[h]
            first = _half_mask(a0.shape, 0)
            num = jnp.where(first, a0, a1)
            den = pltpu.roll(jnp.where(first, a1, a0), LANES // 2, 1)
            pairs.append(num / den)

    _two_pass_attention(
        j,
        q_diag=own_block_query,
        q_past=past_query,
        k_tile=lambda h, n: kaug_ref[h, _block_rows(n), :],
        v_tile=lambda h, n: vaug_ref[h, _block_rows(n), :],
        bias_tile=lambda h, which: bias_ref[h, which],
        s_ref=s_ref, on_done=head_done, diag_first=j > topk)

    o = jnp.concatenate(pairs, axis=1)
    o_ref[...] = _rms(o, g_ref[...]).astype(BF16)


def _moba_kernel(q_ref, k_ref, v_ref, bias_ref, g_ref, o_ref,
                 kmt_ref, kaug_ref, vaug_ref, qaug_ref, s_ref, *, topk):
    i = pl.program_id(1)
    nb = k_ref.shape[0] // BLK
    n_pairs = A_WIDTH // LANES

    def tile(j):
        _moba_tile(j, q_ref, bias_ref, g_ref, o_ref, kmt_ref, kaug_ref, vaug_ref, qaug_ref, s_ref, topk)

    @pl.when(i == 0)
    def _first_tile_of_batch():
        row = lax.broadcasted_iota(jnp.int32, (A_HEADS, A_WIDTH), 0)
        lane = lax.broadcasted_iota(jnp.int32, (A_HEADS, A_WIDTH), 1)
        head_mask = lax.shift_right_logical(lane, 6) == row
        kmt_ref[...] = jnp.zeros(kmt_ref.shape, F32)
        for n in range(nb):
            kb = k_ref[n * BLK:(n + 1) * BLK, :].astype(F32)
            km = jnp.sum(kb, axis=0, keepdims=True) * (1.0 / BLK)
            kmt_ref[n * A_HEADS:(n + 1) * A_HEADS, :] = jnp.where(head_mask, km, 0.0)
        lane = lax.broadcasted_iota(jnp.int32, (1, LANES), 1)
        for p in range(n_pairs):
            cols = slice(p * LANES, (p + 1) * LANES)
            for half in range(2):
                h = 2 * p + half
                own = lax.shift_right_logical(lane, 6) == half
                own_one = jnp.where(own, 1.0, 0.0).astype(BF16)
                vaug_ref[h] = v_ref[:, cols] * own_one + jnp.where(own, 0.0, 1.0).astype(BF16)
                for n in range(nb):
                    rows = slice(n * BLK, (n + 1) * BLK)
                    code = jnp.where((~own) & ((lane & 63) == n * A_HEADS + h), 1.0, 0.0).astype(BF16)
                    kaug_ref[h, rows, :] = k_ref[rows, cols] * own_one + code
        tile(0)

    for j in range(1, nb):
        pl.when(i == j)(functools.partial(tile, j))


def _moba_attention(proj, bias, g, batch, seq):
    nq = seq // BLK
    assert seq % BLK == 0 and nq <= 8 and A_HEADS == N_STREAMS
    topk = min(MOBA_TOPK, nq)
    return pl.pallas_call(
        functools.partial(_moba_kernel, topk=topk),
        grid=(batch, nq),
        in_specs=[pl.BlockSpec((BLK, A_WIDTH), lambda b, i: (b * nq + i, 0)),
                  pl.BlockSpec((seq, A_WIDTH), lambda b, i: (b, 1)),
                  pl.BlockSpec((seq, A_WIDTH), lambda b, i: (b, 2)),
                  pl.BlockSpec((A_HEADS, 2, BLK, BLK), lambda b, i: (0, 0, 0, 0),
                               pipeline_mode=pl.Buffered(1)),
                  _resident((1, A_WIDTH))],
        out_specs=pl.BlockSpec((BLK, A_WIDTH), lambda b, i: (b * nq + i, 0)),
        out_shape=jax.ShapeDtypeStruct((batch * seq, A_WIDTH), BF16),
        scratch_shapes=[pltpu.VMEM((8 * A_HEADS, A_WIDTH), F32),
                        pltpu.VMEM((A_HEADS, seq, LANES), BF16),
                        pltpu.VMEM((A_HEADS, seq, LANES), BF16),
                        pltpu.VMEM((A_HEADS, BLK, LANES), BF16),
                        pltpu.VMEM((A_HEADS, nq, BLK, BLK), F32)],
        compiler_params=_params(2),
        name="moba_attention",
    )(proj, proj, proj, bias, g)


def _diff_tile(j, q_ref, k_ref, bias_ref, lam_ref, g_ref, o_ref, vaug_ref, s_ref):
    def map_query(s):
        qp = q_ref[:, (s // 2) * LANES:(s // 2 + 1) * LANES]
        return jnp.where(_half_mask(qp.shape, s % 2), qp, jnp.zeros_like(qp))

    lp = lam_ref[...]
    lam = (jnp.exp(jnp.sum(lp[0:1] * lp[1:2], axis=1, keepdims=True))
           - jnp.exp(jnp.sum(lp[2:3] * lp[3:4], axis=1, keepdims=True)) + LAMBDA_INIT)
    acc = [None] * N_STREAMS

    def map_done(s, result):
        acc[s] = result
        if s % 2 == 1:
            h = s // 2
            a0, a1 = acc[s - 1], acc[s]
            n0, l0 = a0[:, :B_V_DIM], a0[:, B_V_DIM:]
            n1, l1 = a1[:, :B_V_DIM], a1[:, B_V_DIM:]
            o = (n0 * l1 - lam * (n1 * l0)) / (l0 * l1)
            o_ref[:, h * B_V_DIM:(h + 1) * B_V_DIM] = (
                _rms(o, g_ref[...]) * (1.0 - LAMBDA_INIT)).astype(BF16)

    _two_pass_attention(
        j,
        q_diag=map_query,
        q_past=map_query,
        k_tile=lambda s, n: k_ref[_block_rows(n), (s // 2) * LANES:(s // 2 + 1) * LANES],
        v_tile=lambda s, n: vaug_ref[s // 2, _block_rows(n), :],
        bias_tile=lambda s, which: bias_ref[s // 2, which],
        s_ref=s_ref, on_done=map_done, diag_first=False)


def _diff_kernel(q_ref, k_ref, v_ref, bias_ref, lam_ref, g_ref, o_ref, vaug_ref, s_ref):
    i = pl.program_id(1)

    def tile(j):
        _diff_tile(j, q_ref, k_ref, bias_ref, lam_ref, g_ref, o_ref, vaug_ref, s_ref)

    @pl.when(i == 0)
    def _first_tile_of_batch():
        for h in range(B_HEADS):
            vaug_ref[h, :, :B_V_DIM] = v_ref[:, h * B_V_DIM:(h + 1) * B_V_DIM]
            vaug_ref[h, :, B_V_DIM:] = jnp.ones((v_ref.shape[0], LANES), BF16)
        tile(0)

    for j in range(1, k_ref.shape[0] // BLK):
        pl.when(i == j)(functools.partial(tile, j))


def _diff_attention(proj, bias, lam, g, batch, seq):
    nq = seq // BLK
    assert 2 * B_HEADS == N_STREAMS and B_V_DIM == LANES
    first = (3 * A_WIDTH) // B_WIDTH
    return pl.pallas_call(
        _diff_kernel,
        grid=(batch, nq),
        in_specs=[pl.BlockSpec((BLK, B_WIDTH), lambda b, i: (b * nq + i, first)),
                  pl.BlockSpec((seq, B_WIDTH), lambda b, i: (b, first + 1)),
                  pl.BlockSpec((seq, B_WIDTH), lambda b, i: (b, first + 2)),
                  pl.BlockSpec((B_HEADS, 2, BLK, BLK), lambda b, i: (A_HEADS // B_HEADS, 0, 0, 0),
                               pipeline_mode=pl.Buffered(1)),
                  _resident(lam.shape),
                  _resident((1, B_V_DIM))],
        out_specs=pl.BlockSpec((BLK, B_WIDTH), lambda b, i: (b * nq + i, 0)),
        out_shape=jax.ShapeDtypeStruct((batch * seq, B_WIDTH), BF16),
        scratch_shapes=[pltpu.VMEM((B_HEADS, seq, 2 * LANES), BF16),
                        pltpu.VMEM((N_STREAMS, nq, BLK, BLK), F32)],
        compiler_params=_params(2),
        name="diff_attention",
    )(proj, proj, proj, bias, lam, g)


def _memkv_kernel(mem_ref, g_ref, wk_ref, wv_ref, k_ref, v_ref):
    m = _rms(mem_ref[...], g_ref[...]).astype(BF16)
    k_ref[...] = _dot(m, wk_ref[...]).astype(BF16)
    v_ref[...] = _dot(m, wv_ref[...]).astype(BF16)


def _memory_kv(mem2d, g, wk, wv, mem_len):
    t, d = mem2d.shape
    n = wk.shape[1]
    return pl.pallas_call(
        _memkv_kernel,
        grid=(t // mem_len,),
        in_specs=[pl.BlockSpec((mem_len, d), lambda b: (b, 0)),
                  _resident((1, d)), _resident((d, n)), _resident((d, n))],
        out_specs=[pl.BlockSpec((mem_len, n), lambda b: (b, 0))] * 2,
        out_shape=[jax.ShapeDtypeStruct((t, n), BF16)] * 2,
        compiler_params=_params(1),
        name="memory_kv",
    )(mem2d, g, wk, wv)


def _cross_kernel(x_ref, oa_ref, ob_ref, wo_ref, g_ref, wq_ref, kc_ref, vc_ref, wco_ref, o_ref,
                  *, scale):
    x1 = (x_ref[...] + _dot(oa_ref[...], wo_ref[:A_WIDTH, :]) + _dot(ob_ref[...], wo_ref[A_WIDTH:, :]))
    hb = _rms(x1, g_ref[...]).astype(BF16)
    q = (_dot(hb, wq_ref[...]) * scale).astype(BF16)
    hd = q.shape[1] // MEM_HEADS
    heads = []
    for h in range(MEM_HEADS):
        cols = slice(h * hd, (h + 1) * hd)
        s = _dot_nt(q[:, cols], kc_ref[:, cols])
        p = jnp.exp2(s - jnp.max(s, axis=1, keepdims=True))
        l = jnp.sum(p, axis=1, keepdims=True)
        heads.append((_dot(p.astype(BF16), vc_ref[:, cols]) / l).astype(BF16))
    o = jnp.concatenate(heads, axis=1)
    o_ref[...] = x1 + _dot(o, wco_ref[...])


def _outproj_cross(x2d, oa, ob, wo, g, wq, kc, vc, wco, seq, mem_len):
    t, d = x2d.shape
    per_seq = seq // TM_CROSS
    assert seq % TM_CROSS == 0
    hd = wq.shape[1] // MEM_HEADS
    scale = hd ** -0.5 * LOG2E
    return pl.pallas_call(
        functools.partial(_cross_kernel, scale=scale),
        grid=(t // TM_CROSS,),
        in_specs=[pl.BlockSpec((TM_CROSS, d), lambda i: (i, 0)),
                  pl.BlockSpec((TM_CROSS, A_WIDTH), lambda i: (i, 0)),
                  pl.BlockSpec((TM_CROSS, B_WIDTH), lambda i: (i, 0)),
                  _resident(wo.shape), _resident((1, d)), _resident(wq.shape),
                  pl.BlockSpec((mem_len, kc.shape[1]), lambda i: (i // per_seq, 0)),
                  pl.BlockSpec((mem_len, vc.shape[1]), lambda i: (i // per_seq, 0)),
                  _resident(wco.shape)],
        out_specs=pl.BlockSpec((TM_CROSS, d), lambda i: (i, 0)),
        out_shape=jax.ShapeDtypeStruct((t, d), F32),
        compiler_params=_params(1),
        name="outproj_cross_attention",
    )(x2d, oa, ob, wo, g, wq, kc, vc, wco)


def _ffn_kernel(x_ref, g_ref, wg_ref, wu_ref, wd_ref, gf_ref, o_ref, acc_ref):
    x = x_ref[...]
    hb = _rms(x, g_ref[...]).astype(BF16)
    acc_ref[...] = x
    for c in range(wg_ref.shape[1] // FF_CHUNK):
        cols = slice(c * FF_CHUNK, (c + 1) * FF_CHUNK)
        a = (jax.nn.silu(_dot(hb, wg_ref[:, cols])) * _dot(hb, wu_ref[:, cols])).astype(BF16)
        acc_ref[...] += _dot(a, wd_ref[cols, :])
    o_ref[...] = _rms(acc_ref[...], gf_ref[...])


def _swiglu_final(x2d, g, wg, wu, wd, gf):
    t, d = x2d.shape
    assert t % TM_PROJ == 0 and wg.shape[1] % FF_CHUNK == 0
    return pl.pallas_call(
        _ffn_kernel,
        grid=(t // TM_PROJ,),
        in_specs=[pl.BlockSpec((TM_PROJ, d), lambda i: (i, 0)),
                  _resident((1, d)), _resident(wg.shape), _resident(wu.shape), _resident(wd.shape),
                  _resident((1, d))],
        out_specs=pl.BlockSpec((TM_PROJ, d), lambda i: (i, 0)),
        out_shape=jax.ShapeDtypeStruct((t, d), F32),
        scratch_shapes=[pltpu.VMEM((TM_PROJ, d), F32)],
        compiler_params=_params(1),
        name="swiglu_final_norm",
    )(x2d, g, wg, wu, wd, gf)


def kernel(x, mem, mix_norm_g, w_in, moba_out_g, diff_lambda, diff_subln_g, w_out, rel_bias_table,
           cross_norm_g, mem_norm_g, w_cq, w_ck, w_cv, w_co, ffn_norm_g, w_gate, w_up, w_down,
           final_norm_g):
    batch, seq, d = x.shape
    mem_len = mem.shape[1]
    assert mix_norm_g.shape[0] == 1, "single-layer trunk"
    x2d = x.reshape(batch * seq, d)
    mem2d = mem.reshape(batch * mem_len, d)
    row = lambda v: v.reshape(1, -1).astype(F32)
    wb = lambda w: w[0].astype(BF16)

    bias = _bias_tiles(rel_bias_table.astype(F32))
    proj = _in_projection(x2d, row(mix_norm_g[0]), wb(w_in))
    oa = _moba_attention(proj, bias, row(moba_out_g[0]), batch, seq)
    ob = _diff_attention(proj, bias, diff_lambda[0].astype(F32), row(diff_subln_g[0]), batch, seq)
    kc, vc = _memory_kv(mem2d, row(mem_norm_g[0]), wb(w_ck), wb(w_cv), mem_len)
    x2 = _outproj_cross(x2d, oa, ob, wb(w_out), row(cross_norm_g[0]), wb(w_cq), kc, vc, wb(w_co),
                        seq, mem_len)
    out = _swiglu_final(x2, row(ffn_norm_g[0]), wb(w_gate), wb(w_up), wb(w_down), row(final_norm_g))
    return out.reshape(batch, seq, d)
```

```python
import functools
import math
from typing import Callable, NamedTuple

import numpy as np
import jax
import jax.numpy as jnp
from jax import lax
from jax.experimental import pallas as pl
from jax.experimental.pallas import tpu as pltpu

F32 = jnp.float32
BF16 = jnp.bfloat16

A_HEADS = 8
A_HEAD_DIM = 64
A_WIDTH = A_HEADS * A_HEAD_DIM
MOBA_BLOCK = 256
MOBA_TOPK = 3
B_HEADS = 4
B_QK_DIM = 64
B_V_DIM = 2 * B_QK_DIM
B_WIDTH = B_HEADS * B_V_DIM
MEM_HEADS = 4
REL_BUCKETS = 32
REL_MAX_DIST = 128
EPS = 1e-6
NEG_INF = -1e30
LAMBDA_INIT = 0.8 - 0.6 * math.exp(-0.3 * 0)
QK_SCALE = A_HEAD_DIM ** -0.5
LOG2E = math.log2(math.e)

LANES = 128
SUBLANES = 8
VMEM_LIMIT_BYTES = 56 * 1024 * 1024

BLK = MOBA_BLOCK
TM_PROJ = 1024
TM_CROSS = 512
FF_CHUNK = 256
PROJ_CHUNK = 512


def _dot(a, b):
    return jnp.dot(a, b, preferred_element_type=F32)


def _dot_nt(a, b):
    return lax.dot_general(a, b, (((1,), (1,)), ((), ())), preferred_element_type=F32)


def _rms(x, g):
    return x * lax.rsqrt(jnp.mean(x * x, axis=-1, keepdims=True) + EPS) * g


def _params(n_axes):
    return pltpu.CompilerParams(dimension_semantics=("arbitrary",) * n_axes,
                                vmem_limit_bytes=VMEM_LIMIT_BYTES)


def _resident(shape):
    return pl.BlockSpec(shape, lambda *_: (0,) * len(shape), pipeline_mode=pl.Buffered(1))


def _rel_bucket_np(dist):
    n = np.maximum(dist, 0)
    max_exact = REL_BUCKETS // 2
    ratio = np.maximum(n, max_exact).astype(np.float32) / np.float32(max_exact)
    log_ratio = np.log(ratio) / np.float32(math.log(REL_MAX_DIST / max_exact))
    large = max_exact + (log_ratio * np.float32(REL_BUCKETS - max_exact)).astype(np.int32)
    large = np.minimum(large, REL_BUCKETS - 1)
    return np.where(n < max_exact, n, large).astype(np.int32)


HALF = BLK // 2


def _bucket_tiles():
    i = np.arange(HALF)[:, None]
    j = np.arange(HALF)[None, :]
    band = np.where(i - j >= 0, _rel_bucket_np(i - j), -1)
    corner = _rel_bucket_np(HALF + i - j)
    return np.stack([band, corner]).astype(np.int32)


def _bias_kernel(tab_ref, idx_ref, o_ref):
    assert HALF == REL_MAX_DIST
    idx = idx_ref[...]
    zero = jnp.zeros((HALF, HALF), F32)
    lo, hi = slice(0, HALF), slice(HALF, BLK)
    for h in range(o_ref.shape[0]):
        far = tab_ref[REL_BUCKETS - 1, h]
        acc = jnp.zeros(idx.shape, F32)
        for b in range(REL_BUCKETS - 1):
            acc = jnp.where(idx == b, (tab_ref[b, h] - far) * LOG2E, acc)
        band = jnp.where(idx[0] < 0, NEG_INF, acc[0])
        corner = acc[1]
        o_ref[h, 0, lo, lo] = band
        o_ref[h, 0, lo, hi] = jnp.full((HALF, HALF), NEG_INF, F32)
        o_ref[h, 0, hi, lo] = corner
        o_ref[h, 0, hi, hi] = band
        o_ref[h, 1, lo, lo] = zero
        o_ref[h, 1, lo, hi] = corner
        o_ref[h, 1, hi, lo] = zero
        o_ref[h, 1, hi, hi] = zero


def _bias_tiles(table):
    n_heads = table.shape[1]
    idx = jnp.asarray(_bucket_tiles())
    return pl.pallas_call(
        _bias_kernel,
        in_specs=[pl.BlockSpec(memory_space=pltpu.SMEM),
                  pl.BlockSpec(memory_space=pltpu.VMEM)],
        out_specs=pl.BlockSpec(memory_space=pltpu.VMEM),
        out_shape=jax.ShapeDtypeStruct((n_heads, 2, BLK, BLK), F32),
        compiler_params=pltpu.CompilerParams(vmem_limit_bytes=VMEM_LIMIT_BYTES),
        name="rel_bias_tiles",
    )(table, idx)


def _inproj_kernel(x_ref, g_ref, w_ref, o_ref, *, q_chunks):
    hb = _rms(x_ref[...], g_ref[...]).astype(BF16)
    for j in range(w_ref.shape[1] // PROJ_CHUNK):
        cols = slice(j * PROJ_CHUNK, (j + 1) * PROJ_CHUNK)
        acc = _dot(hb, w_ref[:, cols])
        if j in q_chunks:
            acc = acc * (QK_SCALE * LOG2E)
        o_ref[:, cols] = acc.astype(BF16)


def _in_projection(x2d, g, w_bf16):
    t, d = x2d.shape
    n = w_bf16.shape[1]
    assert t % TM_PROJ == 0 and n % PROJ_CHUNK == 0
    q_chunks = (0, (3 * A_WIDTH) // PROJ_CHUNK)
    return pl.pallas_call(
        functools.partial(_inproj_kernel, q_chunks=q_chunks),
        grid=(t // TM_PROJ,),
        in_specs=[pl.BlockSpec((TM_PROJ, d), lambda i: (i, 0)),
                  _resident((1, d)),
                  _resident((d, n))],
        out_specs=pl.BlockSpec((TM_PROJ, n), lambda i: (i, 0)),
        out_shape=jax.ShapeDtypeStruct((t, n), BF16),
        compiler_params=_params(1),
        name="in_projection",
    )(x2d, g, w_bf16)


def _half_mask(shape, half):
    lane = lax.broadcasted_iota(jnp.int32, shape, 1)
    return lax.shift_right_logical(lane, 6) == half


def _block_rows(n):
    return slice(n * BLK, (n + 1) * BLK)


class _Stream(NamedTuple):
    j: int
    slot: int
    q_diag: Callable
    q_past: Callable
    k_tile: Callable
    v_tile: Callable
    bias_tile: Callable
    done: Callable


def _two_pass_attention(streams, s_ref, n_diag_first):
    lead = 2
    row_max = [None] * len(streams)
    results = [None] * len(streams)

    def blocks(i):
        return list(range(streams[i].j, -1, -1))

    def score_step(i, n):
        st = streams[i]
        s = _dot_nt(st.q_diag() if n == st.j else st.q_past(), st.k_tile(n))
        if n >= st.j - 1:
            s = s + st.bias_tile(st.j - n)
        s_ref[st.slot, n] = s
        mx = jnp.maximum(s[:, :LANES], s[:, LANES:])
        mx = mx if row_max[i] is None else jnp.maximum(row_max[i], mx)
        if n == 0:
            mx = jnp.broadcast_to(jnp.max(mx, axis=1, keepdims=True), mx.shape)
        row_max[i] = mx

    def value_step(i, n):
        st = streams[i]
        m = row_max[i]
        p = jnp.concatenate([jnp.exp2(s_ref[st.slot, n, :, :LANES] - m),
                             jnp.exp2(s_ref[st.slot, n, :, LANES:] - m)], axis=1)
        pv = _dot(p.astype(BF16), st.v_tile(n))
        results[i] = pv if results[i] is None else results[i] + pv

    early = max(lead, n_diag_first)
    for i in range(early):
        score_step(i, streams[i].j)
    for i in range(lead):
        for n in blocks(i)[1:]:
            score_step(i, n)
    for i in range(len(streams)):
        a = i + lead
        ahead = [] if a >= len(streams) else blocks(a)[1:] if a < early else blocks(a)
        mine = blocks(i)
        for k in range(max(len(mine), len(ahead))):
            if k < len(mine):
                value_step(i, mine[k])
            if k < len(ahead):
                score_step(a, ahead[k])
        streams[i].done(results[i])


def _moba_streams(j, rows, q_ref, bias_ref, g_ref, o_ref, kmt_ref, kaug_ref, vaug_ref, qaug_ref, topk):
    n_pairs = A_WIDTH // LANES

    def own_block_query(h):
        qp = q_ref[rows, (h // 2) * LANES:(h // 2 + 1) * LANES]
        return jnp.where(_half_mask(qp.shape, h % 2), qp, jnp.zeros_like(qp))

    if j > topk:
        kmt = kmt_ref[...]
        kmt_hi = kmt.astype(BF16)
        kmt_lo = (kmt - kmt_hi.astype(F32)).astype(BF16)
        q_all = q_ref[rows, :]
        gate = _dot_nt(kmt_hi, q_all) + _dot_nt(kmt_lo, q_all)
        slabs = [gate[n * A_HEADS:(n + 1) * A_HEADS, :] for n in range(j)]
        sel_rows = []
        for n in range(j):
            rank = jnp.zeros(slabs[n].shape, F32)
            for m in range(j):
                if m != n:
                    beats = (slabs[m] >= slabs[n]) if m < n else (slabs[m] > slabs[n])
                    rank = rank + jnp.where(beats, 1.0, 0.0)
            sel_rows.append(jnp.where(rank < topk, 0.0, NEG_INF))
        sel_rows.append(jnp.zeros(((8 - j) * A_HEADS, BLK), F32))
        sel_bias = jnp.concatenate(sel_rows + sel_rows, axis=0).T.astype(BF16)
        base = (j % 2) * A_HEADS
        for p in range(n_pairs):
            qp = q_ref[rows, p * LANES:(p + 1) * LANES]
            for half in range(2):
                qaug_ref[base + 2 * p + half] = jnp.where(_half_mask(qp.shape, half), qp, sel_bias)
        past_query = lambda h: qaug_ref[base + h]
    else:
        past_query = own_block_query

    acc = [None] * A_HEADS
    pairs = []

    def head_done(h, result):
        acc[h] = result
        if h % 2 == 1:
            a0, a1 = acc[h - 1], acc[h]
            first = _half_mask(a0.shape, 0)
            num = jnp.where(first, a0, a1)
            den = pltpu.roll(jnp.where(first, a1, a0), LANES // 2, 1)
            pairs.append(num / den)
        if len(pairs) == n_pairs:
            o_ref[rows, :] = _rms(jnp.concatenate(pairs, axis=1), g_ref[...]).astype(BF16)

    return [_Stream(j=j, slot=h,
                    q_diag=functools.partial(own_block_query, h),
                    q_past=functools.partial(past_query, h),
                    k_tile=lambda n, h=h: kaug_ref[h, _block_rows(n), :],
                    v_tile=lambda n, h=h: vaug_ref[h, _block_rows(n), :],
                    bias_tile=lambda which, h=h: bias_ref[h, which],
                    done=functools.partial(head_done, h))
            for h in range(A_HEADS)]


def _moba_kernel(q_ref, k_ref, v_ref, bias_ref, g_ref, o_ref,
                 kmt_ref, kaug_ref, vaug_ref, qaug_ref, s_ref, *, topk):
    t = pl.program_id(1)
    nb = k_ref.shape[0] // BLK
    n_pairs = A_WIDTH // LANES

    def tiles(t_static):
        streams = []
        for j in (2 * t_static, 2 * t_static + 1):
            rows = slice((j % 2) * BLK, (j % 2 + 1) * BLK)
            streams += _moba_streams(j, rows, q_ref, bias_ref, g_ref, o_ref,
                                     kmt_ref, kaug_ref, vaug_ref, qaug_ref, topk)
        _two_pass_attention(streams, s_ref, n_diag_first=A_HEADS if 2 * t_static > topk else 0)

    @pl.when(t == 0)
    def _first_tiles_of_batch():
        row = lax.broadcasted_iota(jnp.int32, (A_HEADS, A_WIDTH), 0)
        lane = lax.broadcasted_iota(jnp.int32, (A_HEADS, A_WIDTH), 1)
        head_mask = lax.shift_right_logical(lane, 6) == row
        kmt_ref[...] = jnp.zeros(kmt_ref.shape, F32)
        for n in range(nb):
            kb = k_ref[n * BLK:(n + 1) * BLK, :].astype(F32)
            km = jnp.sum(kb, axis=0, keepdims=True) * (1.0 / BLK)
            kmt_ref[n * A_HEADS:(n + 1) * A_HEADS, :] = jnp.where(head_mask, km, 0.0)
        lane = lax.broadcasted_iota(jnp.int32, (1, LANES), 1)
        for p in range(n_pairs):
            cols = slice(p * LANES, (p + 1) * LANES)
            for half in range(2):
                h = 2 * p + half
                own = lax.shift_right_logical(lane, 6) == half
                own_one = jnp.where(own, 1.0, 0.0).astype(BF16)
                vaug_ref[h] = v_ref[:, cols] * own_one + jnp.where(own, 0.0, 1.0).astype(BF16)
                for n in range(nb):
                    rows = slice(n * BLK, (n + 1) * BLK)
                    code = jnp.where((~own) & ((lane & 63) == n * A_HEADS + h), 1.0, 0.0).astype(BF16)
                    kaug_ref[h, rows, :] = k_ref[rows, cols] * own_one + code
        tiles(0)

    for t_static in range(1, nb // 2):
        pl.when(t == t_static)(functools.partial(tiles, t_static))


def _moba_attention(proj, bias, g, batch, seq):
    nq = seq // BLK
    assert seq % (2 * BLK) == 0 and nq <= 8
    topk = min(MOBA_TOPK, nq)
    steps = nq // 2
    return pl.pallas_call(
        functools.partial(_moba_kernel, topk=topk),
        grid=(batch, steps),
        in_specs=[pl.BlockSpec((2 * BLK, A_WIDTH), lambda b, t: (b * steps + t, 0)),
                  pl.BlockSpec((seq, A_WIDTH), lambda b, t: (b, 1)),
                  pl.BlockSpec((seq, A_WIDTH), lambda b, t: (b, 2)),
                  pl.BlockSpec((A_HEADS, 2, BLK, BLK), lambda b, t: (0, 0, 0, 0),
                               pipeline_mode=pl.Buffered(1)),
                  _resident((1, A_WIDTH))],
        out_specs=pl.BlockSpec((2 * BLK, A_WIDTH), lambda b, t: (b * steps + t, 0)),
        out_shape=jax.ShapeDtypeStruct((batch * seq, A_WIDTH), BF16),
        scratch_shapes=[pltpu.VMEM((8 * A_HEADS, A_WIDTH), F32),
                        pltpu.VMEM((A_HEADS, seq, LANES), BF16),
                        pltpu.VMEM((A_HEADS, seq, LANES), BF16),
                        pltpu.VMEM((2 * A_HEADS, BLK, LANES), BF16),
                        pltpu.VMEM((A_HEADS, nq, BLK, BLK), F32)],
        compiler_params=_params(2),
        name="moba_attention",
    )(proj, proj, proj, bias, g)


def _diff_streams(j, rows, q_ref, k_ref, bias_ref, lam, g_ref, o_ref, vaug_ref):
    def map_query(s):
        qp = q_ref[rows, (s // 2) * LANES:(s // 2 + 1) * LANES]
        return jnp.where(_half_mask(qp.shape, s % 2), qp, jnp.zeros_like(qp))

    acc = [None] * (2 * B_HEADS)

    def map_done(s, result):
        acc[s] = result
        if s % 2 == 1:
            h = s // 2
            a0, a1 = acc[s - 1], acc[s]
            n0, l0 = a0[:, :B_V_DIM], a0[:, B_V_DIM:]
            n1, l1 = a1[:, :B_V_DIM], a1[:, B_V_DIM:]
            o = (n0 * l1 - lam * (n1 * l0)) / (l0 * l1)
            o_ref[rows, h * B_V_DIM:(h + 1) * B_V_DIM] = (
                _rms(o, g_ref[...]) * (1.0 - LAMBDA_INIT)).astype(BF16)

    return [_Stream(j=j, slot=s,
                    q_diag=functools.partial(map_query, s),
                    q_past=functools.partial(map_query, s),
                    k_tile=lambda n, s=s: k_ref[_block_rows(n), (s // 2) * LANES:(s // 2 + 1) * LANES],
                    v_tile=lambda n, s=s: vaug_ref[s // 2, _block_rows(n), :],
                    bias_tile=lambda which, s=s: bias_ref[s // 2, which],
                    done=functools.partial(map_done, s))
            for s in range(2 * B_HEADS)]


def _diff_kernel(q_ref, k_ref, v_ref, bias_ref, lam_ref, g_ref, o_ref, vaug_ref, s_ref):
    t = pl.program_id(1)

    def tiles(t_static):
        lp = lam_ref[...]
        lam = (jnp.exp(jnp.sum(lp[0:1] * lp[1:2], axis=1, keepdims=True))
               - jnp.exp(jnp.sum(lp[2:3] * lp[3:4], axis=1, keepdims=True)) + LAMBDA_INIT)
        streams = []
        for j in (2 * t_static, 2 * t_static + 1):
            rows = slice((j % 2) * BLK, (j % 2 + 1) * BLK)
            streams += _diff_streams(j, rows, q_ref, k_ref, bias_ref, lam, g_ref, o_ref, vaug_ref)
        _two_pass_attention(streams, s_ref, n_diag_first=0)

    @pl.when(t == 0)
    def _first_tiles_of_batch():
        for h in range(B_HEADS):
            vaug_ref[h, :, :B_V_DIM] = v_ref[:, h * B_V_DIM:(h + 1) * B_V_DIM]
            vaug_ref[h, :, B_V_DIM:] = jnp.ones((v_ref.shape[0], LANES), BF16)
        tiles(0)

    for t_static in range(1, k_ref.shape[0] // (2 * BLK)):
        pl.when(t == t_static)(functools.partial(tiles, t_static))


def _diff_attention(proj, bias, lam, g, batch, seq):
    nq = seq // BLK
    assert seq % (2 * BLK) == 0 and B_V_DIM == LANES
    steps = nq // 2
    first = (3 * A_WIDTH) // B_WIDTH
    return pl.pallas_call(
        _diff_kernel,
        grid=(batch, steps),
        in_specs=[pl.BlockSpec((2 * BLK, B_WIDTH), lambda b, t: (b * steps + t, first)),
                  pl.BlockSpec((seq, B_WIDTH), lambda b, t: (b, first + 1)),
                  pl.BlockSpec((seq, B_WIDTH), lambda b, t: (b, first + 2)),
                  pl.BlockSpec((B_HEADS, 2, BLK, BLK), lambda b, t: (A_HEADS // B_HEADS, 0, 0, 0),
                               pipeline_mode=pl.Buffered(1)),
                  _resident(lam.shape),
                  _resident((1, B_V_DIM))],
        out_specs=pl.BlockSpec((2 * BLK, B_WIDTH), lambda b, t: (b * steps + t, 0)),
        out_shape=jax.ShapeDtypeStruct((batch * seq, B_WIDTH), BF16),
        scratch_shapes=[pltpu.VMEM((B_HEADS, seq, 2 * LANES), BF16),
                        pltpu.VMEM((2 * B_HEADS, nq, BLK, BLK), F32)],
        compiler_params=_params(2),
        name="diff_attention",
    )(proj, proj, proj, bias, lam, g)


def _memkv_kernel(mem_ref, g_ref, wk_ref, wv_ref, k_ref, v_ref):
    m = _rms(mem_ref[...], g_ref[...]).astype(BF16)
    k_ref[...] = _dot(m, wk_ref[...]).astype(BF16)
    v_ref[...] = _dot(m, wv_ref[...]).astype(BF16)


def _memory_kv(mem2d, g, wk, wv, mem_len):
    t, d = mem2d.shape
    n = wk.shape[1]
    return pl.pallas_call(
        _memkv_kernel,
        grid=(t // mem_len,),
        in_specs=[pl.BlockSpec((mem_len, d), lambda b: (b, 0)),
                  _resident((1, d)), _resident((d, n)), _resident((d, n))],
        out_specs=[pl.BlockSpec((mem_len, n), lambda b: (b, 0))] * 2,
        out_shape=[jax.ShapeDtypeStruct((t, n), BF16)] * 2,
        compiler_params=_params(1),
        name="memory_kv",
    )(mem2d, g, wk, wv)


def _cross_kernel(x_ref, oa_ref, ob_ref, wo_ref, g_ref, wq_ref, kc_ref, vc_ref, wco_ref, o_ref,
                  *, scale):
    x1 = (x_ref[...] + _dot(oa_ref[...], wo_ref[:A_WIDTH, :]) + _dot(ob_ref[...], wo_ref[A_WIDTH:, :]))
    hb = _rms(x1, g_ref[...]).astype(BF16)
    q = (_dot(hb, wq_ref[...]) * scale).astype(BF16)
    hd = q.shape[1] // MEM_HEADS
    heads = []
    for h in range(MEM_HEADS):
        cols = slice(h * hd, (h + 1) * hd)
        s = _dot_nt(q[:, cols], kc_ref[:, cols])
        p = jnp.exp2(s - jnp.max(s, axis=1, keepdims=True))
        l = jnp.sum(p, axis=1, keepdims=True)
        heads.append((_dot(p.astype(BF16), vc_ref[:, cols]) / l).astype(BF16))
    o = jnp.concatenate(heads, axis=1)
    o_ref[...] = x1 + _dot(o, wco_ref[...])


def _outproj_cross(x2d, oa, ob, wo, g, wq, kc, vc, wco, seq, mem_len):
    t, d = x2d.shape
    per_seq = seq // TM_CROSS
    assert seq % TM_CROSS == 0
    hd = wq.shape[1] // MEM_HEADS
    scale = hd ** -0.5 * LOG2E
    return pl.pallas_call(
        functools.partial(_cross_kernel, scale=scale),
        grid=(t // TM_CROSS,),
        in_specs=[pl.BlockSpec((TM_CROSS, d), lambda i: (i, 0)),
                  pl.BlockSpec((TM_CROSS, A_WIDTH), lambda i: (i, 0)),
                  pl.BlockSpec((TM_CROSS, B_WIDTH), lambda i: (i, 0)),
                  _resident(wo.shape), _resident((1, d)), _resident(wq.shape),
                  pl.BlockSpec((mem_len, kc.shape[1]), lambda i: (i // per_seq, 0)),
                  pl.BlockSpec((mem_len, vc.shape[1]), lambda i: (i // per_seq, 0)),
                  _resident(wco.shape)],
        out_specs=pl.BlockSpec((TM_CROSS, d), lambda i: (i, 0)),
        out_shape=jax.ShapeDtypeStruct((t, d), F32),
        compiler_params=_params(1),
        name="outproj_cross_attention",
    )(x2d, oa, ob, wo, g, wq, kc, vc, wco)


def _ffn_kernel(x_ref, g_ref, wg_ref, wu_ref, wd_ref, gf_ref, o_ref, acc_ref):
    x = x_ref[...]
    hb = _rms(x, g_ref[...]).astype(BF16)
    acc_ref[...] = x
    for c in range(wg_ref.shape[1] // FF_CHUNK):
        cols = slice(c * FF_CHUNK, (c + 1) * FF_CHUNK)
        a = (jax.nn.silu(_dot(hb, wg_ref[:, cols])) * _dot(hb, wu_ref[:, cols])).astype(BF16)
        acc_ref[...] += _dot(a, wd_ref[cols, :])
    o_ref[...] = _rms(acc_ref[...], gf_ref[...])


def _swiglu_final(x2d, g, wg, wu, wd, gf):
    t, d = x2d.shape
    assert t % TM_PROJ == 0 and wg.shape[1] % FF_CHUNK == 0
    return pl.pallas_call(
        _ffn_kernel,
        grid=(t // TM_PROJ,),
        in_specs=[pl.BlockSpec((TM_PROJ, d), lambda i: (i, 0)),
                  _resident((1, d)), _resident(wg.shape), _resident(wu.shape), _resident(wd.shape),
                  _resident((1, d))],
        out_specs=pl.BlockSpec((TM_PROJ, d), lambda i: (i, 0)),
        out_shape=jax.ShapeDtypeStruct((t, d), F32),
        scratch_shapes=[pltpu.VMEM((TM_PROJ, d), F32)],
        compiler_params=_params(1),
        name="swiglu_final_norm",
    )(x2d, g, wg, wu, wd, gf)


def kernel(x, mem, mix_norm_g, w_in, moba_out_g, diff_lambda, diff_subln_g, w_out, rel_bias_table,
           cross_norm_g, mem_norm_g, w_cq, w_ck, w_cv, w_co, ffn_norm_g, w_gate, w_up, w_down,
           final_norm_g):
    batch, seq, d = x.shape
    mem_len = mem.shape[1]
    assert mix_norm_g.shape[0] == 1, "single-layer trunk"
    x2d = x.reshape(batch * seq, d)
    mem2d = mem.reshape(batch * mem_len, d)
    row = lambda v: v.reshape(1, -1).astype(F32)
    wb = lambda w: w[0].astype(BF16)

    bias = _bias_tiles(rel_bias_table.astype(F32))
    proj = _in_projection(x2d, row(mix_norm_g[0]), wb(w_in))
    oa = _moba_attention(proj, bias, row(moba_out_g[0]), batch, seq)
    ob = _diff_attention(proj, bias, diff_lambda[0].astype(F32), row(diff_subln_g[0]), batch, seq)
    kc, vc = _memory_kv(mem2d, row(mem_norm_g[0]), wb(w_ck), wb(w_cv), mem_len)
    x2 = _outproj_cross(x2d, oa, ob, wb(w_out), row(cross_norm_g[0]), wb(w_cq), kc, vc, wb(w_co),
                        seq, mem_len)
    out = _swiglu_final(x2, row(ffn_norm_g[0]), wb(w_gate), wb(w_up), wb(w_down), row(final_norm_g))
    return out.reshape(batch, seq, d)
```

```python
import functools
import math
from typing import Callable, NamedTuple

import numpy as np
import jax
import jax.numpy as jnp
from jax import lax
from jax.experimental import pallas as pl
from jax.experimental.pallas import tpu as pltpu

F32 = jnp.float32
BF16 = jnp.bfloat16

A_HEADS = 8
A_HEAD_DIM = 64
A_WIDTH = A_HEADS * A_HEAD_DIM
MOBA_BLOCK = 256
MOBA_TOPK = 3
B_HEADS = 4
B_QK_DIM = 64
B_V_DIM = 2 * B_QK_DIM
B_WIDTH = B_HEADS * B_V_DIM
MEM_HEADS = 4
REL_BUCKETS = 32
REL_MAX_DIST = 128
EPS = 1e-6
NEG_INF = -1e30
LAMBDA_INIT = 0.8 - 0.6 * math.exp(-0.3 * 0)
QK_SCALE = A_HEAD_DIM ** -0.5
LOG2E = math.log2(math.e)

LANES = 128
SUBLANES = 8
VMEM_LIMIT_BYTES = 56 * 1024 * 1024

BLK = MOBA_BLOCK
TILES_PER_STEP = 4
TM_PROJ = 1024
TM_CROSS = 512
FF_CHUNK = 256
PROJ_CHUNK = 512


def _dot(a, b):
    return jnp.dot(a, b, preferred_element_type=F32)


def _dot_nt(a, b):
    return lax.dot_general(a, b, (((1,), (1,)), ((), ())), preferred_element_type=F32)


def _rms(x, g):
    return x * lax.rsqrt(jnp.mean(x * x, axis=-1, keepdims=True) + EPS) * g


def _params(n_axes):
    return pltpu.CompilerParams(dimension_semantics=("arbitrary",) * n_axes,
                                vmem_limit_bytes=VMEM_LIMIT_BYTES)


def _resident(shape):
    return pl.BlockSpec(shape, lambda *_: (0,) * len(shape), pipeline_mode=pl.Buffered(1))


def _rel_bucket_np(dist):
    n = np.maximum(dist, 0)
    max_exact = REL_BUCKETS // 2
    ratio = np.maximum(n, max_exact).astype(np.float32) / np.float32(max_exact)
    log_ratio = np.log(ratio) / np.float32(math.log(REL_MAX_DIST / max_exact))
    large = max_exact + (log_ratio * np.float32(REL_BUCKETS - max_exact)).astype(np.int32)
    large = np.minimum(large, REL_BUCKETS - 1)
    return np.where(n < max_exact, n, large).astype(np.int32)


HALF = BLK // 2


def _bucket_tiles():
    i = np.arange(HALF)[:, None]
    j = np.arange(HALF)[None, :]
    band = np.where(i - j >= 0, _rel_bucket_np(i - j), -1)
    corner = _rel_bucket_np(HALF + i - j)
    return np.stack([band, corner]).astype(np.int32)


def _bias_kernel(tab_ref, idx_ref, o_ref):
    assert HALF == REL_MAX_DIST
    idx = idx_ref[...]
    zero = jnp.zeros((HALF, HALF), F32)
    lo, hi = slice(0, HALF), slice(HALF, BLK)
    for h in range(o_ref.shape[0]):
        far = tab_ref[REL_BUCKETS - 1, h]
        acc = jnp.zeros(idx.shape, F32)
        for b in range(REL_BUCKETS - 1):
            acc = jnp.where(idx == b, (tab_ref[b, h] - far) * LOG2E, acc)
        band = jnp.where(idx[0] < 0, NEG_INF, acc[0])
        corner = acc[1]
        o_ref[h, 0, lo, lo] = band
        o_ref[h, 0, lo, hi] = jnp.full((HALF, HALF), NEG_INF, F32)
        o_ref[h, 0, hi, lo] = corner
        o_ref[h, 0, hi, hi] = band
        o_ref[h, 1, lo, lo] = zero
        o_ref[h, 1, lo, hi] = corner
        o_ref[h, 1, hi, lo] = zero
        o_ref[h, 1, hi, hi] = zero


def _bias_tiles(table):
    n_heads = table.shape[1]
    idx = jnp.asarray(_bucket_tiles())
    return pl.pallas_call(
        _bias_kernel,
        in_specs=[pl.BlockSpec(memory_space=pltpu.SMEM),
                  pl.BlockSpec(memory_space=pltpu.VMEM)],
        out_specs=pl.BlockSpec(memory_space=pltpu.VMEM),
        out_shape=jax.ShapeDtypeStruct((n_heads, 2, BLK, BLK), F32),
        compiler_params=pltpu.CompilerParams(vmem_limit_bytes=VMEM_LIMIT_BYTES),
        name="rel_bias_tiles",
    )(table, idx)


def _inproj_kernel(x_ref, g_ref, w_ref, o_ref, *, q_chunks):
    hb = _rms(x_ref[...], g_ref[...]).astype(BF16)
    for j in range(w_ref.shape[1] // PROJ_CHUNK):
        cols = slice(j * PROJ_CHUNK, (j + 1) * PROJ_CHUNK)
        acc = _dot(hb, w_ref[:, cols])
        if j in q_chunks:
            acc = acc * (QK_SCALE * LOG2E)
        o_ref[:, cols] = acc.astype(BF16)


def _in_projection(x2d, g, w_bf16):
    t, d = x2d.shape
    n = w_bf16.shape[1]
    assert t % TM_PROJ == 0 and n % PROJ_CHUNK == 0
    q_chunks = (0, (3 * A_WIDTH) // PROJ_CHUNK)
    return pl.pallas_call(
        functools.partial(_inproj_kernel, q_chunks=q_chunks),
        grid=(t // TM_PROJ,),
        in_specs=[pl.BlockSpec((TM_PROJ, d), lambda i: (i, 0)),
                  _resident((1, d)),
                  _resident((d, n))],
        out_specs=pl.BlockSpec((TM_PROJ, n), lambda i: (i, 0)),
        out_shape=jax.ShapeDtypeStruct((t, n), BF16),
        compiler_params=_params(1),
        name="in_projection",
    )(x2d, g, w_bf16)


def _half_mask(shape, half):
    lane = lax.broadcasted_iota(jnp.int32, shape, 1)
    return lax.shift_right_logical(lane, 6) == half


def _block_rows(n):
    return slice(n * BLK, (n + 1) * BLK)


class _Stream(NamedTuple):
    j: int
    slot: int
    q_diag: Callable
    q_past: Callable
    k_tile: Callable
    v_tile: Callable
    bias_tile: Callable
    done: Callable


def _two_pass_attention(streams, s_ref, n_diag_first):
    lead = 2
    row_max = [None] * len(streams)
    results = [None] * len(streams)

    def blocks(i):
        return list(range(streams[i].j, -1, -1))

    def score_step(i, n):
        st = streams[i]
        s = _dot_nt(st.q_diag() if n == st.j else st.q_past(), st.k_tile(n))
        if n >= st.j - 1:
            s = s + st.bias_tile(st.j - n)
        s_ref[st.slot, n] = s
        mx = jnp.maximum(s[:, :LANES], s[:, LANES:])
        mx = mx if row_max[i] is None else jnp.maximum(row_max[i], mx)
        if n == 0:
            mx = jnp.broadcast_to(jnp.max(mx, axis=1, keepdims=True), mx.shape)
        row_max[i] = mx

    def value_step(i, n):
        st = streams[i]
        m = row_max[i]
        p = jnp.concatenate([jnp.exp2(s_ref[st.slot, n, :, :LANES] - m),
                             jnp.exp2(s_ref[st.slot, n, :, LANES:] - m)], axis=1)
        pv = _dot(p.astype(BF16), st.v_tile(n))
        results[i] = pv if results[i] is None else results[i] + pv

    early = max(lead, n_diag_first)
    for i in range(early):
        score_step(i, streams[i].j)
    for i in range(lead):
        for n in blocks(i)[1:]:
            score_step(i, n)
    for i in range(len(streams)):
        a = i + lead
        ahead = [] if a >= len(streams) else blocks(a)[1:] if a < early else blocks(a)
        mine = blocks(i)
        for k in range(max(len(mine), len(ahead))):
            if k < len(mine):
                value_step(i, mine[k])
            if k < len(ahead):
                score_step(a, ahead[k])
        streams[i].done(results[i])


def _moba_streams(j, r, q_ref, bias_ref, g_ref, o_ref, kmt_ref, kaug_ref, vaug_ref, qaug_ref, topk):
    n_pairs = A_WIDTH // LANES
    rows = _block_rows(r)

    def own_block_query(h):
        qp = q_ref[rows, (h // 2) * LANES:(h // 2 + 1) * LANES]
        return jnp.where(_half_mask(qp.shape, h % 2), qp, jnp.zeros_like(qp))

    if j > topk:
        kmt = kmt_ref[...]
        kmt_hi = kmt.astype(BF16)
        kmt_lo = (kmt - kmt_hi.astype(F32)).astype(BF16)
        q_all = q_ref[rows, :]
        gate = _dot_nt(kmt_hi, q_all) + _dot_nt(kmt_lo, q_all)
        slabs = [gate[n * A_HEADS:(n + 1) * A_HEADS, :] for n in range(j)]
        sel_rows = []
        for n in range(j):
            rank = jnp.zeros(slabs[n].shape, F32)
            for m in range(j):
                if m != n:
                    beats = (slabs[m] >= slabs[n]) if m < n else (slabs[m] > slabs[n])
                    rank = rank + jnp.where(beats, 1.0, 0.0)
            sel_rows.append(jnp.where(rank < topk, 0.0, NEG_INF))
        sel_rows.append(jnp.zeros(((8 - j) * A_HEADS, BLK), F32))
        sel_bias = jnp.concatenate(sel_rows + sel_rows, axis=0).T.astype(BF16)
        base = r * A_HEADS
        for p in range(n_pairs):
            qp = q_ref[rows, p * LANES:(p + 1) * LANES]
            for half in range(2):
                qaug_ref[base + 2 * p + half] = jnp.where(_half_mask(qp.shape, half), qp, sel_bias)
        past_query = lambda h: qaug_ref[base + h]
    else:
        past_query = own_block_query

    acc = [None] * A_HEADS
    pairs = []

    def head_done(h, result):
        acc[h] = result
        if h % 2 == 1:
            a0, a1 = acc[h - 1], acc[h]
            first = _half_mask(a0.shape, 0)
            num = jnp.where(first, a0, a1)
            den = pltpu.roll(jnp.where(first, a1, a0), LANES // 2, 1)
            pairs.append(num / den)
        if len(pairs) == n_pairs:
            o_ref[rows, :] = _rms(jnp.concatenate(pairs, axis=1), g_ref[...]).astype(BF16)

    return [_Stream(j=j, slot=h,
                    q_diag=functools.partial(own_block_query, h),
                    q_past=functools.partial(past_query, h),
                    k_tile=lambda n, h=h: kaug_ref[h, _block_rows(n), :],
                    v_tile=lambda n, h=h: vaug_ref[h, _block_rows(n), :],
                    bias_tile=lambda which, h=h: bias_ref[h, which],
                    done=functools.partial(head_done, h))
            for h in range(A_HEADS)]


def _moba_kernel(q_ref, k_ref, v_ref, bias_ref, g_ref, o_ref,
                 kmt_ref, kaug_ref, vaug_ref, qaug_ref, s_ref, *, topk, per_step):
    t = pl.program_id(1)
    nb = k_ref.shape[0] // BLK
    n_pairs = A_WIDTH // LANES

    def tiles(t_static):
        streams = []
        for r in range(per_step):
            streams += _moba_streams(per_step * t_static + r, r, q_ref, bias_ref, g_ref, o_ref,
                                     kmt_ref, kaug_ref, vaug_ref, qaug_ref, topk)
        _two_pass_attention(streams, s_ref, n_diag_first=A_HEADS if per_step * t_static > topk else 0)

    @pl.when(t == 0)
    def _first_tiles_of_batch():
        row = lax.broadcasted_iota(jnp.int32, (A_HEADS, A_WIDTH), 0)
        lane = lax.broadcasted_iota(jnp.int32, (A_HEADS, A_WIDTH), 1)
        head_mask = lax.shift_right_logical(lane, 6) == row
        kmt_ref[...] = jnp.zeros(kmt_ref.shape, F32)
        for n in range(nb):
            kb = k_ref[n * BLK:(n + 1) * BLK, :].astype(F32)
            km = jnp.sum(kb, axis=0, keepdims=True) * (1.0 / BLK)
            kmt_ref[n * A_HEADS:(n + 1) * A_HEADS, :] = jnp.where(head_mask, km, 0.0)
        lane = lax.broadcasted_iota(jnp.int32, (1, LANES), 1)
        for p in range(n_pairs):
            cols = slice(p * LANES, (p + 1) * LANES)
            for half in range(2):
                h = 2 * p + half
                own = lax.shift_right_logical(lane, 6) == half
                own_one = jnp.where(own, 1.0, 0.0).astype(BF16)
                vaug_ref[h] = v_ref[:, cols] * own_one + jnp.where(own, 0.0, 1.0).astype(BF16)
                for n in range(nb):
                    rows = slice(n * BLK, (n + 1) * BLK)
                    code = jnp.where((~own) & ((lane & 63) == n * A_HEADS + h), 1.0, 0.0).astype(BF16)
                    kaug_ref[h, rows, :] = k_ref[rows, cols] * own_one + code
        tiles(0)

    for t_static in range(1, nb // per_step):
        pl.when(t == t_static)(functools.partial(tiles, t_static))


def _moba_attention(proj, bias, g, batch, seq):
    nq = seq // BLK
    per_step = min(TILES_PER_STEP, nq)
    assert seq % (per_step * BLK) == 0 and nq <= 8
    topk = min(MOBA_TOPK, nq)
    steps = nq // per_step
    return pl.pallas_call(
        functools.partial(_moba_kernel, topk=topk, per_step=per_step),
        grid=(batch, steps),
        in_specs=[pl.BlockSpec((per_step * BLK, A_WIDTH), lambda b, t: (b * steps + t, 0)),
                  pl.BlockSpec((seq, A_WIDTH), lambda b, t: (b, 1)),
                  pl.BlockSpec((seq, A_WIDTH), lambda b, t: (b, 2)),
                  pl.BlockSpec((A_HEADS, 2, BLK, BLK), lambda b, t: (0, 0, 0, 0),
                               pipeline_mode=pl.Buffered(1)),
                  _resident((1, A_WIDTH))],
        out_specs=pl.BlockSpec((per_step * BLK, A_WIDTH), lambda b, t: (b * steps + t, 0)),
        out_shape=jax.ShapeDtypeStruct((batch * seq, A_WIDTH), BF16),
        scratch_shapes=[pltpu.VMEM((8 * A_HEADS, A_WIDTH), F32),
                        pltpu.VMEM((A_HEADS, seq, LANES), BF16),
                        pltpu.VMEM((A_HEADS, seq, LANES), BF16),
                        pltpu.VMEM((per_step * A_HEADS, BLK, LANES), BF16),
                        pltpu.VMEM((A_HEADS, nq, BLK, BLK), F32)],
        compiler_params=_params(2),
        name="moba_attention",
    )(proj, proj, proj, bias, g)


def _diff_streams(j, r, q_ref, k_ref, bias_ref, lam, g_ref, o_ref, vaug_ref):
    rows = _block_rows(r)

    def map_query(s):
        qp = q_ref[rows, (s // 2) * LANES:(s // 2 + 1) * LANES]
        return jnp.where(_half_mask(qp.shape, s % 2), qp, jnp.zeros_like(qp))

    acc = [None] * (2 * B_HEADS)

    def map_done(s, result):
        acc[s] = result
        if s % 2 == 1:
            h = s // 2
            a0, a1 = acc[s - 1], acc[s]
            n0, l0 = a0[:, :B_V_DIM], a0[:, B_V_DIM:]
            n1, l1 = a1[:, :B_V_DIM], a1[:, B_V_DIM:]
            o = (n0 * l1 - lam * (n1 * l0)) / (l0 * l1)
            o_ref[rows, h * B_V_DIM:(h + 1) * B_V_DIM] = (
                _rms(o, g_ref[...]) * (1.0 - LAMBDA_INIT)).astype(BF16)

    return [_Stream(j=j, slot=s,
                    q_diag=functools.partial(map_query, s),
                    q_past=functools.partial(map_query, s),
                    k_tile=lambda n, s=s: k_ref[_block_rows(n), (s // 2) * LANES:(s // 2 + 1) * LANES],
                    v_tile=lambda n, s=s: vaug_ref[s // 2, _block_rows(n), :],
                    bias_tile=lambda which, s=s: bias_ref[s // 2, which],
                    done=functools.partial(map_done, s))
            for s in range(2 * B_HEADS)]


def _diff_kernel(q_ref, k_ref, v_ref, bias_ref, lam_ref, g_ref, o_ref, vaug_ref, s_ref, *, per_step):
    t = pl.program_id(1)

    def tiles(t_static):
        lp = lam_ref[...]
        lam = (jnp.exp(jnp.sum(lp[0:1] * lp[1:2], axis=1, keepdims=True))
               - jnp.exp(jnp.sum(lp[2:3] * lp[3:4], axis=1, keepdims=True)) + LAMBDA_INIT)
        streams = []
        for r in range(per_step):
            streams += _diff_streams(per_step * t_static + r, r, q_ref, k_ref, bias_ref, lam, g_ref,
                                     o_ref, vaug_ref)
        _two_pass_attention(streams, s_ref, n_diag_first=0)

    @pl.when(t == 0)
    def _first_tiles_of_batch():
        for h in range(B_HEADS):
            vaug_ref[h, :, :B_V_DIM] = v_ref[:, h * B_V_DIM:(h + 1) * B_V_DIM]
            vaug_ref[h, :, B_V_DIM:] = jnp.ones((v_ref.shape[0], LANES), BF16)
        tiles(0)

    for t_static in range(1, k_ref.shape[0] // (per_step * BLK)):
        pl.when(t == t_static)(functools.partial(tiles, t_static))


def _diff_attention(proj, bias, lam, g, batch, seq):
    nq = seq // BLK
    per_step = min(TILES_PER_STEP, nq)
    assert seq % (per_step * BLK) == 0 and B_V_DIM == LANES
    steps = nq // per_step
    first = (3 * A_WIDTH) // B_WIDTH
    return pl.pallas_call(
        functools.partial(_diff_kernel, per_step=per_step),
        grid=(batch, steps),
        in_specs=[pl.BlockSpec((per_step * BLK, B_WIDTH), lambda b, t: (b * steps + t, first)),
                  pl.BlockSpec((seq, B_WIDTH), lambda b, t: (b, first + 1)),
                  pl.BlockSpec((seq, B_WIDTH), lambda b, t: (b, first + 2)),
                  pl.BlockSpec((B_HEADS, 2, BLK, BLK), lambda b, t: (A_HEADS // B_HEADS, 0, 0, 0),
                               pipeline_mode=pl.Buffered(1)),
                  _resident(lam.shape),
                  _resident((1, B_V_DIM))],
        out_specs=pl.BlockSpec((per_step * BLK, B_WIDTH), lambda b, t: (b * steps + t, 0)),
        out_shape=jax.ShapeDtypeStruct((batch * seq, B_WIDTH), BF16),
        scratch_shapes=[pltpu.VMEM((B_HEADS, seq, 2 * LANES), BF16),
                        pltpu.VMEM((2 * B_HEADS, nq, BLK, BLK), F32)],
        compiler_params=_params(2),
        name="diff_attention",
    )(proj, proj, proj, bias, lam, g)


def _memkv_kernel(mem_ref, g_ref, wk_ref, wv_ref, k_ref, v_ref):
    m = _rms(mem_ref[...], g_ref[...]).astype(BF16)
    k_ref[...] = _dot(m, wk_ref[...]).astype(BF16)
    v_ref[...] = _dot(m, wv_ref[...]).astype(BF16)


def _memory_kv(mem2d, g, wk, wv, mem_len):
    t, d = mem2d.shape
    n = wk.shape[1]
    return pl.pallas_call(
        _memkv_kernel,
        grid=(t // mem_len,),
        in_specs=[pl.BlockSpec((mem_len, d), lambda b: (b, 0)),
                  _resident((1, d)), _resident((d, n)), _resident((d, n))],
        out_specs=[pl.BlockSpec((mem_len, n), lambda b: (b, 0))] * 2,
        out_shape=[jax.ShapeDtypeStruct((t, n), BF16)] * 2,
        compiler_params=_params(1),
        name="memory_kv",
    )(mem2d, g, wk, wv)


def _cross_kernel(x_ref, oa_ref, ob_ref, wo_ref, g_ref, wq_ref, kc_ref, vc_ref, wco_ref, o_ref,
                  *, scale):
    x1 = (x_ref[...] + _dot(oa_ref[...], wo_ref[:A_WIDTH, :]) + _dot(ob_ref[...], wo_ref[A_WIDTH:, :]))
    hb = _rms(x1, g_ref[...]).astype(BF16)
    q = (_dot(hb, wq_ref[...]) * scale).astype(BF16)
    hd = q.shape[1] // MEM_HEADS
    heads = []
    for h in range(MEM_HEADS):
        cols = slice(h * hd, (h + 1) * hd)
        s = _dot_nt(q[:, cols], kc_ref[:, cols])
        p = jnp.exp2(s - jnp.max(s, axis=1, keepdims=True))
        l = jnp.sum(p, axis=1, keepdims=True)
        heads.append((_dot(p.astype(BF16), vc_ref[:, cols]) / l).astype(BF16))
    o = jnp.concatenate(heads, axis=1)
    o_ref[...] = x1 + _dot(o, wco_ref[...])


def _outproj_cross(x2d, oa, ob, wo, g, wq, kc, vc, wco, seq, mem_len):
    t, d = x2d.shape
    per_seq = seq // TM_CROSS
    assert seq % TM_CROSS == 0
    hd = wq.shape[1] // MEM_HEADS
    scale = hd ** -0.5 * LOG2E
    return pl.pallas_call(
        functools.partial(_cross_kernel, scale=scale),
        grid=(t // TM_CROSS,),
        in_specs=[pl.BlockSpec((TM_CROSS, d), lambda i: (i, 0)),
                  pl.BlockSpec((TM_CROSS, A_WIDTH), lambda i: (i, 0)),
                  pl.BlockSpec((TM_CROSS, B_WIDTH), lambda i: (i, 0)),
                  _resident(wo.shape), _resident((1, d)), _resident(wq.shape),
                  pl.BlockSpec((mem_len, kc.shape[1]), lambda i: (i // per_seq, 0)),
                  pl.BlockSpec((mem_len, vc.shape[1]), lambda i: (i // per_seq, 0)),
                  _resident(wco.shape)],
        out_specs=pl.BlockSpec((TM_CROSS, d), lambda i: (i, 0)),
        out_shape=jax.ShapeDtypeStruct((t, d), F32),
        compiler_params=_params(1),
        name="outproj_cross_attention",
    )(x2d, oa, ob, wo, g, wq, kc, vc, wco)


def _ffn_kernel(x_ref, g_ref, wg_ref, wu_ref, wd_ref, gf_ref, o_ref, acc_ref):
    x = x_ref[...]
    hb = _rms(x, g_ref[...]).astype(BF16)
    acc_ref[...] = x
    for c in range(wg_ref.shape[1] // FF_CHUNK):
        cols = slice(c * FF_CHUNK, (c + 1) * FF_CHUNK)
        a = (jax.nn.silu(_dot(hb, wg_ref[:, cols])) * _dot(hb, wu_ref[:, cols])).astype(BF16)
        acc_ref[...] += _dot(a, wd_ref[cols, :])
    o_ref[...] = _rms(acc_ref[...], gf_ref[...])


def _swiglu_final(x2d, g, wg, wu, wd, gf):
    t, d = x2d.shape
    assert t % TM_PROJ == 0 and wg.shape[1] % FF_CHUNK == 0
    return pl.pallas_call(
        _ffn_kernel,
        grid=(t // TM_PROJ,),
        in_specs=[pl.BlockSpec((TM_PROJ, d), lambda i: (i, 0)),
                  _resident((1, d)), _resident(wg.shape), _resident(wu.shape), _resident(wd.shape),
                  _resident((1, d))],
        out_specs=pl.BlockSpec((TM_PROJ, d), lambda i: (i, 0)),
        out_shape=jax.ShapeDtypeStruct((t, d), F32),
        scratch_shapes=[pltpu.VMEM((TM_PROJ, d), F32)],
        compiler_params=_params(1),
        name="swiglu_final_norm",
    )(x2d, g, wg, wu, wd, gf)


def kernel(x, mem, mix_norm_g, w_in, moba_out_g, diff_lambda, diff_subln_g, w_out, rel_bias_table,
           cross_norm_g, mem_norm_g, w_cq, w_ck, w_cv, w_co, ffn_norm_g, w_gate, w_up, w_down,
           final_norm_g):
    batch, seq, d = x.shape
    mem_len = mem.shape[1]
    assert mix_norm_g.shape[0] == 1, "single-layer trunk"
    x2d = x.reshape(batch * seq, d)
    mem2d = mem.reshape(batch * mem_len, d)
    row = lambda v: v.reshape(1, -1).astype(F32)
    wb = lambda w: w[0].astype(BF16)

    bias = _bias_tiles(rel_bias_table.astype(F32))
    proj = _in_projection(x2d, row(mix_norm_g[0]), wb(w_in))
    oa = _moba_attention(proj, bias, row(moba_out_g[0]), batch, seq)
    ob = _diff_attention(proj, bias, diff_lambda[0].astype(F32), row(diff_subln_g[0]), batch, seq)
    kc, vc = _memory_kv(mem2d, row(mem_norm_g[0]), wb(w_ck), wb(w_cv), mem_len)
    x2 = _outproj_cross(x2d, oa, ob, wb(w_out), row(cross_norm_g[0]), wb(w_cq), kc, vc, wb(w_co),
                        seq, mem_len)
    out = _swiglu_final(x2, row(ffn_norm_g[0]), wb(w_gate), wb(w_up), wb(w_down), row(final_norm_g))
    return out.reshape(batch, seq, d)
```

```python
import functools
import math
from typing import Callable, NamedTuple

import numpy as np
import jax
import jax.numpy as jnp
from jax import lax
from jax.experimental import pallas as pl
from jax.experimental.pallas import tpu as pltpu

F32 = jnp.float32
BF16 = jnp.bfloat16

A_HEADS = 8
A_HEAD_DIM = 64
A_WIDTH = A_HEADS * A_HEAD_DIM
MOBA_BLOCK = 256
MOBA_TOPK = 3
B_HEADS = 4
B_QK_DIM = 64
B_V_DIM = 2 * B_QK_DIM
B_WIDTH = B_HEADS * B_V_DIM
MEM_HEADS = 4
REL_BUCKETS = 32
REL_MAX_DIST = 128
EPS = 1e-6
NEG_INF = -1e30
LAMBDA_INIT = 0.8 - 0.6 * math.exp(-0.3 * 0)
QK_SCALE = A_HEAD_DIM ** -0.5
LOG2E = math.log2(math.e)

LANES = 128
SUBLANES = 8
VMEM_LIMIT_BYTES = 56 * 1024 * 1024

BLK = MOBA_BLOCK
TILES_PER_STEP = 4
TM_PROJ = 1024
TM_FFN = 512
TM_CROSS = 512
FF_CHUNK = 256
PROJ_CHUNK = 512


def _dot(a, b):
    return jnp.dot(a, b, preferred_element_type=F32)


def _wdot(a, w):
    return jnp.dot(a, w.astype(BF16), preferred_element_type=F32)


def _dot_nt(a, b):
    return lax.dot_general(a, b, (((1,), (1,)), ((), ())), preferred_element_type=F32)


def _rms(x, g):
    return x * lax.rsqrt(jnp.mean(x * x, axis=-1, keepdims=True) + EPS) * g


def _params(n_axes):
    return pltpu.CompilerParams(dimension_semantics=("arbitrary",) * n_axes,
                                vmem_limit_bytes=VMEM_LIMIT_BYTES)


def _resident(shape):
    return pl.BlockSpec(shape, lambda *_: (0,) * len(shape), pipeline_mode=pl.Buffered(1))


def _rel_bucket_np(dist):
    n = np.maximum(dist, 0)
    max_exact = REL_BUCKETS // 2
    ratio = np.maximum(n, max_exact).astype(np.float32) / np.float32(max_exact)
    log_ratio = np.log(ratio) / np.float32(math.log(REL_MAX_DIST / max_exact))
    large = max_exact + (log_ratio * np.float32(REL_BUCKETS - max_exact)).astype(np.int32)
    large = np.minimum(large, REL_BUCKETS - 1)
    return np.where(n < max_exact, n, large).astype(np.int32)


HALF = BLK // 2


def _bucket_tiles():
    i = np.arange(HALF)[:, None]
    j = np.arange(HALF)[None, :]
    band = np.where(i - j >= 0, _rel_bucket_np(i - j), -1)
    corner = _rel_bucket_np(HALF + i - j)
    return np.stack([band, corner]).astype(np.int32)


def _bias_kernel(tab_ref, idx_ref, o_ref):
    assert HALF == REL_MAX_DIST
    idx = idx_ref[...]
    zero = jnp.zeros((HALF, HALF), F32)
    lo, hi = slice(0, HALF), slice(HALF, BLK)
    for h in range(o_ref.shape[0]):
        far = tab_ref[REL_BUCKETS - 1, h]
        acc = jnp.zeros(idx.shape, F32)
        for b in range(REL_BUCKETS - 1):
            acc = jnp.where(idx == b, (tab_ref[b, h] - far) * LOG2E, acc)
        band = jnp.where(idx[0] < 0, NEG_INF, acc[0])
        corner = acc[1]
        o_ref[h, 0, lo, lo] = band
        o_ref[h, 0, lo, hi] = jnp.full((HALF, HALF), NEG_INF, F32)
        o_ref[h, 0, hi, lo] = corner
        o_ref[h, 0, hi, hi] = band
        o_ref[h, 1, lo, lo] = zero
        o_ref[h, 1, lo, hi] = corner
        o_ref[h, 1, hi, lo] = zero
        o_ref[h, 1, hi, hi] = zero


def _bias_tiles(table):
    n_heads = table.shape[1]
    idx = jnp.asarray(_bucket_tiles())
    return pl.pallas_call(
        _bias_kernel,
        in_specs=[pl.BlockSpec(memory_space=pltpu.SMEM),
                  pl.BlockSpec(memory_space=pltpu.VMEM)],
        out_specs=pl.BlockSpec(memory_space=pltpu.VMEM),
        out_shape=jax.ShapeDtypeStruct((n_heads, 2, BLK, BLK), F32),
        compiler_params=pltpu.CompilerParams(vmem_limit_bytes=VMEM_LIMIT_BYTES),
        name="rel_bias_tiles",
    )(table, idx)


def _inproj_kernel(x_ref, g_ref, w_ref, o_ref, *, q_chunks):
    hb = _rms(x_ref[...], g_ref[...]).astype(BF16)
    for j in range(w_ref.shape[1] // PROJ_CHUNK):
        cols = slice(j * PROJ_CHUNK, (j + 1) * PROJ_CHUNK)
        acc = _wdot(hb, w_ref[:, cols])
        if j in q_chunks:
            acc = acc * (QK_SCALE * LOG2E)
        o_ref[:, cols] = acc.astype(BF16)


def _in_projection(x2d, g, w_bf16):
    t, d = x2d.shape
    n = w_bf16.shape[1]
    assert t % TM_PROJ == 0 and n % PROJ_CHUNK == 0
    q_chunks = (0, (3 * A_WIDTH) // PROJ_CHUNK)
    return pl.pallas_call(
        functools.partial(_inproj_kernel, q_chunks=q_chunks),
        grid=(t // TM_PROJ,),
        in_specs=[pl.BlockSpec((TM_PROJ, d), lambda i: (i, 0)),
                  _resident((1, d)),
                  _resident((d, n))],
        out_specs=pl.BlockSpec((TM_PROJ, n), lambda i: (i, 0)),
        out_shape=jax.ShapeDtypeStruct((t, n), BF16),
        compiler_params=_params(1),
        name="in_projection",
    )(x2d, g, w_bf16)


def _half_mask(shape, half):
    lane = lax.broadcasted_iota(jnp.int32, shape, 1)
    return lax.shift_right_logical(lane, 6) == half


def _block_rows(n):
    return slice(n * BLK, (n + 1) * BLK)


class _Stream(NamedTuple):
    j: int
    slot: int
    q_diag: Callable
    q_past: Callable
    k_tile: Callable
    v_tile: Callable
    bias_tile: Callable
    done: Callable


def _two_pass_attention(streams, s_ref, n_diag_first):
    lead = 2
    row_max = [None] * len(streams)
    results = [None] * len(streams)

    def blocks(i):
        return list(range(streams[i].j, -1, -1))

    def score_step(i, n):
        st = streams[i]
        s = _dot_nt(st.q_diag() if n == st.j else st.q_past(), st.k_tile(n))
        if n >= st.j - 1:
            s = s + st.bias_tile(st.j - n)
        s_ref[st.slot, n] = s
        mx = jnp.maximum(s[:, :LANES], s[:, LANES:])
        mx = mx if row_max[i] is None else jnp.maximum(row_max[i], mx)
        if n == 0:
            mx = jnp.broadcast_to(jnp.max(mx, axis=1, keepdims=True), mx.shape)
        row_max[i] = mx

    def value_step(i, n):
        st = streams[i]
        m = row_max[i]
        p = jnp.concatenate([jnp.exp2(s_ref[st.slot, n, :, :LANES] - m),
                             jnp.exp2(s_ref[st.slot, n, :, LANES:] - m)], axis=1)
        pv = _dot(p.astype(BF16), st.v_tile(n))
        results[i] = pv if results[i] is None else results[i] + pv

    early = max(lead, n_diag_first)
    for i in range(early):
        score_step(i, streams[i].j)
    for i in range(lead):
        for n in blocks(i)[1:]:
            score_step(i, n)
    for i in range(len(streams)):
        a = i + lead
        ahead = [] if a >= len(streams) else blocks(a)[1:] if a < early else blocks(a)
        mine = blocks(i)
        for k in range(max(len(mine), len(ahead))):
            if k < len(mine):
                value_step(i, mine[k])
            if k < len(ahead):
                score_step(a, ahead[k])
        streams[i].done(results[i])


def _moba_streams(j, r, q_ref, bias_ref, g_ref, o_ref, kmt_ref, kaug_ref, vaug_ref, qaug_ref, topk):
    n_pairs = A_WIDTH // LANES
    rows = _block_rows(r)

    def own_block_query(h):
        qp = q_ref[rows, (h // 2) * LANES:(h // 2 + 1) * LANES]
        return jnp.where(_half_mask(qp.shape, h % 2), qp, jnp.zeros_like(qp))

    if j > topk:
        kmt = kmt_ref[...]
        kmt_hi = kmt.astype(BF16)
        kmt_lo = (kmt - kmt_hi.astype(F32)).astype(BF16)
        q_all = q_ref[rows, :]
        gate = _dot_nt(kmt_hi, q_all) + _dot_nt(kmt_lo, q_all)
        slabs = [gate[n * A_HEADS:(n + 1) * A_HEADS, :] for n in range(j)]
        sel_rows = []
        for n in range(j):
            rank = jnp.zeros(slabs[n].shape, F32)
            for m in range(j):
                if m != n:
                    beats = (slabs[m] >= slabs[n]) if m < n else (slabs[m] > slabs[n])
                    rank = rank + jnp.where(beats, 1.0, 0.0)
            sel_rows.append(jnp.where(rank < topk, 0.0, NEG_INF))
        sel_rows.append(jnp.zeros(((8 - j) * A_HEADS, BLK), F32))
        sel_bias = jnp.concatenate(sel_rows + sel_rows, axis=0).T.astype(BF16)
        base = r * A_HEADS
        for p in range(n_pairs):
            qp = q_ref[rows, p * LANES:(p + 1) * LANES]
            for half in range(2):
                qaug_ref[base + 2 * p + half] = jnp.where(_half_mask(qp.shape, half), qp, sel_bias)
        past_query = lambda h: qaug_ref[base + h]
    else:
        past_query = own_block_query

    acc = [None] * A_HEADS
    pairs = []

    def head_done(h, result):
        acc[h] = result
        if h % 2 == 1:
            a0, a1 = acc[h - 1], acc[h]
            first = _half_mask(a0.shape, 0)
            num = jnp.where(first, a0, a1)
            den = pltpu.roll(jnp.where(first, a1, a0), LANES // 2, 1)
            pairs.append(num / den)
        if len(pairs) == n_pairs:
            o_ref[rows, :] = _rms(jnp.concatenate(pairs, axis=1), g_ref[...]).astype(BF16)

    return [_Stream(j=j, slot=h,
                    q_diag=functools.partial(own_block_query, h),
                    q_past=functools.partial(past_query, h),
                    k_tile=lambda n, h=h: kaug_ref[h, _block_rows(n), :],
                    v_tile=lambda n, h=h: vaug_ref[h, _block_rows(n), :],
                    bias_tile=lambda which, h=h: bias_ref[h, which],
                    done=functools.partial(head_done, h))
            for h in range(A_HEADS)]


def _moba_kernel(q_ref, k_ref, v_ref, bias_ref, g_ref, o_ref,
                 kmt_ref, kaug_ref, vaug_ref, qaug_ref, s_ref, *, topk, per_step):
    t = pl.program_id(1)
    nb = k_ref.shape[0] // BLK
    n_pairs = A_WIDTH // LANES

    def tiles(t_static):
        streams = []
        for r in range(per_step):
            streams += _moba_streams(per_step * t_static + r, r, q_ref, bias_ref, g_ref, o_ref,
                                     kmt_ref, kaug_ref, vaug_ref, qaug_ref, topk)
        _two_pass_attention(streams, s_ref, n_diag_first=A_HEADS if per_step * t_static > topk else 0)

    @pl.when(t == 0)
    def _first_tiles_of_batch():
        row = lax.broadcasted_iota(jnp.int32, (A_HEADS, A_WIDTH), 0)
        lane = lax.broadcasted_iota(jnp.int32, (A_HEADS, A_WIDTH), 1)
        head_mask = lax.shift_right_logical(lane, 6) == row
        kmt_ref[...] = jnp.zeros(kmt_ref.shape, F32)
        for n in range(nb):
            kb = k_ref[n * BLK:(n + 1) * BLK, :].astype(F32)
            km = jnp.sum(kb, axis=0, keepdims=True) * (1.0 / BLK)
            kmt_ref[n * A_HEADS:(n + 1) * A_HEADS, :] = jnp.where(head_mask, km, 0.0)
        lane = lax.broadcasted_iota(jnp.int32, (1, LANES), 1)
        for p in range(n_pairs):
            cols = slice(p * LANES, (p + 1) * LANES)
            for half in range(2):
                h = 2 * p + half
                own = lax.shift_right_logical(lane, 6) == half
                own_one = jnp.where(own, 1.0, 0.0).astype(BF16)
                vaug_ref[h] = v_ref[:, cols] * own_one + jnp.where(own, 0.0, 1.0).astype(BF16)
                for n in range(nb):
                    rows = slice(n * BLK, (n + 1) * BLK)
                    code = jnp.where((~own) & ((lane & 63) == n * A_HEADS + h), 1.0, 0.0).astype(BF16)
                    kaug_ref[h, rows, :] = k_ref[rows, cols] * own_one + code
        tiles(0)

    for t_static in range(1, nb // per_step):
        pl.when(t == t_static)(functools.partial(tiles, t_static))


def _moba_attention(proj, bias, g, batch, seq):
    nq = seq // BLK
    per_step = min(TILES_PER_STEP, nq)
    assert seq % (per_step * BLK) == 0 and nq <= 8
    topk = min(MOBA_TOPK, nq)
    steps = nq // per_step
    return pl.pallas_call(
        functools.partial(_moba_kernel, topk=topk, per_step=per_step),
        grid=(batch, steps),
        in_specs=[pl.BlockSpec((per_step * BLK, A_WIDTH), lambda b, t: (b * steps + t, 0)),
                  pl.BlockSpec((seq, A_WIDTH), lambda b, t: (b, 1)),
                  pl.BlockSpec((seq, A_WIDTH), lambda b, t: (b, 2)),
                  pl.BlockSpec((A_HEADS, 2, BLK, BLK), lambda b, t: (0, 0, 0, 0),
                               pipeline_mode=pl.Buffered(1)),
                  _resident((1, A_WIDTH))],
        out_specs=pl.BlockSpec((per_step * BLK, A_WIDTH), lambda b, t: (b * steps + t, 0)),
        out_shape=jax.ShapeDtypeStruct((batch * seq, A_WIDTH), BF16),
        scratch_shapes=[pltpu.VMEM((8 * A_HEADS, A_WIDTH), F32),
                        pltpu.VMEM((A_HEADS, seq, LANES), BF16),
                        pltpu.VMEM((A_HEADS, seq, LANES), BF16),
                        pltpu.VMEM((per_step * A_HEADS, BLK, LANES), BF16),
                        pltpu.VMEM((A_HEADS, nq, BLK, BLK), F32)],
        compiler_params=_params(2),
        name="moba_attention",
    )(proj, proj, proj, bias, g)


def _diff_streams(j, r, q_ref, k_ref, bias_ref, lam, g_ref, o_ref, vaug_ref):
    rows = _block_rows(r)

    def map_query(s):
        qp = q_ref[rows, (s // 2) * LANES:(s // 2 + 1) * LANES]
        return jnp.where(_half_mask(qp.shape, s % 2), qp, jnp.zeros_like(qp))

    acc = [None] * (2 * B_HEADS)

    def map_done(s, result):
        acc[s] = result
        if s % 2 == 1:
            h = s // 2
            a0, a1 = acc[s - 1], acc[s]
            n0, l0 = a0[:, :B_V_DIM], a0[:, B_V_DIM:]
            n1, l1 = a1[:, :B_V_DIM], a1[:, B_V_DIM:]
            o = (n0 * l1 - lam * (n1 * l0)) / (l0 * l1)
            o_ref[rows, h * B_V_DIM:(h + 1) * B_V_DIM] = (
                _rms(o, g_ref[...]) * (1.0 - LAMBDA_INIT)).astype(BF16)

    return [_Stream(j=j, slot=s,
                    q_diag=functools.partial(map_query, s),
                    q_past=functools.partial(map_query, s),
                    k_tile=lambda n, s=s: k_ref[_block_rows(n), (s // 2) * LANES:(s // 2 + 1) * LANES],
                    v_tile=lambda n, s=s: vaug_ref[s // 2, _block_rows(n), :],
                    bias_tile=lambda which, s=s: bias_ref[s // 2, which],
                    done=functools.partial(map_done, s))
            for s in range(2 * B_HEADS)]


def _diff_kernel(q_ref, k_ref, v_ref, bias_ref, lam_ref, g_ref, o_ref, vaug_ref, s_ref, *, per_step):
    t = pl.program_id(1)

    def tiles(t_static):
        lp = lam_ref[...]
        lam = (jnp.exp(jnp.sum(lp[0:1] * lp[1:2], axis=1, keepdims=True))
               - jnp.exp(jnp.sum(lp[2:3] * lp[3:4], axis=1, keepdims=True)) + LAMBDA_INIT)
        streams = []
        for r in range(per_step):
            streams += _diff_streams(per_step * t_static + r, r, q_ref, k_ref, bias_ref, lam, g_ref,
                                     o_ref, vaug_ref)
        _two_pass_attention(streams, s_ref, n_diag_first=0)

    @pl.when(t == 0)
    def _first_tiles_of_batch():
        for h in range(B_HEADS):
            vaug_ref[h, :, :B_V_DIM] = v_ref[:, h * B_V_DIM:(h + 1) * B_V_DIM]
            vaug_ref[h, :, B_V_DIM:] = jnp.ones((v_ref.shape[0], LANES), BF16)
        tiles(0)

    for t_static in range(1, k_ref.shape[0] // (per_step * BLK)):
        pl.when(t == t_static)(functools.partial(tiles, t_static))


def _diff_attention(proj, bias, lam, g, batch, seq):
    nq = seq // BLK
    per_step = min(TILES_PER_STEP, nq)
    assert seq % (per_step * BLK) == 0 and B_V_DIM == LANES
    steps = nq // per_step
    first = (3 * A_WIDTH) // B_WIDTH
    return pl.pallas_call(
        functools.partial(_diff_kernel, per_step=per_step),
        grid=(batch, steps),
        in_specs=[pl.BlockSpec((per_step * BLK, B_WIDTH), lambda b, t: (b * steps + t, first)),
                  pl.BlockSpec((seq, B_WIDTH), lambda b, t: (b, first + 1)),
                  pl.BlockSpec((seq, B_WIDTH), lambda b, t: (b, first + 2)),
                  pl.BlockSpec((B_HEADS, 2, BLK, BLK), lambda b, t: (A_HEADS // B_HEADS, 0, 0, 0),
                               pipeline_mode=pl.Buffered(1)),
                  _resident(lam.shape),
                  _resident((1, B_V_DIM))],
        out_specs=pl.BlockSpec((per_step * BLK, B_WIDTH), lambda b, t: (b * steps + t, 0)),
        out_shape=jax.ShapeDtypeStruct((batch * seq, B_WIDTH), BF16),
        scratch_shapes=[pltpu.VMEM((B_HEADS, seq, 2 * LANES), BF16),
                        pltpu.VMEM((2 * B_HEADS, nq, BLK, BLK), F32)],
        compiler_params=_params(2),
        name="diff_attention",
    )(proj, proj, proj, bias, lam, g)


def _memkv_kernel(mem_ref, g_ref, wk_ref, wv_ref, k_ref, v_ref):
    m = _rms(mem_ref[...], g_ref[...]).astype(BF16)
    k_ref[...] = _wdot(m, wk_ref[...]).astype(BF16)
    v_ref[...] = _wdot(m, wv_ref[...]).astype(BF16)


def _memory_kv(mem2d, g, wk, wv, mem_len):
    t, d = mem2d.shape
    n = wk.shape[1]
    return pl.pallas_call(
        _memkv_kernel,
        grid=(t // mem_len,),
        in_specs=[pl.BlockSpec((mem_len, d), lambda b: (b, 0)),
                  _resident((1, d)), _resident((d, n)), _resident((d, n))],
        out_specs=[pl.BlockSpec((mem_len, n), lambda b: (b, 0))] * 2,
        out_shape=[jax.ShapeDtypeStruct((t, n), BF16)] * 2,
        compiler_params=_params(1),
        name="memory_kv",
    )(mem2d, g, wk, wv)


def _cross_kernel(x_ref, oa_ref, ob_ref, wo_ref, g_ref, wq_ref, kc_ref, vc_ref, wco_ref, o_ref,
                  *, scale):
    x1 = (x_ref[...] + _wdot(oa_ref[...], wo_ref[:A_WIDTH, :]) + _wdot(ob_ref[...], wo_ref[A_WIDTH:, :]))
    hb = _rms(x1, g_ref[...]).astype(BF16)
    q = (_wdot(hb, wq_ref[...]) * scale).astype(BF16)
    hd = q.shape[1] // MEM_HEADS
    heads = []
    for h in range(MEM_HEADS):
        cols = slice(h * hd, (h + 1) * hd)
        s = _dot_nt(q[:, cols], kc_ref[:, cols])
        p = jnp.exp2(s - jnp.max(s, axis=1, keepdims=True))
        l = jnp.sum(p, axis=1, keepdims=True)
        heads.append((_dot(p.astype(BF16), vc_ref[:, cols]) / l).astype(BF16))
    o = jnp.concatenate(heads, axis=1)
    o_ref[...] = x1 + _wdot(o, wco_ref[...])


def _outproj_cross(x2d, oa, ob, wo, g, wq, kc, vc, wco, seq, mem_len):
    t, d = x2d.shape
    per_seq = seq // TM_CROSS
    assert seq % TM_CROSS == 0
    hd = wq.shape[1] // MEM_HEADS
    scale = hd ** -0.5 * LOG2E
    return pl.pallas_call(
        functools.partial(_cross_kernel, scale=scale),
        grid=(t // TM_CROSS,),
        in_specs=[pl.BlockSpec((TM_CROSS, d), lambda i: (i, 0)),
                  pl.BlockSpec((TM_CROSS, A_WIDTH), lambda i: (i, 0)),
                  pl.BlockSpec((TM_CROSS, B_WIDTH), lambda i: (i, 0)),
                  _resident(wo.shape), _resident((1, d)), _resident(wq.shape),
                  pl.BlockSpec((mem_len, kc.shape[1]), lambda i: (i // per_seq, 0)),
                  pl.BlockSpec((mem_len, vc.shape[1]), lambda i: (i // per_seq, 0)),
                  _resident(wco.shape)],
        out_specs=pl.BlockSpec((TM_CROSS, d), lambda i: (i, 0)),
        out_shape=jax.ShapeDtypeStruct((t, d), F32),
        compiler_params=_params(1),
        name="outproj_cross_attention",
    )(x2d, oa, ob, wo, g, wq, kc, vc, wco)


def _ffn_kernel(x_ref, g_ref, wg_ref, wu_ref, wd_ref, gf_ref, o_ref, acc_ref):
    x = x_ref[...]
    hb = _rms(x, g_ref[...]).astype(BF16)
    acc_ref[...] = x
    for c in range(wg_ref.shape[1] // FF_CHUNK):
        cols = slice(c * FF_CHUNK, (c + 1) * FF_CHUNK)
        a = (jax.nn.silu(_wdot(hb, wg_ref[:, cols])) * _wdot(hb, wu_ref[:, cols])).astype(BF16)
        acc_ref[...] += _wdot(a, wd_ref[cols, :])
    o_ref[...] = _rms(acc_ref[...], gf_ref[...])


def _swiglu_final(x2d, g, wg, wu, wd, gf):
    t, d = x2d.shape
    assert t % TM_FFN == 0 and wg.shape[1] % FF_CHUNK == 0
    return pl.pallas_call(
        _ffn_kernel,
        grid=(t // TM_FFN,),
        in_specs=[pl.BlockSpec((TM_FFN, d), lambda i: (i, 0)),
                  _resident((1, d)), _resident(wg.shape), _resident(wu.shape), _resident(wd.shape),
                  _resident((1, d))],
        out_specs=pl.BlockSpec((TM_FFN, d), lambda i: (i, 0)),
        out_shape=jax.ShapeDtypeStruct((t, d), F32),
        scratch_shapes=[pltpu.VMEM((TM_FFN, d), F32)],
        compiler_params=_params(1),
        name="swiglu_final_norm",
    )(x2d, g, wg, wu, wd, gf)


def kernel(x, mem, mix_norm_g, w_in, moba_out_g, diff_lambda, diff_subln_g, w_out, rel_bias_table,
           cross_norm_g, mem_norm_g, w_cq, w_ck, w_cv, w_co, ffn_norm_g, w_gate, w_up, w_down,
           final_norm_g):
    batch, seq, d = x.shape
    mem_len = mem.shape[1]
    assert mix_norm_g.shape[0] == 1, "single-layer trunk"
    x2d = x.reshape(batch * seq, d)
    mem2d = mem.reshape(batch * mem_len, d)
    row = lambda v: v.reshape(1, -1).astype(F32)
    wb = lambda w: w[0].astype(F32)

    bias = _bias_tiles(rel_bias_table.astype(F32))
    proj = _in_projection(x2d, row(mix_norm_g[0]), wb(w_in))
    oa = _moba_attention(proj, bias, row(moba_out_g[0]), batch, seq)
    ob = _diff_attention(proj, bias, diff_lambda[0].astype(F32), row(diff_subln_g[0]), batch, seq)
    kc, vc = _memory_kv(mem2d, row(mem_norm_g[0]), wb(w_ck), wb(w_cv), mem_len)
    x2 = _outproj_cross(x2d, oa, ob, wb(w_out), row(cross_norm_g[0]), wb(w_cq), kc, vc, wb(w_co),
                        seq, mem_len)
    out = _swiglu_final(x2, row(ffn_norm_g[0]), wb(w_gate), wb(w_up), wb(w_down), row(final_norm_g))
    return out.reshape(batch, seq, d)
```

```python
import functools
import math
from typing import Callable, NamedTuple

import numpy as np
import jax
import jax.numpy as jnp
from jax import lax
from jax.experimental import pallas as pl
from jax.experimental.pallas import tpu as pltpu

F32 = jnp.float32
BF16 = jnp.bfloat16

A_HEADS = 8
A_HEAD_DIM = 64
A_WIDTH = A_HEADS * A_HEAD_DIM
MOBA_BLOCK = 256
MOBA_TOPK = 3
B_HEADS = 4
B_QK_DIM = 64
B_V_DIM = 2 * B_QK_DIM
B_WIDTH = B_HEADS * B_V_DIM
MEM_HEADS = 4
REL_BUCKETS = 32
REL_MAX_DIST = 128
EPS = 1e-6
NEG_INF = -1e30
LAMBDA_INIT = 0.8 - 0.6 * math.exp(-0.3 * 0)
QK_SCALE = A_HEAD_DIM ** -0.5
LOG2E = math.log2(math.e)

LANES = 128
SUBLANES = 8
VMEM_LIMIT_BYTES = 56 * 1024 * 1024

BLK = MOBA_BLOCK
TILES_PER_STEP = 4
TM_PROJ = 1024
TM_FFN = 1024
TM_CROSS = 512
FF_CHUNK = 256
PROJ_CHUNK = 512


def _dot(a, b):
    return jnp.dot(a, b, preferred_element_type=F32)


def _wdot(a, w):
    return jnp.dot(a, w.astype(BF16), preferred_element_type=F32)


def _dot_nt(a, b):
    return lax.dot_general(a, b, (((1,), (1,)), ((), ())), preferred_element_type=F32)


def _rms(x, g):
    return x * lax.rsqrt(jnp.mean(x * x, axis=-1, keepdims=True) + EPS) * g


def _params(n_axes):
    return pltpu.CompilerParams(dimension_semantics=("arbitrary",) * n_axes,
                                vmem_limit_bytes=VMEM_LIMIT_BYTES)


def _resident(shape):
    return pl.BlockSpec(shape, lambda *_: (0,) * len(shape), pipeline_mode=pl.Buffered(1))


def _rel_bucket_np(dist):
    n = np.maximum(dist, 0)
    max_exact = REL_BUCKETS // 2
    ratio = np.maximum(n, max_exact).astype(np.float32) / np.float32(max_exact)
    log_ratio = np.log(ratio) / np.float32(math.log(REL_MAX_DIST / max_exact))
    large = max_exact + (log_ratio * np.float32(REL_BUCKETS - max_exact)).astype(np.int32)
    large = np.minimum(large, REL_BUCKETS - 1)
    return np.where(n < max_exact, n, large).astype(np.int32)


HALF = BLK // 2


def _bucket_tiles():
    i = np.arange(HALF)[:, None]
    j = np.arange(HALF)[None, :]
    band = np.where(i - j >= 0, _rel_bucket_np(i - j), -1)
    corner = _rel_bucket_np(HALF + i - j)
    return np.stack([band, corner]).astype(np.int32)


def _bias_kernel(tab_ref, idx_ref, o_ref):
    assert HALF == REL_MAX_DIST
    idx = idx_ref[...]
    zero = jnp.zeros((HALF, HALF), F32)
    lo, hi = slice(0, HALF), slice(HALF, BLK)
    for h in range(o_ref.shape[0]):
        far = tab_ref[REL_BUCKETS - 1, h]
        acc = jnp.zeros(idx.shape, F32)
        for b in range(REL_BUCKETS - 1):
            acc = jnp.where(idx == b, (tab_ref[b, h] - far) * LOG2E, acc)
        band = jnp.where(idx[0] < 0, NEG_INF, acc[0])
        corner = acc[1]
        o_ref[h, 0, lo, lo] = band
        o_ref[h, 0, lo, hi] = jnp.full((HALF, HALF), NEG_INF, F32)
        o_ref[h, 0, hi, lo] = corner
        o_ref[h, 0, hi, hi] = band
        o_ref[h, 1, lo, lo] = zero
        o_ref[h, 1, lo, hi] = corner
        o_ref[h, 1, hi, lo] = zero
        o_ref[h, 1, hi, hi] = zero


def _bias_tiles(table):
    n_heads = table.shape[1]
    idx = jnp.asarray(_bucket_tiles())
    return pl.pallas_call(
        _bias_kernel,
        in_specs=[pl.BlockSpec(memory_space=pltpu.SMEM),
                  pl.BlockSpec(memory_space=pltpu.VMEM)],
        out_specs=pl.BlockSpec(memory_space=pltpu.VMEM),
        out_shape=jax.ShapeDtypeStruct((n_heads, 2, BLK, BLK), F32),
        compiler_params=pltpu.CompilerParams(vmem_limit_bytes=VMEM_LIMIT_BYTES),
        name="rel_bias_tiles",
    )(table, idx)


def _inproj_kernel(x_ref, g_ref, w_ref, o_ref, *, q_chunks):
    hb = _rms(x_ref[...], g_ref[...]).astype(BF16)
    for j in range(w_ref.shape[1] // PROJ_CHUNK):
        cols = slice(j * PROJ_CHUNK, (j + 1) * PROJ_CHUNK)
        acc = _wdot(hb, w_ref[:, cols])
        if j in q_chunks:
            acc = acc * (QK_SCALE * LOG2E)
        o_ref[:, cols] = acc.astype(BF16)


def _in_projection(x2d, g, w_bf16):
    t, d = x2d.shape
    n = w_bf16.shape[1]
    assert t % TM_PROJ == 0 and n % PROJ_CHUNK == 0
    q_chunks = (0, (3 * A_WIDTH) // PROJ_CHUNK)
    return pl.pallas_call(
        functools.partial(_inproj_kernel, q_chunks=q_chunks),
        grid=(t // TM_PROJ,),
        in_specs=[pl.BlockSpec((TM_PROJ, d), lambda i: (i, 0)),
                  _resident((1, d)),
                  _resident((d, n))],
        out_specs=pl.BlockSpec((TM_PROJ, n), lambda i: (i, 0)),
        out_shape=jax.ShapeDtypeStruct((t, n), BF16),
        compiler_params=_params(1),
        name="in_projection",
    )(x2d, g, w_bf16)


def _half_mask(shape, half):
    lane = lax.broadcasted_iota(jnp.int32, shape, 1)
    return lax.shift_right_logical(lane, 6) == half


def _block_rows(n):
    return slice(n * BLK, (n + 1) * BLK)


class _Stream(NamedTuple):
    j: int
    slot: int
    q_diag: Callable
    q_past: Callable
    k_tile: Callable
    v_tile: Callable
    bias_tile: Callable
    done: Callable


def _two_pass_attention(streams, s_ref, n_diag_first):
    lead = 2
    row_max = [None] * len(streams)
    results = [None] * len(streams)

    def blocks(i):
        return list(range(streams[i].j, -1, -1))

    def score_step(i, n):
        st = streams[i]
        s = _dot_nt(st.q_diag() if n == st.j else st.q_past(), st.k_tile(n))
        if n >= st.j - 1:
            s = s + st.bias_tile(st.j - n)
        s_ref[st.slot, n] = s
        mx = jnp.maximum(s[:, :LANES], s[:, LANES:])
        mx = mx if row_max[i] is None else jnp.maximum(row_max[i], mx)
        if n == 0:
            mx = jnp.broadcast_to(jnp.max(mx, axis=1, keepdims=True), mx.shape)
        row_max[i] = mx

    def value_step(i, n):
        st = streams[i]
        m = row_max[i]
        p = jnp.concatenate([jnp.exp2(s_ref[st.slot, n, :, :LANES] - m),
                             jnp.exp2(s_ref[st.slot, n, :, LANES:] - m)], axis=1)
        pv = _dot(p.astype(BF16), st.v_tile(n))
        results[i] = pv if results[i] is None else results[i] + pv

    early = max(lead, n_diag_first)
    for i in range(early):
        score_step(i, streams[i].j)
    for i in range(lead):
        for n in blocks(i)[1:]:
            score_step(i, n)
    for i in range(len(streams)):
        a = i + lead
        ahead = [] if a >= len(streams) else blocks(a)[1:] if a < early else blocks(a)
        mine = blocks(i)
        for k in range(max(len(mine), len(ahead))):
            if k < len(mine):
                value_step(i, mine[k])
            if k < len(ahead):
                score_step(a, ahead[k])
        streams[i].done(results[i])


def _moba_streams(j, r, q_ref, bias_ref, g_ref, o_ref, kmt_ref, kaug_ref, vaug_ref, qaug_ref, topk):
    n_pairs = A_WIDTH // LANES
    rows = _block_rows(r)

    def own_block_query(h):
        qp = q_ref[rows, (h // 2) * LANES:(h // 2 + 1) * LANES]
        return jnp.where(_half_mask(qp.shape, h % 2), qp, jnp.zeros_like(qp))

    if j > topk:
        kmt = kmt_ref[...]
        kmt_hi = kmt.astype(BF16)
        kmt_lo = (kmt - kmt_hi.astype(F32)).astype(BF16)
        q_all = q_ref[rows, :]
        gate = _dot_nt(kmt_hi, q_all) + _dot_nt(kmt_lo, q_all)
        slabs = [gate[n * A_HEADS:(n + 1) * A_HEADS, :] for n in range(j)]
        sel_rows = []
        for n in range(j):
            rank = jnp.zeros(slabs[n].shape, F32)
            for m in range(j):
                if m != n:
                    beats = (slabs[m] >= slabs[n]) if m < n else (slabs[m] > slabs[n])
                    rank = rank + jnp.where(beats, 1.0, 0.0)
            sel_rows.append(jnp.where(rank < topk, 0.0, NEG_INF))
        sel_rows.append(jnp.zeros(((8 - j) * A_HEADS, BLK), F32))
        sel_bias = jnp.concatenate(sel_rows + sel_rows, axis=0).T.astype(BF16)
        base = r * A_HEADS
        for p in range(n_pairs):
            qp = q_ref[rows, p * LANES:(p + 1) * LANES]
            for half in range(2):
                qaug_ref[base + 2 * p + half] = jnp.where(_half_mask(qp.shape, half), qp, sel_bias)
        past_query = lambda h: qaug_ref[base + h]
    else:
        past_query = own_block_query

    acc = [None] * A_HEADS
    pairs = []

    def head_done(h, result):
        acc[h] = result
        if h % 2 == 1:
            a0, a1 = acc[h - 1], acc[h]
            first = _half_mask(a0.shape, 0)
            num = jnp.where(first, a0, a1)
            den = pltpu.roll(jnp.where(first, a1, a0), LANES // 2, 1)
            pairs.append(num / den)
        if len(pairs) == n_pairs:
            o_ref[rows, :] = _rms(jnp.concatenate(pairs, axis=1), g_ref[...]).astype(BF16)

    return [_Stream(j=j, slot=h,
                    q_diag=functools.partial(own_block_query, h),
                    q_past=functools.partial(past_query, h),
                    k_tile=lambda n, h=h: kaug_ref[h, _block_rows(n), :],
                    v_tile=lambda n, h=h: vaug_ref[h, _block_rows(n), :],
                    bias_tile=lambda which, h=h: bias_ref[h, which],
                    done=functools.partial(head_done, h))
            for h in range(A_HEADS)]


def _moba_kernel(q_ref, k_ref, v_ref, bias_ref, g_ref, o_ref,
                 kmt_ref, kaug_ref, vaug_ref, qaug_ref, s_ref, *, topk, per_step):
    t = pl.program_id(1)
    nb = k_ref.shape[0] // BLK
    n_pairs = A_WIDTH // LANES

    def tiles(t_static):
        streams = []
        for r in range(per_step):
            streams += _moba_streams(per_step * t_static + r, r, q_ref, bias_ref, g_ref, o_ref,
                                     kmt_ref, kaug_ref, vaug_ref, qaug_ref, topk)
        _two_pass_attention(streams, s_ref, n_diag_first=A_HEADS if per_step * t_static > topk else 0)

    @pl.when(t == 0)
    def _first_tiles_of_batch():
        row = lax.broadcasted_iota(jnp.int32, (A_HEADS, A_WIDTH), 0)
        lane = lax.broadcasted_iota(jnp.int32, (A_HEADS, A_WIDTH), 1)
        head_mask = lax.shift_right_logical(lane, 6) == row
        kmt_ref[...] = jnp.zeros(kmt_ref.shape, F32)
        for n in range(nb):
            kb = k_ref[n * BLK:(n + 1) * BLK, :].astype(F32)
            km = jnp.sum(kb, axis=0, keepdims=True) * (1.0 / BLK)
            kmt_ref[n * A_HEADS:(n + 1) * A_HEADS, :] = jnp.where(head_mask, km, 0.0)
        lane = lax.broadcasted_iota(jnp.int32, (1, LANES), 1)
        for p in range(n_pairs):
            cols = slice(p * LANES, (p + 1) * LANES)
            for half in range(2):
                h = 2 * p + half
                own = lax.shift_right_logical(lane, 6) == half
                own_one = jnp.where(own, 1.0, 0.0).astype(BF16)
                vaug_ref[h] = v_ref[:, cols] * own_one + jnp.where(own, 0.0, 1.0).astype(BF16)
                for n in range(nb):
                    rows = slice(n * BLK, (n + 1) * BLK)
                    code = jnp.where((~own) & ((lane & 63) == n * A_HEADS + h), 1.0, 0.0).astype(BF16)
                    kaug_ref[h, rows, :] = k_ref[rows, cols] * own_one + code
        tiles(0)

    for t_static in range(1, nb // per_step):
        pl.when(t == t_static)(functools.partial(tiles, t_static))


def _moba_attention(proj, bias, g, batch, seq):
    nq = seq // BLK
    per_step = min(TILES_PER_STEP, nq)
    assert seq % (per_step * BLK) == 0 and nq <= 8
    topk = min(MOBA_TOPK, nq)
    steps = nq // per_step
    return pl.pallas_call(
        functools.partial(_moba_kernel, topk=topk, per_step=per_step),
        grid=(batch, steps),
        in_specs=[pl.BlockSpec((per_step * BLK, A_WIDTH), lambda b, t: (b * steps + t, 0)),
                  pl.BlockSpec((seq, A_WIDTH), lambda b, t: (b, 1)),
                  pl.BlockSpec((seq, A_WIDTH), lambda b, t: (b, 2)),
                  pl.BlockSpec((A_HEADS, 2, BLK, BLK), lambda b, t: (0, 0, 0, 0),
                               pipeline_mode=pl.Buffered(1)),
                  _resident((1, A_WIDTH))],
        out_specs=pl.BlockSpec((per_step * BLK, A_WIDTH), lambda b, t: (b * steps + t, 0)),
        out_shape=jax.ShapeDtypeStruct((batch * seq, A_WIDTH), BF16),
        scratch_shapes=[pltpu.VMEM((8 * A_HEADS, A_WIDTH), F32),
                        pltpu.VMEM((A_HEADS, seq, LANES), BF16),
                        pltpu.VMEM((A_HEADS, seq, LANES), BF16),
                        pltpu.VMEM((per_step * A_HEADS, BLK, LANES), BF16),
                        pltpu.VMEM((A_HEADS, nq, BLK, BLK), F32)],
        compiler_params=_params(2),
        name="moba_attention",
    )(proj, proj, proj, bias, g)


def _diff_streams(j, r, q_ref, k_ref, bias_ref, lam, g_ref, o_ref, vaug_ref):
    rows = _block_rows(r)

    def map_query(s):
        qp = q_ref[rows, (s // 2) * LANES:(s // 2 + 1) * LANES]
        return jnp.where(_half_mask(qp.shape, s % 2), qp, jnp.zeros_like(qp))

    acc = [None] * (2 * B_HEADS)

    def map_done(s, result):
        acc[s] = result
        if s % 2 == 1:
            h = s // 2
            a0, a1 = acc[s - 1], acc[s]
            n0, l0 = a0[:, :B_V_DIM], a0[:, B_V_DIM:]
            n1, l1 = a1[:, :B_V_DIM], a1[:, B_V_DIM:]
            o = (n0 * l1 - lam * (n1 * l0)) / (l0 * l1)
            o_ref[rows, h * B_V_DIM:(h + 1) * B_V_DIM] = (
                _rms(o, g_ref[...]) * (1.0 - LAMBDA_INIT)).astype(BF16)

    return [_Stream(j=j, slot=s,
                    q_diag=functools.partial(map_query, s),
                    q_past=functools.partial(map_query, s),
                    k_tile=lambda n, s=s: k_ref[_block_rows(n), (s // 2) * LANES:(s // 2 + 1) * LANES],
                    v_tile=lambda n, s=s: vaug_ref[s // 2, _block_rows(n), :],
                    bias_tile=lambda which, s=s: bias_ref[s // 2, which],
                    done=functools.partial(map_done, s))
            for s in range(2 * B_HEADS)]


def _diff_kernel(q_ref, k_ref, v_ref, bias_ref, lam_ref, g_ref, o_ref, vaug_ref, s_ref, *, per_step):
    t = pl.program_id(1)

    def tiles(t_static):
        lp = lam_ref[...]
        lam = (jnp.exp(jnp.sum(lp[0:1] * lp[1:2], axis=1, keepdims=True))
               - jnp.exp(jnp.sum(lp[2:3] * lp[3:4], axis=1, keepdims=True)) + LAMBDA_INIT)
        streams = []
        for r in range(per_step):
            streams += _diff_streams(per_step * t_static + r, r, q_ref, k_ref, bias_ref, lam, g_ref,
                                     o_ref, vaug_ref)
        _two_pass_attention(streams, s_ref, n_diag_first=0)

    @pl.when(t == 0)
    def _first_tiles_of_batch():
        for h in range(B_HEADS):
            vaug_ref[h, :, :B_V_DIM] = v_ref[:, h * B_V_DIM:(h + 1) * B_V_DIM]
            vaug_ref[h, :, B_V_DIM:] = jnp.ones((v_ref.shape[0], LANES), BF16)
        tiles(0)

    for t_static in range(1, k_ref.shape[0] // (per_step * BLK)):
        pl.when(t == t_static)(functools.partial(tiles, t_static))


def _diff_attention(proj, bias, lam, g, batch, seq):
    nq = seq // BLK
    per_step = min(TILES_PER_STEP, nq)
    assert seq % (per_step * BLK) == 0 and B_V_DIM == LANES
    steps = nq // per_step
    first = (3 * A_WIDTH) // B_WIDTH
    return pl.pallas_call(
        functools.partial(_diff_kernel, per_step=per_step),
        grid=(batch, steps),
        in_specs=[pl.BlockSpec((per_step * BLK, B_WIDTH), lambda b, t: (b * steps + t, first)),
                  pl.BlockSpec((seq, B_WIDTH), lambda b, t: (b, first + 1)),
                  pl.BlockSpec((seq, B_WIDTH), lambda b, t: (b, first + 2)),
                  pl.BlockSpec((B_HEADS, 2, BLK, BLK), lambda b, t: (A_HEADS // B_HEADS, 0, 0, 0),
                               pipeline_mode=pl.Buffered(1)),
                  _resident(lam.shape),
                  _resident((1, B_V_DIM))],
        out_specs=pl.BlockSpec((per_step * BLK, B_WIDTH), lambda b, t: (b * steps + t, 0)),
        out_shape=jax.ShapeDtypeStruct((batch * seq, B_WIDTH), BF16),
        scratch_shapes=[pltpu.VMEM((B_HEADS, seq, 2 * LANES), BF16),
                        pltpu.VMEM((2 * B_HEADS, nq, BLK, BLK), F32)],
        compiler_params=_params(2),
        name="diff_attention",
    )(proj, proj, proj, bias, lam, g)


def _cross_kernel(x_ref, oa_ref, ob_ref, wo_ref, g_ref, wq_ref, mem_ref, gm_ref, wk_ref, wv_ref, wco_ref,
                  o_ref, kc_ref, vc_ref, *, scale, per_seq):
    @pl.when(pl.program_id(0) % per_seq == 0)
    def _memory_keys_values():
        m = _rms(mem_ref[...], gm_ref[...]).astype(BF16)
        kc_ref[...] = _wdot(m, wk_ref[...]).astype(BF16)
        vc_ref[...] = _wdot(m, wv_ref[...]).astype(BF16)

    x1 = (x_ref[...] + _wdot(oa_ref[...], wo_ref[:A_WIDTH, :]) + _wdot(ob_ref[...], wo_ref[A_WIDTH:, :]))
    hb = _rms(x1, g_ref[...]).astype(BF16)
    q = (_wdot(hb, wq_ref[...]) * scale).astype(BF16)
    hd = q.shape[1] // MEM_HEADS
    heads = []
    for h in range(MEM_HEADS):
        cols = slice(h * hd, (h + 1) * hd)
        s = _dot_nt(q[:, cols], kc_ref[:, cols])
        p = jnp.exp2(s - jnp.max(s, axis=1, keepdims=True))
        l = jnp.sum(p, axis=1, keepdims=True)
        heads.append((_dot(p.astype(BF16), vc_ref[:, cols]) / l).astype(BF16))
    o = jnp.concatenate(heads, axis=1)
    o_ref[...] = x1 + _wdot(o, wco_ref[...])


def _outproj_cross(x2d, oa, ob, wo, g, wq, mem2d, gm, wk, wv, wco, seq, mem_len):
    t, d = x2d.shape
    per_seq = seq // TM_CROSS
    assert seq % TM_CROSS == 0
    n = wk.shape[1]
    hd = wq.shape[1] // MEM_HEADS
    scale = hd ** -0.5 * LOG2E
    return pl.pallas_call(
        functools.partial(_cross_kernel, scale=scale, per_seq=per_seq),
        grid=(t // TM_CROSS,),
        in_specs=[pl.BlockSpec((TM_CROSS, d), lambda i: (i, 0)),
                  pl.BlockSpec((TM_CROSS, A_WIDTH), lambda i: (i, 0)),
                  pl.BlockSpec((TM_CROSS, B_WIDTH), lambda i: (i, 0)),
                  _resident(wo.shape), _resident((1, d)), _resident(wq.shape),
                  pl.BlockSpec((mem_len, d), lambda i: (i // per_seq, 0)),
                  _resident((1, d)), _resident(wk.shape), _resident(wv.shape),
                  _resident(wco.shape)],
        out_specs=pl.BlockSpec((TM_CROSS, d), lambda i: (i, 0)),
        out_shape=jax.ShapeDtypeStruct((t, d), F32),
        scratch_shapes=[pltpu.VMEM((mem_len, n), BF16),
                        pltpu.VMEM((mem_len, n), BF16)],
        compiler_params=_params(1),
        name="outproj_cross_attention",
    )(x2d, oa, ob, wo, g, wq, mem2d, gm, wk, wv, wco)


def _ffn_kernel(x_ref, g_ref, wg_ref, wu_ref, wd_ref, gf_ref, o_ref):
    x = x_ref[...]
    hb = _rms(x, g_ref[...]).astype(BF16)
    o_ref[...] = x
    for c in range(wg_ref.shape[1] // FF_CHUNK):
        cols = slice(c * FF_CHUNK, (c + 1) * FF_CHUNK)
        a = (jax.nn.silu(_wdot(hb, wg_ref[:, cols])) * _wdot(hb, wu_ref[:, cols])).astype(BF16)
        o_ref[...] += _wdot(a, wd_ref[cols, :])
    o_ref[...] = _rms(o_ref[...], gf_ref[...])


def _swiglu_final(x2d, g, wg, wu, wd, gf):
    t, d = x2d.shape
    assert t % TM_FFN == 0 and wg.shape[1] % FF_CHUNK == 0
    return pl.pallas_call(
        _ffn_kernel,
        grid=(t // TM_FFN,),
        in_specs=[pl.BlockSpec((TM_FFN, d), lambda i: (i, 0)),
                  _resident((1, d)), _resident(wg.shape), _resident(wu.shape), _resident(wd.shape),
                  _resident((1, d))],
        out_specs=pl.BlockSpec((TM_FFN, d), lambda i: (i, 0)),
        out_shape=jax.ShapeDtypeStruct((t, d), F32),
        compiler_params=_params(1),
        name="swiglu_final_norm",
    )(x2d, g, wg, wu, wd, gf)


def kernel(x, mem, mix_norm_g, w_in, moba_out_g, diff_lambda, diff_subln_g, w_out, rel_bias_table,
           cross_norm_g, mem_norm_g, w_cq, w_ck, w_cv, w_co, ffn_norm_g, w_gate, w_up, w_down,
           final_norm_g):
    batch, seq, d = x.shape
    mem_len = mem.shape[1]
    assert mix_norm_g.shape[0] == 1, "single-layer trunk"
    x2d = x.reshape(batch * seq, d)
    mem2d = mem.reshape(batch * mem_len, d)
    row = lambda v: v.reshape(1, -1).astype(F32)
    wb = lambda w: w[0].astype(F32)

    bias = _bias_tiles(rel_bias_table.astype(F32))
    proj = _in_projection(x2d, row(mix_norm_g[0]), wb(w_in))
    oa = _moba_attention(proj, bias, row(moba_out_g[0]), batch, seq)
    ob = _diff_attention(proj, bias, diff_lambda[0].astype(F32), row(diff_subln_g[0]), batch, seq)
    x2 = _outproj_cross(x2d, oa, ob, wb(w_out), row(cross_norm_g[0]), wb(w_cq),
                        mem2d, row(mem_norm_g[0]), wb(w_ck), wb(w_cv), wb(w_co), seq, mem_len)
    out = _swiglu_final(x2, row(ffn_norm_g[0]), wb(w_gate), wb(w_up), wb(w_down), row(final_norm_g))
    return out.reshape(batch, seq, d)
```

```python
import functools
import math
from typing import Callable, NamedTuple

import numpy as np
import jax
import jax.numpy as jnp
from jax import lax
from jax.experimental import pallas as pl
from jax.experimental.pallas import tpu as pltpu

F32 = jnp.float32
BF16 = jnp.bfloat16

A_HEADS = 8
A_HEAD_DIM = 64
A_WIDTH = A_HEADS * A_HEAD_DIM
MOBA_BLOCK = 256
MOBA_TOPK = 3
B_HEADS = 4
B_QK_DIM = 64
B_V_DIM = 2 * B_QK_DIM
B_WIDTH = B_HEADS * B_V_DIM
MEM_HEADS = 4
REL_BUCKETS = 32
REL_MAX_DIST = 128
EPS = 1e-6
NEG_INF = -1e30
LAMBDA_INIT = 0.8 - 0.6 * math.exp(-0.3 * 0)
QK_SCALE = A_HEAD_DIM ** -0.5
LOG2E = math.log2(math.e)

LANES = 128
SUBLANES = 8
VMEM_LIMIT_BYTES = 56 * 1024 * 1024

BLK = MOBA_BLOCK
TILES_PER_STEP = 4
TM_PROJ = 1024
TM_FFN = 1024
TM_CROSS = 512
FF_CHUNK = 256
PROJ_CHUNK = 512


def _dot(a, b):
    return jnp.dot(a, b, preferred_element_type=F32)


def _wdot(a, w):
    return jnp.dot(a, w.astype(BF16), preferred_element_type=F32)


def _dot_nt(a, b):
    return lax.dot_general(a, b, (((1,), (1,)), ((), ())), preferred_element_type=F32)


def _rms(x, g):
    return x * lax.rsqrt(jnp.mean(x * x, axis=-1, keepdims=True) + EPS) * g


def _params(n_axes):
    return pltpu.CompilerParams(dimension_semantics=("arbitrary",) * n_axes,
                                vmem_limit_bytes=VMEM_LIMIT_BYTES)


def _resident(shape):
    return pl.BlockSpec(shape, lambda *_: (0,) * len(shape), pipeline_mode=pl.Buffered(1))


def _rel_bucket_np(dist):
    n = np.maximum(dist, 0)
    max_exact = REL_BUCKETS // 2
    ratio = np.maximum(n, max_exact).astype(np.float32) / np.float32(max_exact)
    log_ratio = np.log(ratio) / np.float32(math.log(REL_MAX_DIST / max_exact))
    large = max_exact + (log_ratio * np.float32(REL_BUCKETS - max_exact)).astype(np.int32)
    large = np.minimum(large, REL_BUCKETS - 1)
    return np.where(n < max_exact, n, large).astype(np.int32)


HALF = BLK // 2


def _bucket_tiles():
    k = np.arange(HALF)[:, None]
    q = np.arange(HALF)[None, :]
    band = np.where(q - k >= 0, _rel_bucket_np(q - k), -1)
    corner = _rel_bucket_np(HALF + q - k)
    return np.stack([band, corner]).astype(np.int32)


def _bias_kernel(tab_ref, idx_ref, o_ref):
    assert HALF == REL_MAX_DIST
    idx = idx_ref[...]
    zero = jnp.zeros((HALF, HALF), F32)
    lo, hi = slice(0, HALF), slice(HALF, BLK)
    for h in range(o_ref.shape[0]):
        far = tab_ref[REL_BUCKETS - 1, h]
        acc = jnp.zeros(idx.shape, F32)
        for b in range(REL_BUCKETS - 1):
            acc = jnp.where(idx == b, (tab_ref[b, h] - far) * LOG2E, acc)
        band = jnp.where(idx[0] < 0, NEG_INF, acc[0])
        corner = acc[1]
        o_ref[h, 0, lo, lo] = band
        o_ref[h, 0, lo, hi] = corner
        o_ref[h, 0, hi, lo] = jnp.full((HALF, HALF), NEG_INF, F32)
        o_ref[h, 0, hi, hi] = band
        o_ref[h, 1, lo, lo] = zero
        o_ref[h, 1, lo, hi] = zero
        o_ref[h, 1, hi, lo] = corner
        o_ref[h, 1, hi, hi] = zero


def _bias_tiles(table):
    n_heads = table.shape[1]
    idx = jnp.asarray(_bucket_tiles())
    return pl.pallas_call(
        _bias_kernel,
        in_specs=[pl.BlockSpec(memory_space=pltpu.SMEM),
                  pl.BlockSpec(memory_space=pltpu.VMEM)],
        out_specs=pl.BlockSpec(memory_space=pltpu.VMEM),
        out_shape=jax.ShapeDtypeStruct((n_heads, 2, BLK, BLK), F32),
        compiler_params=pltpu.CompilerParams(vmem_limit_bytes=VMEM_LIMIT_BYTES),
        name="rel_bias_tiles",
    )(table, idx)


def _inproj_kernel(x_ref, g_ref, w_ref, o_ref, *, q_chunks):
    hb = _rms(x_ref[...], g_ref[...]).astype(BF16)
    for j in range(w_ref.shape[1] // PROJ_CHUNK):
        cols = slice(j * PROJ_CHUNK, (j + 1) * PROJ_CHUNK)
        acc = _wdot(hb, w_ref[:, cols])
        if j in q_chunks:
            acc = acc * (QK_SCALE * LOG2E)
        o_ref[:, cols] = acc.astype(BF16)


def _in_projection(x2d, g, w_bf16):
    t, d = x2d.shape
    n = w_bf16.shape[1]
    assert t % TM_PROJ == 0 and n % PROJ_CHUNK == 0
    q_chunks = (0, (3 * A_WIDTH) // PROJ_CHUNK)
    return pl.pallas_call(
        functools.partial(_inproj_kernel, q_chunks=q_chunks),
        grid=(t // TM_PROJ,),
        in_specs=[pl.BlockSpec((TM_PROJ, d), lambda i: (i, 0)),
                  _resident((1, d)),
                  _resident((d, n))],
        out_specs=pl.BlockSpec((TM_PROJ, n), lambda i: (i, 0)),
        out_shape=jax.ShapeDtypeStruct((t, n), BF16),
        compiler_params=_params(1),
        name="in_projection",
    )(x2d, g, w_bf16)


def _half_mask(shape, half):
    lane = lax.broadcasted_iota(jnp.int32, shape, 1)
    return lax.shift_right_logical(lane, 6) == half


def _block_rows(n):
    return slice(n * BLK, (n + 1) * BLK)


class _Stream(NamedTuple):
    j: int
    slot: int
    q_diag: Callable
    q_past: Callable
    k_tile: Callable
    v_tile: Callable
    bias_tile: Callable
    done: Callable


def _two_pass_attention(streams, s_ref, n_diag_first):
    lead = 2
    col_max = [None] * len(streams)
    results = [None] * len(streams)

    def blocks(i):
        return list(range(streams[i].j, -1, -1))

    def score_step(i, n):
        st = streams[i]
        s = _dot_nt(st.k_tile(n), st.q_diag() if n == st.j else st.q_past())
        if n >= st.j - 1:
            s = s + st.bias_tile(st.j - n)
        s_ref[st.slot, n] = s
        mx = jnp.max(s.reshape(BLK // SUBLANES, SUBLANES, BLK), axis=0)
        mx = mx if col_max[i] is None else jnp.maximum(col_max[i], mx)
        if n == 0:
            mx = jnp.max(mx, axis=0, keepdims=True)
        col_max[i] = mx

    def value_step(i, n):
        st = streams[i]
        p = jnp.exp2(s_ref[st.slot, n] - col_max[i]).astype(BF16)
        pv = _dot(st.v_tile(n), p)
        results[i] = pv if results[i] is None else results[i] + pv

    early = max(lead, n_diag_first)
    for i in range(early):
        score_step(i, streams[i].j)
    for i in range(lead):
        for n in blocks(i)[1:]:
            score_step(i, n)
    for i in range(len(streams)):
        a = i + lead
        ahead = [] if a >= len(streams) else blocks(a)[1:] if a < early else blocks(a)
        mine = blocks(i)
        for k in range(max(len(mine), len(ahead))):
            if k < len(mine):
                value_step(i, mine[k])
            if k < len(ahead):
                score_step(a, ahead[k])
        streams[i].done(results[i])


def _over_denominator(num, den):
    r = num.shape[0] // SUBLANES
    return (num.reshape(r, SUBLANES, BLK) / den[None]).reshape(num.shape)


def _times_row(x, row8):
    r = x.shape[0] // SUBLANES
    return (x.reshape(r, SUBLANES, BLK) * row8[None]).reshape(x.shape)


ONES_ROWS = 16


def _moba_streams(j, r, q_ref, bias_ref, g_ref, o_ref, kmt_ref, kaug_ref, vaug_ref, qaug_ref, topk):
    n_pairs = A_WIDTH // LANES
    rows = _block_rows(r)

    def own_block_query(h):
        qp = q_ref[rows, (h // 2) * LANES:(h // 2 + 1) * LANES]
        return jnp.where(_half_mask(qp.shape, h % 2), qp, jnp.zeros_like(qp))

    if j > topk:
        kmt = kmt_ref[...]
        kmt_hi = kmt.astype(BF16)
        kmt_lo = (kmt - kmt_hi.astype(F32)).astype(BF16)
        q_all = q_ref[rows, :]
        gate = _dot_nt(kmt_hi, q_all) + _dot_nt(kmt_lo, q_all)
        slabs = [gate[n * A_HEADS:(n + 1) * A_HEADS, :] for n in range(j)]
        sel_rows = []
        for n in range(j):
            rank = jnp.zeros(slabs[n].shape, F32)
            for m in range(j):
                if m != n:
                    beats = (slabs[m] >= slabs[n]) if m < n else (slabs[m] > slabs[n])
                    rank = rank + jnp.where(beats, 1.0, 0.0)
            sel_rows.append(jnp.where(rank < topk, 0.0, NEG_INF))
        sel_rows.append(jnp.zeros(((8 - j) * A_HEADS, BLK), F32))
        sel_bias = jnp.concatenate(sel_rows + sel_rows, axis=0).T.astype(BF16)
        base = r * A_HEADS
        for p in range(n_pairs):
            qp = q_ref[rows, p * LANES:(p + 1) * LANES]
            for half in range(2):
                qaug_ref[base + 2 * p + half] = jnp.where(_half_mask(qp.shape, half), qp, sel_bias)
        past_query = lambda h: qaug_ref[base + h]
    else:
        past_query = own_block_query

    heads = []

    def head_done(h, result):
        heads.append(_over_denominator(result[:A_HEAD_DIM], result[A_HEAD_DIM:A_HEAD_DIM + SUBLANES]))
        if len(heads) == A_HEADS:
            o = jnp.concatenate(heads, axis=0)
            inv = lax.rsqrt(jnp.mean(o * o, axis=0, keepdims=True) + EPS)
            g = g_ref[...]
            y = o * inv * jnp.concatenate([g, g], axis=1)
            o_ref[rows, :] = y.T.astype(BF16)

    return [_Stream(j=j, slot=h,
                    q_diag=functools.partial(own_block_query, h),
                    q_past=functools.partial(past_query, h),
                    k_tile=lambda n, h=h: kaug_ref[h, _block_rows(n), :],
                    v_tile=lambda n, h=h: vaug_ref[h, :, _block_rows(n)],
                    bias_tile=lambda which, h=h: bias_ref[h, which],
                    done=functools.partial(head_done, h))
            for h in range(A_HEADS)]


def _moba_kernel(q_ref, k_ref, v_ref, bias_ref, g_ref, o_ref,
                 kmt_ref, kaug_ref, vaug_ref, qaug_ref, s_ref, *, topk, per_step):
    t = pl.program_id(1)
    seq = k_ref.shape[0]
    nb = seq // BLK
    n_pairs = A_WIDTH // LANES

    def tiles(t_static):
        streams = []
        for r in range(per_step):
            streams += _moba_streams(per_step * t_static + r, r, q_ref, bias_ref, g_ref, o_ref,
                                     kmt_ref, kaug_ref, vaug_ref, qaug_ref, topk)
        _two_pass_attention(streams, s_ref, n_diag_first=A_HEADS if per_step * t_static > topk else 0)

    @pl.when(t == 0)
    def _first_tiles_of_batch():
        row = lax.broadcasted_iota(jnp.int32, (A_HEADS, A_WIDTH), 0)
        lane = lax.broadcasted_iota(jnp.int32, (A_HEADS, A_WIDTH), 1)
        head_mask = lax.shift_right_logical(lane, 6) == row
        kmt_ref[...] = jnp.zeros(kmt_ref.shape, F32)
        for n in range(nb):
            kb = k_ref[n * BLK:(n + 1) * BLK, :].astype(F32)
            km = jnp.sum(kb, axis=0, keepdims=True) * (1.0 / BLK)
            kmt_ref[n * A_HEADS:(n + 1) * A_HEADS, :] = jnp.where(head_mask, km, 0.0)
        lane = lax.broadcasted_iota(jnp.int32, (1, LANES), 1)
        for p in range(n_pairs):
            cols = slice(p * LANES, (p + 1) * LANES)
            vt = v_ref[:, cols].astype(F32).T
            for half in range(2):
                h = 2 * p + half
                vaug_ref[h, :A_HEAD_DIM, :] = vt[half * A_HEAD_DIM:(half + 1) * A_HEAD_DIM].astype(BF16)
                vaug_ref[h, A_HEAD_DIM:, :] = jnp.ones((ONES_ROWS, seq), BF16)
                own = lax.shift_right_logical(lane, 6) == half
                own_one = jnp.where(own, 1.0, 0.0).astype(BF16)
                for n in range(nb):
                    rows = slice(n * BLK, (n + 1) * BLK)
                    code = jnp.where((~own) & ((lane & 63) == n * A_HEADS + h), 1.0, 0.0).astype(BF16)
                    kaug_ref[h, rows, :] = k_ref[rows, cols] * own_one + code
        tiles(0)

    for t_static in range(1, nb // per_step):
        pl.when(t == t_static)(functools.partial(tiles, t_static))


def _moba_attention(proj, bias, g, batch, seq):
    nq = seq // BLK
    per_step = min(TILES_PER_STEP, nq)
    assert seq % (per_step * BLK) == 0 and nq <= 8
    topk = min(MOBA_TOPK, nq)
    steps = nq // per_step
    return pl.pallas_call(
        functools.partial(_moba_kernel, topk=topk, per_step=per_step),
        grid=(batch, steps),
        in_specs=[pl.BlockSpec((per_step * BLK, A_WIDTH), lambda b, t: (b * steps + t, 0)),
                  pl.BlockSpec((seq, A_WIDTH), lambda b, t: (b, 1)),
                  pl.BlockSpec((seq, A_WIDTH), lambda b, t: (b, 2)),
                  pl.BlockSpec((A_HEADS, 2, BLK, BLK), lambda b, t: (0, 0, 0, 0),
                               pipeline_mode=pl.Buffered(1)),
                  _resident(g.shape)],
        out_specs=pl.BlockSpec((per_step * BLK, A_WIDTH), lambda b, t: (b * steps + t, 0)),
        out_shape=jax.ShapeDtypeStruct((batch * seq, A_WIDTH), BF16),
        scratch_shapes=[pltpu.VMEM((8 * A_HEADS, A_WIDTH), F32),
                        pltpu.VMEM((A_HEADS, seq, LANES), BF16),
                        pltpu.VMEM((A_HEADS, A_HEAD_DIM + ONES_ROWS, seq), BF16),
                        pltpu.VMEM((per_step * A_HEADS, BLK, LANES), BF16),
                        pltpu.VMEM((A_HEADS, nq, BLK, BLK), F32)],
        compiler_params=_params(2),
        name="moba_attention",
    )(proj, proj, proj, bias, g)


def _diff_streams(j, r, q_ref, k_ref, bias_ref, lam8, g_ref, o_ref, vaug_ref):
    rows = _block_rows(r)

    def map_query(s):
        qp = q_ref[rows, (s // 2) * LANES:(s // 2 + 1) * LANES]
        return jnp.where(_half_mask(qp.shape, s % 2), qp, jnp.zeros_like(qp))

    acc = [None] * (2 * B_HEADS)

    def map_done(s, result):
        acc[s] = result
        if s % 2 == 1:
            h = s // 2
            a0, a1 = acc[s - 1], acc[s]
            n0, l0 = a0[:B_V_DIM], a0[B_V_DIM:B_V_DIM + SUBLANES]
            n1, l1 = a1[:B_V_DIM], a1[B_V_DIM:B_V_DIM + SUBLANES]
            o = _over_denominator(_times_row(n0, l1) - _times_row(n1, lam8 * l0), l0 * l1)
            inv = lax.rsqrt(jnp.mean(o * o, axis=0, keepdims=True) + EPS)
            g = g_ref[...]
            y = o * inv * jnp.concatenate([g, g], axis=1) * (1.0 - LAMBDA_INIT)
            o_ref[rows, h * B_V_DIM:(h + 1) * B_V_DIM] = y.T.astype(BF16)

    return [_Stream(j=j, slot=s,
                    q_diag=functools.partial(map_query, s),
                    q_past=functools.partial(map_query, s),
                    k_tile=lambda n, s=s: k_ref[_block_rows(n), (s // 2) * LANES:(s // 2 + 1) * LANES],
                    v_tile=lambda n, s=s: vaug_ref[s // 2, :, _block_rows(n)],
                    bias_tile=lambda which, s=s: bias_ref[s // 2, which],
                    done=functools.partial(map_done, s))
            for s in range(2 * B_HEADS)]


def _diff_kernel(q_ref, k_ref, v_ref, bias_ref, lam_ref, g_ref, o_ref, vaug_ref, s_ref, *, per_step):
    t = pl.program_id(1)
    seq = k_ref.shape[0]

    def tiles(t_static):
        lp = lam_ref[...]
        lam = (jnp.exp(jnp.sum(lp[0:1] * lp[1:2], axis=1, keepdims=True))
               - jnp.exp(jnp.sum(lp[2:3] * lp[3:4], axis=1, keepdims=True)) + LAMBDA_INIT)
        lam8 = jnp.broadcast_to(lam, (SUBLANES, BLK))
        streams = []
        for r in range(per_step):
            streams += _diff_streams(per_step * t_static + r, r, q_ref, k_ref, bias_ref, lam8, g_ref,
                                     o_ref, vaug_ref)
        _two_pass_attention(streams, s_ref, n_diag_first=0)

    @pl.when(t == 0)
    def _first_tiles_of_batch():
        for h in range(B_HEADS):
            vt = v_ref[:, h * B_V_DIM:(h + 1) * B_V_DIM].astype(F32).T
            vaug_ref[h, :B_V_DIM, :] = vt.astype(BF16)
            vaug_ref[h, B_V_DIM:, :] = jnp.ones((ONES_ROWS, seq), BF16)
        tiles(0)

    for t_static in range(1, seq // (per_step * BLK)):
        pl.when(t == t_static)(functools.partial(tiles, t_static))


def _diff_attention(proj, bias, lam, g, batch, seq):
    nq = seq // BLK
    per_step = min(TILES_PER_STEP, nq)
    assert seq % (per_step * BLK) == 0 and B_V_DIM == LANES
    steps = nq // per_step
    first = (3 * A_WIDTH) // B_WIDTH
    return pl.pallas_call(
        functools.partial(_diff_kernel, per_step=per_step),
        grid=(batch, steps),
        in_specs=[pl.BlockSpec((per_step * BLK, B_WIDTH), lambda b, t: (b * steps + t, first)),
                  pl.BlockSpec((seq, B_WIDTH), lambda b, t: (b, first + 1)),
                  pl.BlockSpec((seq, B_WIDTH), lambda b, t: (b, first + 2)),
                  pl.BlockSpec((B_HEADS, 2, BLK, BLK), lambda b, t: (A_HEADS // B_HEADS, 0, 0, 0),
                               pipeline_mode=pl.Buffered(1)),
                  _resident(lam.shape),
                  _resident(g.shape)],
        out_specs=pl.BlockSpec((per_step * BLK, B_WIDTH), lambda b, t: (b * steps + t, 0)),
        out_shape=jax.ShapeDtypeStruct((batch * seq, B_WIDTH), BF16),
        scratch_shapes=[pltpu.VMEM((B_HEADS, B_V_DIM + ONES_ROWS, seq), BF16),
                        pltpu.VMEM((2 * B_HEADS, nq, BLK, BLK), F32)],
        compiler_params=_params(2),
        name="diff_attention",
    )(proj, proj, proj, bias, lam, g)


def _cross_kernel(x_ref, oa_ref, ob_ref, wo_ref, g_ref, wq_ref, mem_ref, gm_ref, wk_ref, wv_ref, wco_ref,
                  o_ref, kc_ref, vc_ref, *, scale, per_seq):
    @pl.when(pl.program_id(0) % per_seq == 0)
    def _memory_keys_values():
        m = _rms(mem_ref[...], gm_ref[...]).astype(BF16)
        kc_ref[...] = _wdot(m, wk_ref[...]).astype(BF16)
        vc_ref[...] = _wdot(m, wv_ref[...]).astype(BF16)

    x1 = (x_ref[...] + _wdot(oa_ref[...], wo_ref[:A_WIDTH, :]) + _wdot(ob_ref[...], wo_ref[A_WIDTH:, :]))
    hb = _rms(x1, g_ref[...]).astype(BF16)
    q = (_wdot(hb, wq_ref[...]) * scale).astype(BF16)
    hd = q.shape[1] // MEM_HEADS
    heads = []
    for h in range(MEM_HEADS):
        cols = slice(h * hd, (h + 1) * hd)
        s = _dot_nt(q[:, cols], kc_ref[:, cols])
        p = jnp.exp2(s - jnp.max(s, axis=1, keepdims=True))
        l = jnp.sum(p, axis=1, keepdims=True)
        heads.append((_dot(p.astype(BF16), vc_ref[:, cols]) / l).astype(BF16))
    o = jnp.concatenate(heads, axis=1)
    o_ref[...] = x1 + _wdot(o, wco_ref[...])


def _outproj_cross(x2d, oa, ob, wo, g, wq, mem2d, gm, wk, wv, wco, seq, mem_len):
    t, d = x2d.shape
    per_seq = seq // TM_CROSS
    assert seq % TM_CROSS == 0
    n = wk.shape[1]
    hd = wq.shape[1] // MEM_HEADS
    scale = hd ** -0.5 * LOG2E
    return pl.pallas_call(
        functools.partial(_cross_kernel, scale=scale, per_seq=per_seq),
        grid=(t // TM_CROSS,),
        in_specs=[pl.BlockSpec((TM_CROSS, d), lambda i: (i, 0)),
                  pl.BlockSpec((TM_CROSS, A_WIDTH), lambda i: (i, 0)),
                  pl.BlockSpec((TM_CROSS, B_WIDTH), lambda i: (i, 0)),
                  _resident(wo.shape), _resident((1, d)), _resident(wq.shape),
                  pl.BlockSpec((mem_len, d), lambda i: (i // per_seq, 0)),
                  _resident((1, d)), _resident(wk.shape), _resident(wv.shape),
                  _resident(wco.shape)],
        out_specs=pl.BlockSpec((TM_CROSS, d), lambda i: (i, 0)),
        out_shape=jax.ShapeDtypeStruct((t, d), F32),
        scratch_shapes=[pltpu.VMEM((mem_len, n), BF16),
                        pltpu.VMEM((mem_len, n), BF16)],
        compiler_params=_params(1),
        name="outproj_cross_attention",
    )(x2d, oa, ob, wo, g, wq, mem2d, gm, wk, wv, wco)


def _ffn_kernel(x_ref, g_ref, wg_ref, wu_ref, wd_ref, gf_ref, o_ref):
    x = x_ref[...]
    hb = _rms(x, g_ref[...]).astype(BF16)
    o_ref[...] = x
    for c in range(wg_ref.shape[1] // FF_CHUNK):
        cols = slice(c * FF_CHUNK, (c + 1) * FF_CHUNK)
        a = (jax.nn.silu(_wdot(hb, wg_ref[:, cols])) * _wdot(hb, wu_ref[:, cols])).astype(BF16)
        o_ref[...] += _wdot(a, wd_ref[cols, :])
    o_ref[...] = _rms(o_ref[...], gf_ref[...])


def _swiglu_final(x2d, g, wg, wu, wd, gf):
    t, d = x2d.shape
    assert t % TM_FFN == 0 and wg.shape[1] % FF_CHUNK == 0
    return pl.pallas_call(
        _ffn_kernel,
        grid=(t // TM_FFN,),
        in_specs=[pl.BlockSpec((TM_FFN, d), lambda i: (i, 0)),
                  _resident((1, d)), _resident(wg.shape), _resident(wu.shape), _resident(wd.shape),
                  _resident((1, d))],
        out_specs=pl.BlockSpec((TM_FFN, d), lambda i: (i, 0)),
        out_shape=jax.ShapeDtypeStruct((t, d), F32),
        compiler_params=_params(1),
        name="swiglu_final_norm",
    )(x2d, g, wg, wu, wd, gf)


def kernel(x, mem, mix_norm_g, w_in, moba_out_g, diff_lambda, diff_subln_g, w_out, rel_bias_table,
           cross_norm_g, mem_norm_g, w_cq, w_ck, w_cv, w_co, ffn_norm_g, w_gate, w_up, w_down,
           final_norm_g):
    batch, seq, d = x.shape
    mem_len = mem.shape[1]
    assert mix_norm_g.shape[0] == 1, "single-layer trunk"
    x2d = x.reshape(batch * seq, d)
    mem2d = mem.reshape(batch * mem_len, d)
    row = lambda v: v.reshape(1, -1).astype(F32)
    wb = lambda w: w[0].astype(F32)

    bias = _bias_tiles(rel_bias_table.astype(F32))
    proj = _in_projection(x2d, row(mix_norm_g[0]), wb(w_in))
    col = lambda v: jnp.broadcast_to(v.reshape(-1, 1).astype(F32), (v.size, LANES))
    oa = _moba_attention(proj, bias, col(moba_out_g[0]), batch, seq)
    ob = _diff_attention(proj, bias, diff_lambda[0].astype(F32), col(diff_subln_g[0]), batch, seq)
    x2 = _outproj_cross(x2d, oa, ob, wb(w_out), row(cross_norm_g[0]), wb(w_cq),
                        mem2d, row(mem_norm_g[0]), wb(w_ck), wb(w_cv), wb(w_co), seq, mem_len)
    out = _swiglu_final(x2, row(ffn_norm_g[0]), wb(w_gate), wb(w_up), wb(w_down), row(final_norm_g))
    return out.reshape(batch, seq, d)
```

```python
import functools
import math
from typing import Callable, NamedTuple

import numpy as np
import jax
import jax.numpy as jnp
from jax import lax
from jax.experimental import pallas as pl
from jax.experimental.pallas import tpu as pltpu

F32 = jnp.float32
BF16 = jnp.bfloat16

A_HEADS = 8
A_HEAD_DIM = 64
A_WIDTH = A_HEADS * A_HEAD_DIM
MOBA_BLOCK = 256
MOBA_TOPK = 3
B_HEADS = 4
B_QK_DIM = 64
B_V_DIM = 2 * B_QK_DIM
B_WIDTH = B_HEADS * B_V_DIM
MEM_HEADS = 4
REL_BUCKETS = 32
REL_MAX_DIST = 128
EPS = 1e-6
NEG_INF = -1e30
LAMBDA_INIT = 0.8 - 0.6 * math.exp(-0.3 * 0)
QK_SCALE = A_HEAD_DIM ** -0.5
LOG2E = math.log2(math.e)

LANES = 128
SUBLANES = 8
VMEM_LIMIT_BYTES = 56 * 1024 * 1024

BLK = MOBA_BLOCK
TILES_PER_STEP = 4
SCORE_GROUP = 2
TM_PROJ = 1024
TM_FFN = 1024
TM_CROSS = 512
FF_CHUNK = 256
PROJ_CHUNK = 512


def _dot(a, b):
    return jnp.dot(a, b, preferred_element_type=F32)


def _wdot(a, w):
    return jnp.dot(a, w.astype(BF16), preferred_element_type=F32)


def _dot_nt(a, b):
    return lax.dot_general(a, b, (((1,), (1,)), ((), ())), preferred_element_type=F32)


def _rms(x, g):
    return x * lax.rsqrt(jnp.mean(x * x, axis=-1, keepdims=True) + EPS) * g


def _params(n_axes):
    return pltpu.CompilerParams(dimension_semantics=("arbitrary",) * n_axes,
                                vmem_limit_bytes=VMEM_LIMIT_BYTES)


def _resident(shape):
    return pl.BlockSpec(shape, lambda *_: (0,) * len(shape), pipeline_mode=pl.Buffered(1))


def _rel_bucket_np(dist):
    n = np.maximum(dist, 0)
    max_exact = REL_BUCKETS // 2
    ratio = np.maximum(n, max_exact).astype(np.float32) / np.float32(max_exact)
    log_ratio = np.log(ratio) / np.float32(math.log(REL_MAX_DIST / max_exact))
    large = max_exact + (log_ratio * np.float32(REL_BUCKETS - max_exact)).astype(np.int32)
    large = np.minimum(large, REL_BUCKETS - 1)
    return np.where(n < max_exact, n, large).astype(np.int32)


HALF = BLK // 2


def _bucket_tiles():
    k = np.arange(HALF)[:, None]
    q = np.arange(HALF)[None, :]
    band = np.where(q - k >= 0, _rel_bucket_np(q - k), -1)
    corner = _rel_bucket_np(HALF + q - k)
    return np.stack([band, corner]).astype(np.int32)


def _bias_kernel(tab_ref, idx_ref, o_ref):
    assert HALF == REL_MAX_DIST
    idx = idx_ref[...]
    zero = jnp.zeros((HALF, HALF), F32)
    lo, hi = slice(0, HALF), slice(HALF, BLK)
    for h in range(o_ref.shape[0]):
        far = tab_ref[REL_BUCKETS - 1, h]
        acc = jnp.zeros(idx.shape, F32)
        for b in range(REL_BUCKETS - 1):
            acc = jnp.where(idx == b, (tab_ref[b, h] - far) * LOG2E, acc)
        band = jnp.where(idx[0] < 0, NEG_INF, acc[0])
        corner = acc[1]
        o_ref[h, 0, lo, lo] = band
        o_ref[h, 0, lo, hi] = corner
        o_ref[h, 0, hi, lo] = jnp.full((HALF, HALF), NEG_INF, F32)
        o_ref[h, 0, hi, hi] = band
        o_ref[h, 1, lo, lo] = zero
        o_ref[h, 1, lo, hi] = zero
        o_ref[h, 1, hi, lo] = corner
        o_ref[h, 1, hi, hi] = zero


def _bias_tiles(table):
    n_heads = table.shape[1]
    idx = jnp.asarray(_bucket_tiles())
    return pl.pallas_call(
        _bias_kernel,
        in_specs=[pl.BlockSpec(memory_space=pltpu.SMEM),
                  pl.BlockSpec(memory_space=pltpu.VMEM)],
        out_specs=pl.BlockSpec(memory_space=pltpu.VMEM),
        out_shape=jax.ShapeDtypeStruct((n_heads, 2, BLK, BLK), F32),
        compiler_params=pltpu.CompilerParams(vmem_limit_bytes=VMEM_LIMIT_BYTES),
        name="rel_bias_tiles",
    )(table, idx)


def _inproj_kernel(x_ref, g_ref, w_ref, o_ref, *, q_chunks):
    hb = _rms(x_ref[...], g_ref[...]).astype(BF16)
    for j in range(w_ref.shape[1] // PROJ_CHUNK):
        cols = slice(j * PROJ_CHUNK, (j + 1) * PROJ_CHUNK)
        acc = _wdot(hb, w_ref[:, cols])
        if j in q_chunks:
            acc = acc * (QK_SCALE * LOG2E)
        o_ref[:, cols] = acc.astype(BF16)


def _in_projection(x2d, g, w_bf16):
    t, d = x2d.shape
    n = w_bf16.shape[1]
    assert t % TM_PROJ == 0 and n % PROJ_CHUNK == 0
    q_chunks = (0, (3 * A_WIDTH) // PROJ_CHUNK)
    return pl.pallas_call(
        functools.partial(_inproj_kernel, q_chunks=q_chunks),
        grid=(t // TM_PROJ,),
        in_specs=[pl.BlockSpec((TM_PROJ, d), lambda i: (i, 0)),
                  _resident((1, d)),
                  _resident((d, n))],
        out_specs=pl.BlockSpec((TM_PROJ, n), lambda i: (i, 0)),
        out_shape=jax.ShapeDtypeStruct((t, n), BF16),
        compiler_params=_params(1),
        name="in_projection",
    )(x2d, g, w_bf16)


def _half_mask(shape, half):
    lane = lax.broadcasted_iota(jnp.int32, shape, 1)
    return lax.shift_right_logical(lane, 6) == half


def _block_rows(n):
    return slice(n * BLK, (n + 1) * BLK)


class _Stream(NamedTuple):
    j: int
    slot: int
    q_diag: Callable
    q_past: Callable
    k_rows: Callable
    v_tile: Callable
    bias_tile: Callable
    done: Callable


def _two_pass_attention(streams, s_ref, n_diag_first):
    lead = 2
    col_max = [None] * len(streams)
    results = [None] * len(streams)

    def blocks(i):
        return list(range(streams[i].j, -1, -1))

    def score_groups(i):
        j = streams[i].j
        out = [(j, j)]
        hi = j - 1
        while hi >= 0:
            lo = max(hi - SCORE_GROUP + 1, 0)
            out.append((lo, hi))
            hi = lo - 1
        return out

    def score_step(i, group):
        lo, hi = group
        st = streams[i]
        run = _dot_nt(st.k_rows(lo, hi + 1), st.q_diag() if hi == st.j else st.q_past())
        mx = col_max[i]
        for n in range(hi, lo - 1, -1):
            s = run[(n - lo) * BLK:(n - lo + 1) * BLK]
            if n >= st.j - 1:
                s = s + st.bias_tile(st.j - n)
            s_ref[st.slot, n] = s
            top = jnp.max(s.reshape(BLK // SUBLANES, SUBLANES, BLK), axis=0)
            mx = top if mx is None else jnp.maximum(mx, top)
        if lo == 0:
            mx = jnp.max(mx, axis=0, keepdims=True)
        col_max[i] = mx

    def value_step(i, n):
        st = streams[i]
        p = jnp.exp2(s_ref[st.slot, n] - col_max[i]).astype(BF16)
        pv = _dot(st.v_tile(n), p)
        results[i] = pv if results[i] is None else results[i] + pv

    early = max(lead, n_diag_first)
    for i in range(early):
        score_step(i, score_groups(i)[0])
    for i in range(lead):
        for group in score_groups(i)[1:]:
            score_step(i, group)
    for i in range(len(streams)):
        a = i + lead
        ahead = [] if a >= len(streams) else score_groups(a)[1:] if a < early else score_groups(a)
        mine = blocks(i)
        for k in range(max(len(mine), len(ahead))):
            if k < len(mine):
                value_step(i, mine[k])
            if k < len(ahead):
                score_step(a, ahead[k])
        streams[i].done(results[i])


def _over_denominator(num, den):
    r = num.shape[0] // SUBLANES
    return (num.reshape(r, SUBLANES, BLK) / den[None]).reshape(num.shape)


def _times_row(x, row8):
    r = x.shape[0] // SUBLANES
    return (x.reshape(r, SUBLANES, BLK) * row8[None]).reshape(x.shape)


ONES_ROWS = 16


def _moba_streams(j, r, q_ref, bias_ref, g_ref, o_ref, kmt_ref, kaug_ref, vaug_ref, qaug_ref, topk):
    n_pairs = A_WIDTH // LANES
    rows = _block_rows(r)

    def own_block_query(h):
        qp = q_ref[rows, (h // 2) * LANES:(h // 2 + 1) * LANES]
        return jnp.where(_half_mask(qp.shape, h % 2), qp, jnp.zeros_like(qp))

    if j > topk:
        kmt = kmt_ref[...]
        kmt_hi = kmt.astype(BF16)
        kmt_lo = (kmt - kmt_hi.astype(F32)).astype(BF16)
        q_all = q_ref[rows, :]
        gate = _dot_nt(kmt_hi, q_all) + _dot_nt(kmt_lo, q_all)
        slabs = [gate[n * A_HEADS:(n + 1) * A_HEADS, :] for n in range(j)]
        sel_rows = []
        for n in range(j):
            rank = jnp.zeros(slabs[n].shape, F32)
            for m in range(j):
                if m != n:
                    beats = (slabs[m] >= slabs[n]) if m < n else (slabs[m] > slabs[n])
                    rank = rank + jnp.where(beats, 1.0, 0.0)
            sel_rows.append(jnp.where(rank < topk, 0.0, NEG_INF))
        sel_rows.append(jnp.zeros(((8 - j) * A_HEADS, BLK), F32))
        sel_bias = jnp.concatenate(sel_rows + sel_rows, axis=0).T.astype(BF16)
        base = r * A_HEADS
        for p in range(n_pairs):
            qp = q_ref[rows, p * LANES:(p + 1) * LANES]
            for half in range(2):
                qaug_ref[base + 2 * p + half] = jnp.where(_half_mask(qp.shape, half), qp, sel_bias)
        past_query = lambda h: qaug_ref[base + h]
    else:
        past_query = own_block_query

    heads = []

    def head_done(h, result):
        heads.append(_over_denominator(result[:A_HEAD_DIM], result[A_HEAD_DIM:A_HEAD_DIM + SUBLANES]))
        if len(heads) == A_HEADS:
            o = jnp.concatenate(heads, axis=0)
            inv = lax.rsqrt(jnp.mean(o * o, axis=0, keepdims=True) + EPS)
            g = g_ref[...]
            y = o * inv * jnp.concatenate([g, g], axis=1)
            o_ref[rows, :] = y.T.astype(BF16)

    return [_Stream(j=j, slot=h,
                    q_diag=functools.partial(own_block_query, h),
                    q_past=functools.partial(past_query, h),
                    k_rows=lambda lo, hi, h=h: kaug_ref[h, lo * BLK:hi * BLK, :],
                    v_tile=lambda n, h=h: vaug_ref[h, :, _block_rows(n)],
                    bias_tile=lambda which, h=h: bias_ref[h, which],
                    done=functools.partial(head_done, h))
            for h in range(A_HEADS)]


def _moba_kernel(q_ref, k_ref, v_ref, bias_ref, g_ref, o_ref,
                 kmt_ref, kaug_ref, vaug_ref, qaug_ref, s_ref, *, topk, per_step):
    t = pl.program_id(1)
    seq = k_ref.shape[0]
    nb = seq // BLK
    n_pairs = A_WIDTH // LANES

    def tiles(t_static):
        streams = []
        for r in range(per_step):
            streams += _moba_streams(per_step * t_static + r, r, q_ref, bias_ref, g_ref, o_ref,
                                     kmt_ref, kaug_ref, vaug_ref, qaug_ref, topk)
        _two_pass_attention(streams, s_ref, n_diag_first=A_HEADS if per_step * t_static > topk else 0)

    @pl.when(t == 0)
    def _first_tiles_of_batch():
        row = lax.broadcasted_iota(jnp.int32, (A_HEADS, A_WIDTH), 0)
        lane = lax.broadcasted_iota(jnp.int32, (A_HEADS, A_WIDTH), 1)
        head_mask = lax.shift_right_logical(lane, 6) == row
        kmt_ref[...] = jnp.zeros(kmt_ref.shape, F32)
        for n in range(nb):
            kb = k_ref[n * BLK:(n + 1) * BLK, :].astype(F32)
            km = jnp.sum(kb, axis=0, keepdims=True) * (1.0 / BLK)
            kmt_ref[n * A_HEADS:(n + 1) * A_HEADS, :] = jnp.where(head_mask, km, 0.0)
        lane = lax.broadcasted_iota(jnp.int32, (1, LANES), 1)
        for p in range(n_pairs):
            cols = slice(p * LANES, (p + 1) * LANES)
            vt = v_ref[:, cols].astype(F32).T
            for half in range(2):
                h = 2 * p + half
                vaug_ref[h, :A_HEAD_DIM, :] = vt[half * A_HEAD_DIM:(half + 1) * A_HEAD_DIM].astype(BF16)
                vaug_ref[h, A_HEAD_DIM:, :] = jnp.ones((ONES_ROWS, seq), BF16)
                own = lax.shift_right_logical(lane, 6) == half
                own_one = jnp.where(own, 1.0, 0.0).astype(BF16)
                for n in range(nb):
                    rows = slice(n * BLK, (n + 1) * BLK)
                    code = jnp.where((~own) & ((lane & 63) == n * A_HEADS + h), 1.0, 0.0).astype(BF16)
                    kaug_ref[h, rows, :] = k_ref[rows, cols] * own_one + code
        tiles(0)

    for t_static in range(1, nb // per_step):
        pl.when(t == t_static)(functools.partial(tiles, t_static))


def _moba_attention(proj, bias, g, batch, seq):
    nq = seq // BLK
    per_step = min(TILES_PER_STEP, nq)
    assert seq % (per_step * BLK) == 0 and nq <= 8
    topk = min(MOBA_TOPK, nq)
    steps = nq // per_step
    return pl.pallas_call(
        functools.partial(_moba_kernel, topk=topk, per_step=per_step),
        grid=(batch, steps),
        in_specs=[pl.BlockSpec((per_step * BLK, A_WIDTH), lambda b, t: (b * steps + t, 0)),
                  pl.BlockSpec((seq, A_WIDTH), lambda b, t: (b, 1)),
                  pl.BlockSpec((seq, A_WIDTH), lambda b, t: (b, 2)),
                  pl.BlockSpec((A_HEADS, 2, BLK, BLK), lambda b, t: (0, 0, 0, 0),
                               pipeline_mode=pl.Buffered(1)),
                  _resident(g.shape)],
        out_specs=pl.BlockSpec((per_step * BLK, A_WIDTH), lambda b, t: (b * steps + t, 0)),
        out_shape=jax.ShapeDtypeStruct((batch * seq, A_WIDTH), BF16),
        scratch_shapes=[pltpu.VMEM((8 * A_HEADS, A_WIDTH), F32),
                        pltpu.VMEM((A_HEADS, seq, LANES), BF16),
                        pltpu.VMEM((A_HEADS, A_HEAD_DIM + ONES_ROWS, seq), BF16),
                        pltpu.VMEM((per_step * A_HEADS, BLK, LANES), BF16),
                        pltpu.VMEM((A_HEADS, nq, BLK, BLK), F32)],
        compiler_params=_params(2),
        name="moba_attention",
    )(proj, proj, proj, bias, g)


def _diff_streams(j, r, q_ref, k_ref, bias_ref, lam8, g_ref, o_ref, vaug_ref):
    rows = _block_rows(r)

    def map_query(s):
        qp = q_ref[rows, (s // 2) * LANES:(s // 2 + 1) * LANES]
        return jnp.where(_half_mask(qp.shape, s % 2), qp, jnp.zeros_like(qp))

    acc = [None] * (2 * B_HEADS)

    def map_done(s, result):
        acc[s] = result
        if s % 2 == 1:
            h = s // 2
            a0, a1 = acc[s - 1], acc[s]
            n0, l0 = a0[:B_V_DIM], a0[B_V_DIM:B_V_DIM + SUBLANES]
            n1, l1 = a1[:B_V_DIM], a1[B_V_DIM:B_V_DIM + SUBLANES]
            o = _over_denominator(_times_row(n0, l1) - _times_row(n1, lam8 * l0), l0 * l1)
            inv = lax.rsqrt(jnp.mean(o * o, axis=0, keepdims=True) + EPS)
            g = g_ref[...]
            y = o * inv * jnp.concatenate([g, g], axis=1) * (1.0 - LAMBDA_INIT)
            o_ref[rows, h * B_V_DIM:(h + 1) * B_V_DIM] = y.T.astype(BF16)

    return [_Stream(j=j, slot=s,
                    q_diag=functools.partial(map_query, s),
                    q_past=functools.partial(map_query, s),
                    k_rows=lambda lo, hi, s=s: k_ref[lo * BLK:hi * BLK, (s // 2) * LANES:(s // 2 + 1) * LANES],
                    v_tile=lambda n, s=s: vaug_ref[s // 2, :, _block_rows(n)],
                    bias_tile=lambda which, s=s: bias_ref[s // 2, which],
                    done=functools.partial(map_done, s))
            for s in range(2 * B_HEADS)]


def _diff_kernel(q_ref, k_ref, v_ref, bias_ref, lam_ref, g_ref, o_ref, vaug_ref, s_ref, *, per_step):
    t = pl.program_id(1)
    seq = k_ref.shape[0]

    def tiles(t_static):
        lp = lam_ref[...]
        lam = (jnp.exp(jnp.sum(lp[0:1] * lp[1:2], axis=1, keepdims=True))
               - jnp.exp(jnp.sum(lp[2:3] * lp[3:4], axis=1, keepdims=True)) + LAMBDA_INIT)
        lam8 = jnp.broadcast_to(lam, (SUBLANES, BLK))
        streams = []
        for r in range(per_step):
            streams += _diff_streams(per_step * t_static + r, r, q_ref, k_ref, bias_ref, lam8, g_ref,
                                     o_ref, vaug_ref)
        _two_pass_attention(streams, s_ref, n_diag_first=0)

    @pl.when(t == 0)
    def _first_tiles_of_batch():
        for h in range(B_HEADS):
            vt = v_ref[:, h * B_V_DIM:(h + 1) * B_V_DIM].astype(F32).T
            vaug_ref[h, :B_V_DIM, :] = vt.astype(BF16)
            vaug_ref[h, B_V_DIM:, :] = jnp.ones((ONES_ROWS, seq), BF16)
        tiles(0)

    for t_static in range(1, seq // (per_step * BLK)):
        pl.when(t == t_static)(functools.partial(tiles, t_static))


def _diff_attention(proj, bias, lam, g, batch, seq):
    nq = seq // BLK
    per_step = min(TILES_PER_STEP, nq)
    assert seq % (per_step * BLK) == 0 and B_V_DIM == LANES
    steps = nq // per_step
    first = (3 * A_WIDTH) // B_WIDTH
    return pl.pallas_call(
        functools.partial(_diff_kernel, per_step=per_step),
        grid=(batch, steps),
        in_specs=[pl.BlockSpec((per_step * BLK, B_WIDTH), lambda b, t: (b * steps + t, first)),
                  pl.BlockSpec((seq, B_WIDTH), lambda b, t: (b, first + 1)),
                  pl.BlockSpec((seq, B_WIDTH), lambda b, t: (b, first + 2)),
                  pl.BlockSpec((B_HEADS, 2, BLK, BLK), lambda b, t: (A_HEADS // B_HEADS, 0, 0, 0),
                               pipeline_mode=pl.Buffered(1)),
                  _resident(lam.shape),
                  _resident(g.shape)],
        out_specs=pl.BlockSpec((per_step * BLK, B_WIDTH), lambda b, t: (b * steps + t, 0)),
        out_shape=jax.ShapeDtypeStruct((batch * seq, B_WIDTH), BF16),
        scratch_shapes=[pltpu.VMEM((B_HEADS, B_V_DIM + ONES_ROWS, seq), BF16),
                        pltpu.VMEM((2 * B_HEADS, nq, BLK, BLK), F32)],
        compiler_params=_params(2),
        name="diff_attention",
    )(proj, proj, proj, bias, lam, g)


def _cross_kernel(x_ref, oa_ref, ob_ref, wo_ref, g_ref, wq_ref, mem_ref, gm_ref, wk_ref, wv_ref, wco_ref,
                  o_ref, kc_ref, vc_ref, *, scale, per_seq):
    @pl.when(pl.program_id(0) % per_seq == 0)
    def _memory_keys_values():
        m = _rms(mem_ref[...], gm_ref[...]).astype(BF16)
        kc_ref[...] = _wdot(m, wk_ref[...]).astype(BF16)
        vc_ref[...] = _wdot(m, wv_ref[...]).astype(BF16)

    x1 = (x_ref[...] + _wdot(oa_ref[...], wo_ref[:A_WIDTH, :]) + _wdot(ob_ref[...], wo_ref[A_WIDTH:, :]))
    hb = _rms(x1, g_ref[...]).astype(BF16)
    q = (_wdot(hb, wq_ref[...]) * scale).astype(BF16)
    hd = q.shape[1] // MEM_HEADS
    heads = []
    for h in range(MEM_HEADS):
        cols = slice(h * hd, (h + 1) * hd)
        s = _dot_nt(q[:, cols], kc_ref[:, cols])
        p = jnp.exp2(s - jnp.max(s, axis=1, keepdims=True))
        l = jnp.sum(p, axis=1, keepdims=True)
        heads.append((_dot(p.astype(BF16), vc_ref[:, cols]) / l).astype(BF16))
    o = jnp.concatenate(heads, axis=1)
    o_ref[...] = x1 + _wdot(o, wco_ref[...])


def _outproj_cross(x2d, oa, ob, wo, g, wq, mem2d, gm, wk, wv, wco, seq, mem_len):
    t, d = x2d.shape
    per_seq = seq // TM_CROSS
    assert seq % TM_CROSS == 0
    n = wk.shape[1]
    hd = wq.shape[1] // MEM_HEADS
    scale = hd ** -0.5 * LOG2E
    return pl.pallas_call(
        functools.partial(_cross_kernel, scale=scale, per_seq=per_seq),
        grid=(t // TM_CROSS,),
        in_specs=[pl.BlockSpec((TM_CROSS, d), lambda i: (i, 0)),
                  pl.BlockSpec((TM_CROSS, A_WIDTH), lambda i: (i, 0)),
                  pl.BlockSpec((TM_CROSS, B_WIDTH), lambda i: (i, 0)),
                  _resident(wo.shape), _resident((1, d)), _resident(wq.shape),
                  pl.BlockSpec((mem_len, d), lambda i: (i // per_seq, 0)),
                  _resident((1, d)), _resident(wk.shape), _resident(wv.shape),
                  _resident(wco.shape)],
        out_specs=pl.BlockSpec((TM_CROSS, d), lambda i: (i, 0)),
        out_shape=jax.ShapeDtypeStruct((t, d), F32),
        scratch_shapes=[pltpu.VMEM((mem_len, n), BF16),
                        pltpu.VMEM((mem_len, n), BF16)],
        compiler_params=_params(1),
        name="outproj_cross_attention",
    )(x2d, oa, ob, wo, g, wq, mem2d, gm, wk, wv, wco)


def _ffn_kernel(x_ref, g_ref, wg_ref, wu_ref, wd_ref, gf_ref, o_ref):
    x = x_ref[...]
    hb = _rms(x, g_ref[...]).astype(BF16)
    o_ref[...] = x
    for c in range(wg_ref.shape[1] // FF_CHUNK):
        cols = slice(c * FF_CHUNK, (c + 1) * FF_CHUNK)
        a = (jax.nn.silu(_wdot(hb, wg_ref[:, cols])) * _wdot(hb, wu_ref[:, cols])).astype(BF16)
        o_ref[...] += _wdot(a, wd_ref[cols, :])
    o_ref[...] = _rms(o_ref[...], gf_ref[...])


def _swiglu_final(x2d, g, wg, wu, wd, gf):
    t, d = x2d.shape
    assert t % TM_FFN == 0 and wg.shape[1] % FF_CHUNK == 0
    return pl.pallas_call(
        _ffn_kernel,
        grid=(t // TM_FFN,),
        in_specs=[pl.BlockSpec((TM_FFN, d), lambda i: (i, 0)),
                  _resident((1, d)), _resident(wg.shape), _resident(wu.shape), _resident(wd.shape),
                  _resident((1, d))],
        out_specs=pl.BlockSpec((TM_FFN, d), lambda i: (i, 0)),
        out_shape=jax.ShapeDtypeStruct((t, d), F32),
        compiler_params=_params(1),
        name="swiglu_final_norm",
    )(x2d, g, wg, wu, wd, gf)


def kernel(x, mem, mix_norm_g, w_in, moba_out_g, diff_lambda, diff_subln_g, w_out, rel_bias_table,
           cross_norm_g, mem_norm_g, w_cq, w_ck, w_cv, w_co, ffn_norm_g, w_gate, w_up, w_down,
           final_norm_g):
    batch, seq, d = x.shape
    mem_len = mem.shape[1]
    assert mix_norm_g.shape[0] == 1, "single-layer trunk"
    x2d = x.reshape(batch * seq, d)
    mem2d = mem.reshape(batch * mem_len, d)
    row = lambda v: v.reshape(1, -1).astype(F32)
    wb = lambda w: w[0].astype(F32)

    bias = _bias_tiles(rel_bias_table.astype(F32))
    proj = _in_projection(x2d, row(mix_norm_g[0]), wb(w_in))
    col = lambda v: jnp.broadcast_to(v.reshape(-1, 1).astype(F32), (v.size, LANES))
    oa = _moba_attention(proj, bias, col(moba_out_g[0]), batch, seq)
    ob = _diff_attention(proj, bias, diff_lambda[0].astype(F32), col(diff_subln_g[0]), batch, seq)
    x2 = _outproj_cross(x2d, oa, ob, wb(w_out), row(cross_norm_g[0]), wb(w_cq),
                        mem2d, row(mem_norm_g[0]), wb(w_ck), wb(w_cv), wb(w_co), seq, mem_len)
    out = _swiglu_final(x2, row(ffn_norm_g[0]), wb(w_gate), wb(w_up), wb(w_down), row(final_norm_g))
    return out.reshape(batch, seq, d)
```

```python
import functools
import math
from typing import Callable, NamedTuple

import numpy as np
import jax
import jax.numpy as jnp
from jax import lax
from jax.experimental import pallas as pl
from jax.experimental.pallas import tpu as pltpu

F32 = jnp.float32
BF16 = jnp.bfloat16

A_HEADS = 8
A_HEAD_DIM = 64
A_WIDTH = A_HEADS * A_HEAD_DIM
MOBA_BLOCK = 256
MOBA_TOPK = 3
B_HEADS = 4
B_QK_DIM = 64
B_V_DIM = 2 * B_QK_DIM
B_WIDTH = B_HEADS * B_V_DIM
MEM_HEADS = 4
REL_BUCKETS = 32
REL_MAX_DIST = 128
EPS = 1e-6
NEG_INF = -1e30
LAMBDA_INIT = 0.8 - 0.6 * math.exp(-0.3 * 0)
QK_SCALE = A_HEAD_DIM ** -0.5
LOG2E = math.log2(math.e)

LANES = 128
SUBLANES = 8
VMEM_LIMIT_BYTES = 56 * 1024 * 1024

BLK = MOBA_BLOCK
TILES_PER_STEP = 4
TM_PROJ = 1024
TM_FFN = 1024
TM_CROSS = 512
FF_CHUNK = 256
PROJ_CHUNK = 512


def _dot(a, b):
    return jnp.dot(a, b, preferred_element_type=F32)


def _wdot(a, w):
    return jnp.dot(a, w.astype(BF16), preferred_element_type=F32)


def _dot_nt(a, b):
    return lax.dot_general(a, b, (((1,), (1,)), ((), ())), preferred_element_type=F32)


def _rms(x, g):
    return x * lax.rsqrt(jnp.mean(x * x, axis=-1, keepdims=True) + EPS) * g


def _params(n_axes):
    return pltpu.CompilerParams(dimension_semantics=("arbitrary",) * n_axes,
                                vmem_limit_bytes=VMEM_LIMIT_BYTES)


def _resident(shape):
    return pl.BlockSpec(shape, lambda *_: (0,) * len(shape), pipeline_mode=pl.Buffered(1))


def _rel_bucket_np(dist):
    n = np.maximum(dist, 0)
    max_exact = REL_BUCKETS // 2
    ratio = np.maximum(n, max_exact).astype(np.float32) / np.float32(max_exact)
    log_ratio = np.log(ratio) / np.float32(math.log(REL_MAX_DIST / max_exact))
    large = max_exact + (log_ratio * np.float32(REL_BUCKETS - max_exact)).astype(np.int32)
    large = np.minimum(large, REL_BUCKETS - 1)
    return np.where(n < max_exact, n, large).astype(np.int32)


HALF = BLK // 2


def _bucket_tiles():
    k = np.arange(HALF)[:, None]
    q = np.arange(HALF)[None, :]
    band = np.where(q - k >= 0, _rel_bucket_np(q - k), -1)
    corner = _rel_bucket_np(HALF + q - k)
    return np.stack([band, corner]).astype(np.int32)


def _bias_kernel(tab_ref, idx_ref, o_ref):
    assert HALF == REL_MAX_DIST
    idx = idx_ref[...]
    zero = jnp.zeros((HALF, HALF), F32)
    lo, hi = slice(0, HALF), slice(HALF, BLK)
    for h in range(o_ref.shape[0]):
        far = tab_ref[REL_BUCKETS - 1, h]
        acc = jnp.zeros(idx.shape, F32)
        for b in range(REL_BUCKETS - 1):
            acc = jnp.where(idx == b, (tab_ref[b, h] - far) * LOG2E, acc)
        band = jnp.where(idx[0] < 0, NEG_INF, acc[0])
        corner = acc[1]
        o_ref[h, 0, lo, lo] = band
        o_ref[h, 0, lo, hi] = corner
        o_ref[h, 0, hi, lo] = jnp.full((HALF, HALF), NEG_INF, F32)
        o_ref[h, 0, hi, hi] = band
        o_ref[h, 1, lo, lo] = zero
        o_ref[h, 1, lo, hi] = zero
        o_ref[h, 1, hi, lo] = corner
        o_ref[h, 1, hi, hi] = zero


def _bias_tiles(table):
    n_heads = table.shape[1]
    idx = jnp.asarray(_bucket_tiles())
    return pl.pallas_call(
        _bias_kernel,
        in_specs=[pl.BlockSpec(memory_space=pltpu.SMEM),
                  pl.BlockSpec(memory_space=pltpu.VMEM)],
        out_specs=pl.BlockSpec(memory_space=pltpu.VMEM),
        out_shape=jax.ShapeDtypeStruct((n_heads, 2, BLK, BLK), F32),
        compiler_params=pltpu.CompilerParams(vmem_limit_bytes=VMEM_LIMIT_BYTES),
        name="rel_bias_tiles",
    )(table, idx)


def _inproj_kernel(x_ref, g_ref, w_ref, o_ref, *, q_chunks):
    hb = _rms(x_ref[...], g_ref[...]).astype(BF16)
    for j in range(w_ref.shape[1] // PROJ_CHUNK):
        cols = slice(j * PROJ_CHUNK, (j + 1) * PROJ_CHUNK)
        acc = _wdot(hb, w_ref[:, cols])
        if j in q_chunks:
            acc = acc * (QK_SCALE * LOG2E)
        o_ref[:, cols] = acc.astype(BF16)


def _in_projection(x2d, g, w_bf16):
    t, d = x2d.shape
    n = w_bf16.shape[1]
    assert t % TM_PROJ == 0 and n % PROJ_CHUNK == 0
    q_chunks = (0, (3 * A_WIDTH) // PROJ_CHUNK)
    return pl.pallas_call(
        functools.partial(_inproj_kernel, q_chunks=q_chunks),
        grid=(t // TM_PROJ,),
        in_specs=[pl.BlockSpec((TM_PROJ, d), lambda i: (i, 0)),
                  _resident((1, d)),
                  _resident((d, n))],
        out_specs=pl.BlockSpec((TM_PROJ, n), lambda i: (i, 0)),
        out_shape=jax.ShapeDtypeStruct((t, n), BF16),
        compiler_params=_params(1),
        name="in_projection",
    )(x2d, g, w_bf16)


def _half_mask(shape, half):
    lane = lax.broadcasted_iota(jnp.int32, shape, 1)
    return lax.shift_right_logical(lane, 6) == half


def _block_rows(n):
    return slice(n * BLK, (n + 1) * BLK)


class _Stream(NamedTuple):
    j: int
    slot: int
    q_diag: Callable
    q_past: Callable
    k_tile: Callable
    v_tile: Callable
    bias_tile: Callable
    done: Callable


def _two_pass_attention(streams, s_ref, n_diag_first):
    lead = 2
    col_max = [None] * len(streams)
    results = [None] * len(streams)

    def blocks(i):
        return list(range(streams[i].j, -1, -1))

    def score_step(i, n):
        st = streams[i]
        s = _dot_nt(st.k_tile(n), st.q_diag() if n == st.j else st.q_past())
        if n >= st.j - 1:
            s = s + st.bias_tile(st.j - n)
        s_ref[st.slot, n] = s
        mx = jnp.max(s.reshape(BLK // SUBLANES, SUBLANES, BLK), axis=0)
        mx = mx if col_max[i] is None else jnp.maximum(col_max[i], mx)
        if n == 0:
            mx = jnp.max(mx, axis=0, keepdims=True)
        col_max[i] = mx

    def value_step(i, n):
        st = streams[i]
        p = jnp.exp2(s_ref[st.slot, n] - col_max[i]).astype(BF16)
        pv = _dot(st.v_tile(n), p)
        results[i] = pv if results[i] is None else results[i] + pv

    early = max(lead, n_diag_first)
    for i in range(early):
        score_step(i, streams[i].j)
    for i in range(lead):
        for n in blocks(i)[1:]:
            score_step(i, n)
    for i in range(len(streams)):
        a = i + lead
        ahead = [] if a >= len(streams) else blocks(a)[1:] if a < early else blocks(a)
        mine = blocks(i)
        for k in range(max(len(mine), len(ahead))):
            if k < len(mine):
                value_step(i, mine[k])
            if k < len(ahead):
                score_step(a, ahead[k])
        streams[i].done(results[i])


def _over_denominator(num, den):
    r = num.shape[0] // SUBLANES
    return (num.reshape(r, SUBLANES, BLK) / den[None]).reshape(num.shape)


def _times_row(x, row8):
    r = x.shape[0] // SUBLANES
    return (x.reshape(r, SUBLANES, BLK) * row8[None]).reshape(x.shape)


ONES_ROWS = 16
MOBA_ONES_ROWS = 64


def _moba_streams(j, r, q_ref, bias_ref, g_ref, o_ref, kmt_ref, kaug_ref, vaug_ref, qaug_ref, topk):
    n_pairs = A_WIDTH // LANES
    rows = _block_rows(r)

    def own_block_query(h):
        qp = q_ref[rows, (h // 2) * LANES:(h // 2 + 1) * LANES]
        return jnp.where(_half_mask(qp.shape, h % 2), qp, jnp.zeros_like(qp))

    if j > topk:
        kmt = kmt_ref[...]
        kmt_hi = kmt.astype(BF16)
        kmt_lo = (kmt - kmt_hi.astype(F32)).astype(BF16)
        q_all = q_ref[rows, :]
        gate = _dot_nt(kmt_hi, q_all) + _dot_nt(kmt_lo, q_all)
        slabs = [gate[n * A_HEADS:(n + 1) * A_HEADS, :] for n in range(j)]
        sel_rows = []
        for n in range(j):
            rank = jnp.zeros(slabs[n].shape, F32)
            for m in range(j):
                if m != n:
                    beats = (slabs[m] >= slabs[n]) if m < n else (slabs[m] > slabs[n])
                    rank = rank + jnp.where(beats, 1.0, 0.0)
            sel_rows.append(jnp.where(rank < topk, 0.0, NEG_INF))
        sel_rows.append(jnp.zeros(((8 - j) * A_HEADS, BLK), F32))
        sel_bias = jnp.concatenate(sel_rows + sel_rows, axis=0).T.astype(BF16)
        base = r * A_HEADS
        for p in range(n_pairs):
            qp = q_ref[rows, p * LANES:(p + 1) * LANES]
            for half in range(2):
                qaug_ref[base + 2 * p + half] = jnp.where(_half_mask(qp.shape, half), qp, sel_bias)
        past_query = lambda h: qaug_ref[base + h]
    else:
        past_query = own_block_query

    heads = []

    def head_done(h, result):
        heads.append(_over_denominator(result[:A_HEAD_DIM], result[A_HEAD_DIM:A_HEAD_DIM + SUBLANES]))
        if len(heads) == A_HEADS:
            o = jnp.concatenate(heads, axis=0)
            inv = lax.rsqrt(jnp.mean(o * o, axis=0, keepdims=True) + EPS)
            g = g_ref[...]
            y = o * inv * jnp.concatenate([g, g], axis=1)
            o_ref[rows, :] = y.T.astype(BF16)

    return [_Stream(j=j, slot=h,
                    q_diag=functools.partial(own_block_query, h),
                    q_past=functools.partial(past_query, h),
                    k_tile=lambda n, h=h: kaug_ref[h, _block_rows(n), :],
                    v_tile=lambda n, h=h: vaug_ref[h, :, _block_rows(n)],
                    bias_tile=lambda which, h=h: bias_ref[h, which],
                    done=functools.partial(head_done, h))
            for h in range(A_HEADS)]


def _moba_kernel(q_ref, k_ref, v_ref, bias_ref, g_ref, o_ref,
                 kmt_ref, kaug_ref, vaug_ref, qaug_ref, s_ref, *, topk, per_step):
    t = pl.program_id(1)
    seq = k_ref.shape[0]
    nb = seq // BLK
    n_pairs = A_WIDTH // LANES

    def tiles(t_static):
        streams = []
        for r in range(per_step):
            streams += _moba_streams(per_step * t_static + r, r, q_ref, bias_ref, g_ref, o_ref,
                                     kmt_ref, kaug_ref, vaug_ref, qaug_ref, topk)
        _two_pass_attention(streams, s_ref, n_diag_first=A_HEADS if per_step * t_static > topk else 0)

    @pl.when(t == 0)
    def _first_tiles_of_batch():
        row = lax.broadcasted_iota(jnp.int32, (A_HEADS, A_WIDTH), 0)
        lane = lax.broadcasted_iota(jnp.int32, (A_HEADS, A_WIDTH), 1)
        head_mask = lax.shift_right_logical(lane, 6) == row
        kmt_ref[...] = jnp.zeros(kmt_ref.shape, F32)
        for n in range(nb):
            kb = k_ref[n * BLK:(n + 1) * BLK, :].astype(F32)
            km = jnp.sum(kb, axis=0, keepdims=True) * (1.0 / BLK)
            kmt_ref[n * A_HEADS:(n + 1) * A_HEADS, :] = jnp.where(head_mask, km, 0.0)
        lane = lax.broadcasted_iota(jnp.int32, (1, LANES), 1)
        for p in range(n_pairs):
            cols = slice(p * LANES, (p + 1) * LANES)
            vt = v_ref[:, cols].astype(F32).T
            for half in range(2):
                h = 2 * p + half
                vaug_ref[h, :A_HEAD_DIM, :] = vt[half * A_HEAD_DIM:(half + 1) * A_HEAD_DIM].astype(BF16)
                vaug_ref[h, A_HEAD_DIM:, :] = jnp.ones((MOBA_ONES_ROWS, seq), BF16)
                own = lax.shift_right_logical(lane, 6) == half
                own_one = jnp.where(own, 1.0, 0.0).astype(BF16)
                for n in range(nb):
                    rows = slice(n * BLK, (n + 1) * BLK)
                    code = jnp.where((~own) & ((lane & 63) == n * A_HEADS + h), 1.0, 0.0).astype(BF16)
                    kaug_ref[h, rows, :] = k_ref[rows, cols] * own_one + code
        tiles(0)

    for t_static in range(1, nb // per_step):
        pl.when(t == t_static)(functools.partial(tiles, t_static))


def _moba_attention(proj, bias, g, batch, seq):
    nq = seq // BLK
    per_step = min(TILES_PER_STEP, nq)
    assert seq % (per_step * BLK) == 0 and nq <= 8
    topk = min(MOBA_TOPK, nq)
    steps = nq // per_step
    return pl.pallas_call(
        functools.partial(_moba_kernel, topk=topk, per_step=per_step),
        grid=(batch, steps),
        in_specs=[pl.BlockSpec((per_step * BLK, A_WIDTH), lambda b, t: (b * steps + t, 0)),
                  pl.BlockSpec((seq, A_WIDTH), lambda b, t: (b, 1)),
                  pl.BlockSpec((seq, A_WIDTH), lambda b, t: (b, 2)),
                  pl.BlockSpec((A_HEADS, 2, BLK, BLK), lambda b, t: (0, 0, 0, 0),
                               pipeline_mode=pl.Buffered(1)),
                  _resident(g.shape)],
        out_specs=pl.BlockSpec((per_step * BLK, A_WIDTH), lambda b, t: (b * steps + t, 0)),
        out_shape=jax.ShapeDtypeStruct((batch * seq, A_WIDTH), BF16),
        scratch_shapes=[pltpu.VMEM((8 * A_HEADS, A_WIDTH), F32),
                        pltpu.VMEM((A_HEADS, seq, LANES), BF16),
                        pltpu.VMEM((A_HEADS, A_HEAD_DIM + MOBA_ONES_ROWS, seq), BF16),
                        pltpu.VMEM((per_step * A_HEADS, BLK, LANES), BF16),
                        pltpu.VMEM((A_HEADS, nq, BLK, BLK), F32)],
        compiler_params=_params(2),
        name="moba_attention",
    )(proj, proj, proj, bias, g)


def _diff_streams(j, r, q_ref, k_ref, bias_ref, lam8, g_ref, o_ref, vaug_ref):
    rows = _block_rows(r)

    def map_query(s):
        qp = q_ref[rows, (s // 2) * LANES:(s // 2 + 1) * LANES]
        return jnp.where(_half_mask(qp.shape, s % 2), qp, jnp.zeros_like(qp))

    acc = [None] * (2 * B_HEADS)

    def map_done(s, result):
        acc[s] = result
        if s % 2 == 1:
            h = s // 2
            a0, a1 = acc[s - 1], acc[s]
            n0, l0 = a0[:B_V_DIM], a0[B_V_DIM:B_V_DIM + SUBLANES]
            n1, l1 = a1[:B_V_DIM], a1[B_V_DIM:B_V_DIM + SUBLANES]
            o = _over_denominator(_times_row(n0, l1) - _times_row(n1, lam8 * l0), l0 * l1)
            inv = lax.rsqrt(jnp.mean(o * o, axis=0, keepdims=True) + EPS)
            g = g_ref[...]
            y = o * inv * jnp.concatenate([g, g], axis=1) * (1.0 - LAMBDA_INIT)
            o_ref[rows, h * B_V_DIM:(h + 1) * B_V_DIM] = y.T.astype(BF16)

    return [_Stream(j=j, slot=s,
                    q_diag=functools.partial(map_query, s),
                    q_past=functools.partial(map_query, s),
                    k_tile=lambda n, s=s: k_ref[_block_rows(n), (s // 2) * LANES:(s // 2 + 1) * LANES],
                    v_tile=lambda n, s=s: vaug_ref[s // 2, :, _block_rows(n)],
                    bias_tile=lambda which, s=s: bias_ref[s // 2, which],
                    done=functools.partial(map_done, s))
            for s in range(2 * B_HEADS)]


def _diff_kernel(q_ref, k_ref, v_ref, bias_ref, lam_ref, g_ref, o_ref, vaug_ref, s_ref, *, per_step):
    t = pl.program_id(1)
    seq = k_ref.shape[0]

    def tiles(t_static):
        lp = lam_ref[...]
        lam = (jnp.exp(jnp.sum(lp[0:1] * lp[1:2], axis=1, keepdims=True))
               - jnp.exp(jnp.sum(lp[2:3] * lp[3:4], axis=1, keepdims=True)) + LAMBDA_INIT)
        lam8 = jnp.broadcast_to(lam, (SUBLANES, BLK))
        streams = []
        for r in range(per_step):
            streams += _diff_streams(per_step * t_static + r, r, q_ref, k_ref, bias_ref, lam8, g_ref,
                                     o_ref, vaug_ref)
        _two_pass_attention(streams, s_ref, n_diag_first=0)

    @pl.when(t == 0)
    def _first_tiles_of_batch():
        for h in range(B_HEADS):
            vt = v_ref[:, h * B_V_DIM:(h + 1) * B_V_DIM].astype(F32).T
            vaug_ref[h, :B_V_DIM, :] = vt.astype(BF16)
            vaug_ref[h, B_V_DIM:, :] = jnp.ones((ONES_ROWS, seq), BF16)
        tiles(0)

    for t_static in range(1, seq // (per_step * BLK)):
        pl.when(t == t_static)(functools.partial(tiles, t_static))


def _diff_attention(proj, bias, lam, g, batch, seq):
    nq = seq // BLK
    per_step = min(TILES_PER_STEP, nq)
    assert seq % (per_step * BLK) == 0 and B_V_DIM == LANES
    steps = nq // per_step
    first = (3 * A_WIDTH) // B_WIDTH
    return pl.pallas_call(
        functools.partial(_diff_kernel, per_step=per_step),
        grid=(batch, steps),
        in_specs=[pl.BlockSpec((per_step * BLK, B_WIDTH), lambda b, t: (b * steps + t, first)),
                  pl.BlockSpec((seq, B_WIDTH), lambda b, t: (b, first + 1)),
                  pl.BlockSpec((seq, B_WIDTH), lambda b, t: (b, first + 2)),
                  pl.BlockSpec((B_HEADS, 2, BLK, BLK), lambda b, t: (A_HEADS // B_HEADS, 0, 0, 0),
                               pipeline_mode=pl.Buffered(1)),
                  _resident(lam.shape),
                  _resident(g.shape)],
        out_specs=pl.BlockSpec((per_step * BLK, B_WIDTH), lambda b, t: (b * steps + t, 0)),
        out_shape=jax.ShapeDtypeStruct((batch * seq, B_WIDTH), BF16),
        scratch_shapes=[pltpu.VMEM((B_HEADS, B_V_DIM + ONES_ROWS, seq), BF16),
                        pltpu.VMEM((2 * B_HEADS, nq, BLK, BLK), F32)],
        compiler_params=_params(2),
        name="diff_attention",
    )(proj, proj, proj, bias, lam, g)


def _cross_kernel(x_ref, oa_ref, ob_ref, wo_ref, g_ref, wq_ref, mem_ref, gm_ref, wk_ref, wv_ref, wco_ref,
                  o_ref, kc_ref, vc_ref, *, scale, per_seq):
    @pl.when(pl.program_id(0) % per_seq == 0)
    def _memory_keys_values():
        m = _rms(mem_ref[...], gm_ref[...]).astype(BF16)
        kc_ref[...] = _wdot(m, wk_ref[...]).astype(BF16)
        vc_ref[...] = _wdot(m, wv_ref[...]).astype(BF16)

    x1 = (x_ref[...] + _wdot(oa_ref[...], wo_ref[:A_WIDTH, :]) + _wdot(ob_ref[...], wo_ref[A_WIDTH:, :]))
    hb = _rms(x1, g_ref[...]).astype(BF16)
    q = (_wdot(hb, wq_ref[...]) * scale).astype(BF16)
    hd = q.shape[1] // MEM_HEADS
    heads = []
    for h in range(MEM_HEADS):
        cols = slice(h * hd, (h + 1) * hd)
        s = _dot_nt(q[:, cols], kc_ref[:, cols])
        p = jnp.exp2(s - jnp.max(s, axis=1, keepdims=True))
        l = jnp.sum(p, axis=1, keepdims=True)
        heads.append((_dot(p.astype(BF16), vc_ref[:, cols]) / l).astype(BF16))
    o = jnp.concatenate(heads, axis=1)
    o_ref[...] = x1 + _wdot(o, wco_ref[...])


def _outproj_cross(x2d, oa, ob, wo, g, wq, mem2d, gm, wk, wv, wco, seq, mem_len):
    t, d = x2d.shape
    per_seq = seq // TM_CROSS
    assert seq % TM_CROSS == 0
    n = wk.shape[1]
    hd = wq.shape[1] // MEM_HEADS
    scale = hd ** -0.5 * LOG2E
    return pl.pallas_call(
        functools.partial(_cross_kernel, scale=scale, per_seq=per_seq),
        grid=(t // TM_CROSS,),
        in_specs=[pl.BlockSpec((TM_CROSS, d), lambda i: (i, 0)),
                  pl.BlockSpec((TM_CROSS, A_WIDTH), lambda i: (i, 0)),
                  pl.BlockSpec((TM_CROSS, B_WIDTH), lambda i: (i, 0)),
                  _resident(wo.shape), _resident((1, d)), _resident(wq.shape),
                  pl.BlockSpec((mem_len, d), lambda i: (i // per_seq, 0)),
                  _resident((1, d)), _resident(wk.shape), _resident(wv.shape),
                  _resident(wco.shape)],
        out_specs=pl.BlockSpec((TM_CROSS, d), lambda i: (i, 0)),
        out_shape=jax.ShapeDtypeStruct((t, d), F32),
        scratch_shapes=[pltpu.VMEM((mem_len, n), BF16),
                        pltpu.VMEM((mem_len, n), BF16)],
        compiler_params=_params(1),
        name="outproj_cross_attention",
    )(x2d, oa, ob, wo, g, wq, mem2d, gm, wk, wv, wco)


def _ffn_kernel(x_ref, g_ref, wg_ref, wu_ref, wd_ref, gf_ref, o_ref):
    x = x_ref[...]
    hb = _rms(x, g_ref[...]).astype(BF16)
    o_ref[...] = x
    for c in range(wg_ref.shape[1] // FF_CHUNK):
        cols = slice(c * FF_CHUNK, (c + 1) * FF_CHUNK)
        a = (jax.nn.silu(_wdot(hb, wg_ref[:, cols])) * _wdot(hb, wu_ref[:, cols])).astype(BF16)
        o_ref[...] += _wdot(a, wd_ref[cols, :])
    o_ref[...] = _rms(o_ref[...], gf_ref[...])


def _swiglu_final(x2d, g, wg, wu, wd, gf):
    t, d = x2d.shape
    assert t % TM_FFN == 0 and wg.shape[1] % FF_CHUNK == 0
    return pl.pallas_call(
        _ffn_kernel,
        grid=(t // TM_FFN,),
        in_specs=[pl.BlockSpec((TM_FFN, d), lambda i: (i, 0)),
                  _resident((1, d)), _resident(wg.shape), _resident(wu.shape), _resident(wd.shape),
                  _resident((1, d))],
        out_specs=pl.BlockSpec((TM_FFN, d), lambda i: (i, 0)),
        out_shape=jax.ShapeDtypeStruct((t, d), F32),
        compiler_params=_params(1),
        name="swiglu_final_norm",
    )(x2d, g, wg, wu, wd, gf)


def kernel(x, mem, mix_norm_g, w_in, moba_out_g, diff_lambda, diff_subln_g, w_out, rel_bias_table,
           cross_norm_g, mem_norm_g, w_cq, w_ck, w_cv, w_co, ffn_norm_g, w_gate, w_up, w_down,
           final_norm_g):
    batch, seq, d = x.shape
    mem_len = mem.shape[1]
    assert mix_norm_g.shape[0] == 1, "single-layer trunk"
    x2d = x.reshape(batch * seq, d)
    mem2d = mem.reshape(batch * mem_len, d)
    row = lambda v: v.reshape(1, -1).astype(F32)
    wb = lambda w: w[0].astype(F32)

    bias = _bias_tiles(rel_bias_table.astype(F32))
    proj = _in_projection(x2d, row(mix_norm_g[0]), wb(w_in))
    col = lambda v: jnp.broadcast_to(v.reshape(-1, 1).astype(F32), (v.size, LANES))
    oa = _moba_attention(proj, bias, col(moba_out_g[0]), batch, seq)
    ob = _diff_attention(proj, bias, diff_lambda[0].astype(F32), col(diff_subln_g[0]), batch, seq)
    x2 = _outproj_cross(x2d, oa, ob, wb(w_out), row(cross_norm_g[0]), wb(w_cq),
                        mem2d, row(mem_norm_g[0]), wb(w_ck), wb(w_cv), wb(w_co), seq, mem_len)
    out = _swiglu_final(x2, row(ffn_norm_g[0]), wb(w_gate), wb(w_up), wb(w_down), row(final_norm_g))
    return out.reshape(batch, seq, d)
```

```python
import functools
import math
from typing import Callable, NamedTuple

import numpy as np
import jax
import jax.numpy as jnp
from jax import lax
from jax.experimental import pallas as pl
from jax.experimental.pallas import tpu as pltpu

F32 = jnp.float32
BF16 = jnp.bfloat16

A_HEADS = 8
A_HEAD_DIM = 64
A_WIDTH = A_HEADS * A_HEAD_DIM
MOBA_BLOCK = 256
MOBA_TOPK = 3
B_HEADS = 4
B_QK_DIM = 64
B_V_DIM = 2 * B_QK_DIM
B_WIDTH = B_HEADS * B_V_DIM
MEM_HEADS = 4
REL_BUCKETS = 32
REL_MAX_DIST = 128
EPS = 1e-6
NEG_INF = -1e30
LAMBDA_INIT = 0.8 - 0.6 * math.exp(-0.3 * 0)
QK_SCALE = A_HEAD_DIM ** -0.5
LOG2E = math.log2(math.e)

LANES = 128
SUBLANES = 8
VMEM_LIMIT_BYTES = 56 * 1024 * 1024

BLK = MOBA_BLOCK
TILES_PER_STEP = 4
TM_PROJ = 1024
TM_FFN = 1024
TM_CROSS = 1024
FF_CHUNK = 256
PROJ_CHUNK = 512


def _dot(a, b):
    return jnp.dot(a, b, preferred_element_type=F32)


def _wdot(a, w):
    return jnp.dot(a, w.astype(BF16), preferred_element_type=F32)


def _dot_nt(a, b):
    return lax.dot_general(a, b, (((1,), (1,)), ((), ())), preferred_element_type=F32)


def _rms(x, g):
    return x * lax.rsqrt(jnp.mean(x * x, axis=-1, keepdims=True) + EPS) * g


def _params(n_axes):
    return pltpu.CompilerParams(dimension_semantics=("arbitrary",) * n_axes,
                                vmem_limit_bytes=VMEM_LIMIT_BYTES)


def _resident(shape):
    return pl.BlockSpec(shape, lambda *_: (0,) * len(shape), pipeline_mode=pl.Buffered(1))


def _rel_bucket_np(dist):
    n = np.maximum(dist, 0)
    max_exact = REL_BUCKETS // 2
    ratio = np.maximum(n, max_exact).astype(np.float32) / np.float32(max_exact)
    log_ratio = np.log(ratio) / np.float32(math.log(REL_MAX_DIST / max_exact))
    large = max_exact + (log_ratio * np.float32(REL_BUCKETS - max_exact)).astype(np.int32)
    large = np.minimum(large, REL_BUCKETS - 1)
    return np.where(n < max_exact, n, large).astype(np.int32)


HALF = BLK // 2


def _bucket_tiles():
    k = np.arange(HALF)[:, None]
    q = np.arange(HALF)[None, :]
    band = np.where(q - k >= 0, _rel_bucket_np(q - k), -1)
    corner = _rel_bucket_np(HALF + q - k)
    return np.stack([band, corner]).astype(np.int32)


def _bias_kernel(tab_ref, idx_ref, o_ref):
    assert HALF == REL_MAX_DIST
    idx = idx_ref[...]
    zero = jnp.zeros((HALF, HALF), F32)
    lo, hi = slice(0, HALF), slice(HALF, BLK)
    for h in range(o_ref.shape[0]):
        far = tab_ref[REL_BUCKETS - 1, h]
        acc = jnp.zeros(idx.shape, F32)
        for b in range(REL_BUCKETS - 1):
            acc = jnp.where(idx == b, (tab_ref[b, h] - far) * LOG2E, acc)
        band = jnp.where(idx[0] < 0, NEG_INF, acc[0])
        corner = acc[1]
        o_ref[h, 0, lo, lo] = band
        o_ref[h, 0, lo, hi] = corner
        o_ref[h, 0, hi, lo] = jnp.full((HALF, HALF), NEG_INF, F32)
        o_ref[h, 0, hi, hi] = band
        o_ref[h, 1, lo, lo] = zero
        o_ref[h, 1, lo, hi] = zero
        o_ref[h, 1, hi, lo] = corner
        o_ref[h, 1, hi, hi] = zero


def _bias_tiles(table):
    n_heads = table.shape[1]
    idx = jnp.asarray(_bucket_tiles())
    return pl.pallas_call(
        _bias_kernel,
        in_specs=[pl.BlockSpec(memory_space=pltpu.SMEM),
                  pl.BlockSpec(memory_space=pltpu.VMEM)],
        out_specs=pl.BlockSpec(memory_space=pltpu.VMEM),
        out_shape=jax.ShapeDtypeStruct((n_heads, 2, BLK, BLK), F32),
        compiler_params=pltpu.CompilerParams(vmem_limit_bytes=VMEM_LIMIT_BYTES),
        name="rel_bias_tiles",
    )(table, idx)


def _inproj_kernel(x_ref, g_ref, w_ref, o_ref, *, q_chunks):
    hb = _rms(x_ref[...], g_ref[...]).astype(BF16)
    for j in range(w_ref.shape[1] // PROJ_CHUNK):
        cols = slice(j * PROJ_CHUNK, (j + 1) * PROJ_CHUNK)
        acc = _wdot(hb, w_ref[:, cols])
        if j in q_chunks:
            acc = acc * (QK_SCALE * LOG2E)
        o_ref[:, cols] = acc.astype(BF16)


def _in_projection(x2d, g, w_bf16):
    t, d = x2d.shape
    n = w_bf16.shape[1]
    assert t % TM_PROJ == 0 and n % PROJ_CHUNK == 0
    q_chunks = (0, (3 * A_WIDTH) // PROJ_CHUNK)
    return pl.pallas_call(
        functools.partial(_inproj_kernel, q_chunks=q_chunks),
        grid=(t // TM_PROJ,),
        in_specs=[pl.BlockSpec((TM_PROJ, d), lambda i: (i, 0)),
                  _resident((1, d)),
                  _resident((d, n))],
        out_specs=pl.BlockSpec((TM_PROJ, n), lambda i: (i, 0)),
        out_shape=jax.ShapeDtypeStruct((t, n), BF16),
        compiler_params=_params(1),
        name="in_projection",
    )(x2d, g, w_bf16)


def _half_mask(shape, half):
    lane = lax.broadcasted_iota(jnp.int32, shape, 1)
    return lax.shift_right_logical(lane, 6) == half


def _block_rows(n):
    return slice(n * BLK, (n + 1) * BLK)


class _Stream(NamedTuple):
    j: int
    slot: int
    q_diag: Callable
    q_past: Callable
    k_tile: Callable
    v_tile: Callable
    bias_tile: Callable
    done: Callable


def _two_pass_attention(streams, s_ref, n_diag_first):
    lead = 2
    col_max = [None] * len(streams)
    results = [None] * len(streams)

    def blocks(i):
        return list(range(streams[i].j, -1, -1))

    def score_step(i, n):
        st = streams[i]
        s = _dot_nt(st.k_tile(n), st.q_diag() if n == st.j else st.q_past())
        if n >= st.j - 1:
            s = s + st.bias_tile(st.j - n)
        s_ref[st.slot, n] = s
        mx = jnp.max(s.reshape(BLK // SUBLANES, SUBLANES, BLK), axis=0)
        mx = mx if col_max[i] is None else jnp.maximum(col_max[i], mx)
        if n == 0:
            mx = jnp.max(mx, axis=0, keepdims=True)
        col_max[i] = mx

    def value_step(i, n):
        st = streams[i]
        p = jnp.exp2(s_ref[st.slot, n] - col_max[i]).astype(BF16)
        pv = _dot(st.v_tile(n), p)
        results[i] = pv if results[i] is None else results[i] + pv

    early = max(lead, n_diag_first)
    for i in range(early):
        score_step(i, streams[i].j)
    for i in range(lead):
        for n in blocks(i)[1:]:
            score_step(i, n)
    for i in range(len(streams)):
        a = i + lead
        ahead = [] if a >= len(streams) else blocks(a)[1:] if a < early else blocks(a)
        mine = blocks(i)
        for k in range(max(len(mine), len(ahead))):
            if k < len(mine):
                value_step(i, mine[k])
            if k < len(ahead):
                score_step(a, ahead[k])
        streams[i].done(results[i])


def _over_denominator(num, den):
    r = num.shape[0] // SUBLANES
    return (num.reshape(r, SUBLANES, BLK) / den[None]).reshape(num.shape)


def _times_row(x, row8):
    r = x.shape[0] // SUBLANES
    return (x.reshape(r, SUBLANES, BLK) * row8[None]).reshape(x.shape)


ONES_ROWS = 16
MOBA_ONES_ROWS = 64


def _moba_streams(j, r, q_ref, bias_ref, g_ref, o_ref, kmt_ref, kaug_ref, vaug_ref, qaug_ref, topk):
    n_pairs = A_WIDTH // LANES
    rows = _block_rows(r)

    def own_block_query(h):
        qp = q_ref[rows, (h // 2) * LANES:(h // 2 + 1) * LANES]
        return jnp.where(_half_mask(qp.shape, h % 2), qp, jnp.zeros_like(qp))

    if j > topk:
        kmt = kmt_ref[...]
        kmt_hi = kmt.astype(BF16)
        kmt_lo = (kmt - kmt_hi.astype(F32)).astype(BF16)
        q_all = q_ref[rows, :]
        gate = _dot_nt(kmt_hi, q_all) + _dot_nt(kmt_lo, q_all)
        slabs = [gate[n * A_HEADS:(n + 1) * A_HEADS, :] for n in range(j)]
        sel_rows = []
        for n in range(j):
            rank = jnp.zeros(slabs[n].shape, F32)
            for m in range(j):
                if m != n:
                    beats = (slabs[m] >= slabs[n]) if m < n else (slabs[m] > slabs[n])
                    rank = rank + jnp.where(beats, 1.0, 0.0)
            sel_rows.append(jnp.where(rank < topk, 0.0, NEG_INF))
        sel_rows.append(jnp.zeros(((8 - j) * A_HEADS, BLK), F32))
        sel_bias = jnp.concatenate(sel_rows + sel_rows, axis=0).T.astype(BF16)
        base = r * A_HEADS
        for p in range(n_pairs):
            qp = q_ref[rows, p * LANES:(p + 1) * LANES]
            for half in range(2):
                qaug_ref[base + 2 * p + half] = jnp.where(_half_mask(qp.shape, half), qp, sel_bias)
        past_query = lambda h: qaug_ref[base + h]
    else:
        past_query = own_block_query

    heads = []

    def head_done(h, result):
        heads.append(_over_denominator(result[:A_HEAD_DIM], result[A_HEAD_DIM:A_HEAD_DIM + SUBLANES]))
        if len(heads) == A_HEADS:
            o = jnp.concatenate(heads, axis=0)
            inv = lax.rsqrt(jnp.mean(o * o, axis=0, keepdims=True) + EPS)
            g = g_ref[...]
            y = o * inv * jnp.concatenate([g, g], axis=1)
            o_ref[rows, :] = y.T.astype(BF16)

    return [_Stream(j=j, slot=h,
                    q_diag=functools.partial(own_block_query, h),
                    q_past=functools.partial(past_query, h),
                    k_tile=lambda n, h=h: kaug_ref[h, _block_rows(n), :],
                    v_tile=lambda n, h=h: vaug_ref[h, :, _block_rows(n)],
                    bias_tile=lambda which, h=h: bias_ref[h, which],
                    done=functools.partial(head_done, h))
            for h in range(A_HEADS)]


def _moba_kernel(q_ref, k_ref, v_ref, bias_ref, g_ref, o_ref,
                 kmt_ref, kaug_ref, vaug_ref, qaug_ref, s_ref, *, topk, per_step):
    t = pl.program_id(1)
    seq = k_ref.shape[0]
    nb = seq // BLK
    n_pairs = A_WIDTH // LANES

    def tiles(t_static):
        streams = []
        for r in range(per_step):
            streams += _moba_streams(per_step * t_static + r, r, q_ref, bias_ref, g_ref, o_ref,
                                     kmt_ref, kaug_ref, vaug_ref, qaug_ref, topk)
        _two_pass_attention(streams, s_ref, n_diag_first=A_HEADS if per_step * t_static > topk else 0)

    @pl.when(t == 0)
    def _first_tiles_of_batch():
        row = lax.broadcasted_iota(jnp.int32, (A_HEADS, A_WIDTH), 0)
        lane = lax.broadcasted_iota(jnp.int32, (A_HEADS, A_WIDTH), 1)
        head_mask = lax.shift_right_logical(lane, 6) == row
        kmt_ref[...] = jnp.zeros(kmt_ref.shape, F32)
        for n in range(nb):
            kb = k_ref[n * BLK:(n + 1) * BLK, :].astype(F32)
            km = jnp.sum(kb, axis=0, keepdims=True) * (1.0 / BLK)
            kmt_ref[n * A_HEADS:(n + 1) * A_HEADS, :] = jnp.where(head_mask, km, 0.0)
        lane = lax.broadcasted_iota(jnp.int32, (1, LANES), 1)
        for p in range(n_pairs):
            cols = slice(p * LANES, (p + 1) * LANES)
            vt = v_ref[:, cols].astype(F32).T
            for half in range(2):
                h = 2 * p + half
                vaug_ref[h, :A_HEAD_DIM, :] = vt[half * A_HEAD_DIM:(half + 1) * A_HEAD_DIM].astype(BF16)
                vaug_ref[h, A_HEAD_DIM:, :] = jnp.ones((MOBA_ONES_ROWS, seq), BF16)
                own = lax.shift_right_logical(lane, 6) == half
                own_one = jnp.where(own, 1.0, 0.0).astype(BF16)
                for n in range(nb):
                    rows = slice(n * BLK, (n + 1) * BLK)
                    code = jnp.where((~own) & ((lane & 63) == n * A_HEADS + h), 1.0, 0.0).astype(BF16)
                    kaug_ref[h, rows, :] = k_ref[rows, cols] * own_one + code
        tiles(0)

    for t_static in range(1, nb // per_step):
        pl.when(t == t_static)(functools.partial(tiles, t_static))


def _moba_attention(proj, bias, g, batch, seq):
    nq = seq // BLK
    per_step = min(TILES_PER_STEP, nq)
    assert seq % (per_step * BLK) == 0 and nq <= 8
    topk = min(MOBA_TOPK, nq)
    steps = nq // per_step
    return pl.pallas_call(
        functools.partial(_moba_kernel, topk=topk, per_step=per_step),
        grid=(batch, steps),
        in_specs=[pl.BlockSpec((per_step * BLK, A_WIDTH), lambda b, t: (b * steps + t, 0)),
                  pl.BlockSpec((seq, A_WIDTH), lambda b, t: (b, 1)),
                  pl.BlockSpec((seq, A_WIDTH), lambda b, t: (b, 2)),
                  pl.BlockSpec((A_HEADS, 2, BLK, BLK), lambda b, t: (0, 0, 0, 0),
                               pipeline_mode=pl.Buffered(1)),
                  _resident(g.shape)],
        out_specs=pl.BlockSpec((per_step * BLK, A_WIDTH), lambda b, t: (b * steps + t, 0)),
        out_shape=jax.ShapeDtypeStruct((batch * seq, A_WIDTH), BF16),
        scratch_shapes=[pltpu.VMEM((8 * A_HEADS, A_WIDTH), F32),
                        pltpu.VMEM((A_HEADS, seq, LANES), BF16),
                        pltpu.VMEM((A_HEADS, A_HEAD_DIM + MOBA_ONES_ROWS, seq), BF16),
                        pltpu.VMEM((per_step * A_HEADS, BLK, LANES), BF16),
                        pltpu.VMEM((A_HEADS, nq, BLK, BLK), F32)],
        compiler_params=_params(2),
        name="moba_attention",
    )(proj, proj, proj, bias, g)


def _diff_streams(j, r, q_ref, k_ref, bias_ref, lam8, g_ref, o_ref, vaug_ref):
    rows = _block_rows(r)

    def map_query(s):
        qp = q_ref[rows, (s // 2) * LANES:(s // 2 + 1) * LANES]
        return jnp.where(_half_mask(qp.shape, s % 2), qp, jnp.zeros_like(qp))

    acc = [None] * (2 * B_HEADS)

    def map_done(s, result):
        acc[s] = result
        if s % 2 == 1:
            h = s // 2
            a0, a1 = acc[s - 1], acc[s]
            n0, l0 = a0[:B_V_DIM], a0[B_V_DIM:B_V_DIM + SUBLANES]
            n1, l1 = a1[:B_V_DIM], a1[B_V_DIM:B_V_DIM + SUBLANES]
            o = _over_denominator(_times_row(n0, l1) - _times_row(n1, lam8 * l0), l0 * l1)
            inv = lax.rsqrt(jnp.mean(o * o, axis=0, keepdims=True) + EPS)
            g = g_ref[...]
            y = o * inv * jnp.concatenate([g, g], axis=1) * (1.0 - LAMBDA_INIT)
            o_ref[rows, h * B_V_DIM:(h + 1) * B_V_DIM] = y.T.astype(BF16)

    return [_Stream(j=j, slot=s,
                    q_diag=functools.partial(map_query, s),
                    q_past=functools.partial(map_query, s),
                    k_tile=lambda n, s=s: k_ref[_block_rows(n), (s // 2) * LANES:(s // 2 + 1) * LANES],
                    v_tile=lambda n, s=s: vaug_ref[s // 2, :, _block_rows(n)],
                    bias_tile=lambda which, s=s: bias_ref[s // 2, which],
                    done=functools.partial(map_done, s))
            for s in range(2 * B_HEADS)]


def _diff_kernel(q_ref, k_ref, v_ref, bias_ref, lam_ref, g_ref, o_ref, vaug_ref, s_ref, *, per_step):
    t = pl.program_id(1)
    seq = k_ref.shape[0]

    def tiles(t_static):
        lp = lam_ref[...]
        lam = (jnp.exp(jnp.sum(lp[0:1] * lp[1:2], axis=1, keepdims=True))
               - jnp.exp(jnp.sum(lp[2:3] * lp[3:4], axis=1, keepdims=True)) + LAMBDA_INIT)
        lam8 = jnp.broadcast_to(lam, (SUBLANES, BLK))
        streams = []
        for r in range(per_step):
            streams += _diff_streams(per_step * t_static + r, r, q_ref, k_ref, bias_ref, lam8, g_ref,
                                     o_ref, vaug_ref)
        _two_pass_attention(streams, s_ref, n_diag_first=0)

    @pl.when(t == 0)
    def _first_tiles_of_batch():
        for h in range(B_HEADS):
            vt = v_ref[:, h * B_V_DIM:(h + 1) * B_V_DIM].astype(F32).T
            vaug_ref[h, :B_V_DIM, :] = vt.astype(BF16)
            vaug_ref[h, B_V_DIM:, :] = jnp.ones((ONES_ROWS, seq), BF16)
        tiles(0)

    for t_static in range(1, seq // (per_step * BLK)):
        pl.when(t == t_static)(functools.partial(tiles, t_static))


def _diff_attention(proj, bias, lam, g, batch, seq):
    nq = seq // BLK
    per_step = min(TILES_PER_STEP, nq)
    assert seq % (per_step * BLK) == 0 and B_V_DIM == LANES
    steps = nq // per_step
    first = (3 * A_WIDTH) // B_WIDTH
    return pl.pallas_call(
        functools.partial(_diff_kernel, per_step=per_step),
        grid=(batch, steps),
        in_specs=[pl.BlockSpec((per_step * BLK, B_WIDTH), lambda b, t: (b * steps + t, first)),
                  pl.BlockSpec((seq, B_WIDTH), lambda b, t: (b, first + 1)),
                  pl.BlockSpec((seq, B_WIDTH), lambda b, t: (b, first + 2)),
                  pl.BlockSpec((B_HEADS, 2, BLK, BLK), lambda b, t: (A_HEADS // B_HEADS, 0, 0, 0),
                               pipeline_mode=pl.Buffered(1)),
                  _resident(lam.shape),
                  _resident(g.shape)],
        out_specs=pl.BlockSpec((per_step * BLK, B_WIDTH), lambda b, t: (b * steps + t, 0)),
        out_shape=jax.ShapeDtypeStruct((batch * seq, B_WIDTH), BF16),
        scratch_shapes=[pltpu.VMEM((B_HEADS, B_V_DIM + ONES_ROWS, seq), BF16),
                        pltpu.VMEM((2 * B_HEADS, nq, BLK, BLK), F32)],
        compiler_params=_params(2),
        name="diff_attention",
    )(proj, proj, proj, bias, lam, g)


def _cross_kernel(x_ref, oa_ref, ob_ref, wo_ref, g_ref, wq_ref, mem_ref, gm_ref, wk_ref, wv_ref, wco_ref,
                  o_ref, kc_ref, vc_ref, *, scale, per_seq):
    @pl.when(pl.program_id(0) % per_seq == 0)
    def _memory_keys_values():
        m = _rms(mem_ref[...], gm_ref[...]).astype(BF16)
        kc_ref[...] = _wdot(m, wk_ref[...]).astype(BF16)
        vc_ref[...] = _wdot(m, wv_ref[...]).astype(BF16)

    x1 = (x_ref[...] + _wdot(oa_ref[...], wo_ref[:A_WIDTH, :]) + _wdot(ob_ref[...], wo_ref[A_WIDTH:, :]))
    hb = _rms(x1, g_ref[...]).astype(BF16)
    q = (_wdot(hb, wq_ref[...]) * scale).astype(BF16)
    hd = q.shape[1] // MEM_HEADS
    heads = []
    for h in range(MEM_HEADS):
        cols = slice(h * hd, (h + 1) * hd)
        s = _dot_nt(q[:, cols], kc_ref[:, cols])
        p = jnp.exp2(s - jnp.max(s, axis=1, keepdims=True))
        l = jnp.sum(p, axis=1, keepdims=True)
        heads.append((_dot(p.astype(BF16), vc_ref[:, cols]) / l).astype(BF16))
    o = jnp.concatenate(heads, axis=1)
    o_ref[...] = x1 + _wdot(o, wco_ref[...])


def _outproj_cross(x2d, oa, ob, wo, g, wq, mem2d, gm, wk, wv, wco, seq, mem_len):
    t, d = x2d.shape
    per_seq = seq // TM_CROSS
    assert seq % TM_CROSS == 0
    n = wk.shape[1]
    hd = wq.shape[1] // MEM_HEADS
    scale = hd ** -0.5 * LOG2E
    return pl.pallas_call(
        functools.partial(_cross_kernel, scale=scale, per_seq=per_seq),
        grid=(t // TM_CROSS,),
        in_specs=[pl.BlockSpec((TM_CROSS, d), lambda i: (i, 0)),
                  pl.BlockSpec((TM_CROSS, A_WIDTH), lambda i: (i, 0)),
                  pl.BlockSpec((TM_CROSS, B_WIDTH), lambda i: (i, 0)),
                  _resident(wo.shape), _resident((1, d)), _resident(wq.shape),
                  pl.BlockSpec((mem_len, d), lambda i: (i // per_seq, 0)),
                  _resident((1, d)), _resident(wk.shape), _resident(wv.shape),
                  _resident(wco.shape)],
        out_specs=pl.BlockSpec((TM_CROSS, d), lambda i: (i, 0)),
        out_shape=jax.ShapeDtypeStruct((t, d), F32),
        scratch_shapes=[pltpu.VMEM((mem_len, n), BF16),
                        pltpu.VMEM((mem_len, n), BF16)],
        compiler_params=_params(1),
        name="outproj_cross_attention",
    )(x2d, oa, ob, wo, g, wq, mem2d, gm, wk, wv, wco)


def _ffn_kernel(x_ref, g_ref, wg_ref, wu_ref, wd_ref, gf_ref, o_ref):
    x = x_ref[...]
    hb = _rms(x, g_ref[...]).astype(BF16)
    o_ref[...] = x
    for c in range(wg_ref.shape[1] // FF_CHUNK):
        cols = slice(c * FF_CHUNK, (c + 1) * FF_CHUNK)
        a = (jax.nn.silu(_wdot(hb, wg_ref[:, cols])) * _wdot(hb, wu_ref[:, cols])).astype(BF16)
        o_ref[...] += _wdot(a, wd_ref[cols, :])
    o_ref[...] = _rms(o_ref[...], gf_ref[...])


def _swiglu_final(x2d, g, wg, wu, wd, gf):
    t, d = x2d.shape
    assert t % TM_FFN == 0 and wg.shape[1] % FF_CHUNK == 0
    return pl.pallas_call(
        _ffn_kernel,
        grid=(t // TM_FFN,),
        in_specs=[pl.BlockSpec((TM_FFN, d), lambda i: (i, 0)),
                  _resident((1, d)), _resident(wg.shape), _resident(wu.shape), _resident(wd.shape),
                  _resident((1, d))],
        out_specs=pl.BlockSpec((TM_FFN, d), lambda i: (i, 0)),
        out_shape=jax.ShapeDtypeStruct((t, d), F32),
        compiler_params=_params(1),
        name="swiglu_final_norm",
    )(x2d, g, wg, wu, wd, gf)


def kernel(x, mem, mix_norm_g, w_in, moba_out_g, diff_lambda, diff_subln_g, w_out, rel_bias_table,
           cross_norm_g, mem_norm_g, w_cq, w_ck, w_cv, w_co, ffn_norm_g, w_gate, w_up, w_down,
           final_norm_g):
    batch, seq, d = x.shape
    mem_len = mem.shape[1]
    assert mix_norm_g.shape[0] == 1, "single-layer trunk"
    x2d = x.reshape(batch * seq, d)
    mem2d = mem.reshape(batch * mem_len, d)
    row = lambda v: v.reshape(1, -1).astype(F32)
    wb = lambda w: w[0].astype(F32)

    bias = _bias_tiles(rel_bias_table.astype(F32))
    proj = _in_projection(x2d, row(mix_norm_g[0]), wb(w_in))
    col = lambda v: jnp.broadcast_to(v.reshape(-1, 1).astype(F32), (v.size, LANES))
    oa = _moba_attention(proj, bias, col(moba_out_g[0]), batch, seq)
    ob = _diff_attention(proj, bias, diff_lambda[0].astype(F32), col(diff_subln_g[0]), batch, seq)
    x2 = _outproj_cross(x2d, oa, ob, wb(w_out), row(cross_norm_g[0]), wb(w_cq),
                        mem2d, row(mem_norm_g[0]), wb(w_ck), wb(w_cv), wb(w_co), seq, mem_len)
    out = _swiglu_final(x2, row(ffn_norm_g[0]), wb(w_gate), wb(w_up), wb(w_down), row(final_norm_g))
    return out.reshape(batch, seq, d)
```

```python
import functools
import math
from typing import Callable, NamedTuple

import numpy as np
import jax
import jax.numpy as jnp
from jax import lax
from jax.experimental import pallas as pl
from jax.experimental.pallas import tpu as pltpu

F32 = jnp.float32
BF16 = jnp.bfloat16

A_HEADS = 8
A_HEAD_DIM = 64
A_WIDTH = A_HEADS * A_HEAD_DIM
MOBA_BLOCK = 256
MOBA_TOPK = 3
B_HEADS = 4
B_QK_DIM = 64
B_V_DIM = 2 * B_QK_DIM
B_WIDTH = B_HEADS * B_V_DIM
MEM_HEADS = 4
REL_BUCKETS = 32
REL_MAX_DIST = 128
EPS = 1e-6
NEG_INF = -1e30
LAMBDA_INIT = 0.8 - 0.6 * math.exp(-0.3 * 0)
QK_SCALE = A_HEAD_DIM ** -0.5
LOG2E = math.log2(math.e)

LANES = 128
SUBLANES = 8
VMEM_LIMIT_BYTES = 56 * 1024 * 1024

BLK = MOBA_BLOCK
TILES_PER_STEP = 4
TM_PROJ = 1024
TM_FFN = 1024
TM_CROSS = 1024
FF_CHUNK = 256
PROJ_CHUNK = 512


def _dot(a, b):
    return jnp.dot(a, b, preferred_element_type=F32)


def _wdot(a, w):
    return jnp.dot(a, w.astype(BF16), preferred_element_type=F32)


def _dot_nt(a, b):
    return lax.dot_general(a, b, (((1,), (1,)), ((), ())), preferred_element_type=F32)


def _rms(x, g):
    return x * lax.rsqrt(jnp.mean(x * x, axis=-1, keepdims=True) + EPS) * g


def _params(n_axes):
    return pltpu.CompilerParams(dimension_semantics=("arbitrary",) * n_axes,
                                vmem_limit_bytes=VMEM_LIMIT_BYTES)


def _resident(shape):
    return pl.BlockSpec(shape, lambda *_: (0,) * len(shape), pipeline_mode=pl.Buffered(1))


def _rel_bucket_np(dist):
    n = np.maximum(dist, 0)
    max_exact = REL_BUCKETS // 2
    ratio = np.maximum(n, max_exact).astype(np.float32) / np.float32(max_exact)
    log_ratio = np.log(ratio) / np.float32(math.log(REL_MAX_DIST / max_exact))
    large = max_exact + (log_ratio * np.float32(REL_BUCKETS - max_exact)).astype(np.int32)
    large = np.minimum(large, REL_BUCKETS - 1)
    return np.where(n < max_exact, n, large).astype(np.int32)


HALF = BLK // 2


def _bucket_tiles():
    k = np.arange(HALF)[:, None]
    q = np.arange(HALF)[None, :]
    band = np.where(q - k >= 0, _rel_bucket_np(q - k), -1)
    corner = _rel_bucket_np(HALF + q - k)
    return np.stack([band, corner]).astype(np.int32)


def _bias_kernel(tab_ref, idx_ref, o_ref):
    assert HALF == REL_MAX_DIST
    idx = idx_ref[...]
    zero = jnp.zeros((HALF, HALF), F32)
    lo, hi = slice(0, HALF), slice(HALF, BLK)
    for h in range(o_ref.shape[0]):
        far = tab_ref[REL_BUCKETS - 1, h]
        acc = jnp.zeros(idx.shape, F32)
        for b in range(REL_BUCKETS - 1):
            acc = jnp.where(idx == b, (tab_ref[b, h] - far) * LOG2E, acc)
        band = jnp.where(idx[0] < 0, NEG_INF, acc[0])
        corner = acc[1]
        o_ref[h, 0, lo, lo] = band
        o_ref[h, 0, lo, hi] = corner
        o_ref[h, 0, hi, lo] = jnp.full((HALF, HALF), NEG_INF, F32)
        o_ref[h, 0, hi, hi] = band
        o_ref[h, 1, lo, lo] = zero
        o_ref[h, 1, lo, hi] = zero
        o_ref[h, 1, hi, lo] = corner
        o_ref[h, 1, hi, hi] = zero


def _bias_tiles(table):
    n_heads = table.shape[1]
    idx = jnp.asarray(_bucket_tiles())
    return pl.pallas_call(
        _bias_kernel,
        in_specs=[pl.BlockSpec(memory_space=pltpu.SMEM),
                  pl.BlockSpec(memory_space=pltpu.VMEM)],
        out_specs=pl.BlockSpec(memory_space=pltpu.VMEM),
        out_shape=jax.ShapeDtypeStruct((n_heads, 2, BLK, BLK), F32),
        compiler_params=pltpu.CompilerParams(vmem_limit_bytes=VMEM_LIMIT_BYTES),
        name="rel_bias_tiles",
    )(table, idx)


def _inproj_kernel(x_ref, g_ref, w_ref, o_ref, *, q_chunks):
    hb = _rms(x_ref[...], g_ref[...]).astype(BF16)
    for j in range(w_ref.shape[1] // PROJ_CHUNK):
        cols = slice(j * PROJ_CHUNK, (j + 1) * PROJ_CHUNK)
        acc = _wdot(hb, w_ref[:, cols])
        if j in q_chunks:
            acc = acc * (QK_SCALE * LOG2E)
        o_ref[:, cols] = acc.astype(BF16)


def _in_projection(x2d, g, w_bf16):
    t, d = x2d.shape
    n = w_bf16.shape[1]
    assert t % TM_PROJ == 0 and n % PROJ_CHUNK == 0
    q_chunks = (0, (3 * A_WIDTH) // PROJ_CHUNK)
    return pl.pallas_call(
        functools.partial(_inproj_kernel, q_chunks=q_chunks),
        grid=(t // TM_PROJ,),
        in_specs=[pl.BlockSpec((TM_PROJ, d), lambda i: (i, 0)),
                  _resident((1, d)),
                  _resident((d, n))],
        out_specs=pl.BlockSpec((TM_PROJ, n), lambda i: (i, 0)),
        out_shape=jax.ShapeDtypeStruct((t, n), BF16),
        compiler_params=_params(1),
        name="in_projection",
    )(x2d, g, w_bf16)


def _half_mask(shape, half):
    lane = lax.broadcasted_iota(jnp.int32, shape, 1)
    return lax.shift_right_logical(lane, 6) == half


def _block_rows(n):
    return slice(n * BLK, (n + 1) * BLK)


class _Stream(NamedTuple):
    j: int
    slot: int
    q_diag: Callable
    q_past: Callable
    k_tile: Callable
    v_tile: Callable
    bias_tile: Callable
    row_bias: Callable
    done: Callable


def _two_pass_attention(streams, s_ref, n_diag_first):
    lead = 2
    col_max = [None] * len(streams)
    results = [None] * len(streams)

    def blocks(i):
        return list(range(streams[i].j, -1, -1))

    def score_step(i, n):
        st = streams[i]
        s = _dot_nt(st.k_tile(n), st.q_diag() if n == st.j else st.q_past())
        if n >= st.j - 1:
            s = s + st.bias_tile(st.j - n)
        if st.row_bias(n) is not None:
            s = s + st.row_bias(n)
        s_ref[st.slot, n] = s
        mx = jnp.max(s.reshape(BLK // SUBLANES, SUBLANES, BLK), axis=0)
        mx = mx if col_max[i] is None else jnp.maximum(col_max[i], mx)
        if n == 0:
            mx = jnp.max(mx, axis=0, keepdims=True)
        col_max[i] = mx

    def value_step(i, n):
        st = streams[i]
        p = jnp.exp2(s_ref[st.slot, n] - col_max[i]).astype(BF16)
        pv = _dot(st.v_tile(n), p)
        results[i] = pv if results[i] is None else results[i] + pv

    early = max(lead, n_diag_first)
    for i in range(early):
        score_step(i, streams[i].j)
    for i in range(lead):
        for n in blocks(i)[1:]:
            score_step(i, n)
    for i in range(len(streams)):
        a = i + lead
        ahead = [] if a >= len(streams) else blocks(a)[1:] if a < early else blocks(a)
        mine = blocks(i)
        for k in range(max(len(mine), len(ahead))):
            if k < len(mine):
                value_step(i, mine[k])
            if k < len(ahead):
                score_step(a, ahead[k])
        streams[i].done(results[i])


def _over_denominator(num, den):
    r = num.shape[0] // SUBLANES
    return (num.reshape(r, SUBLANES, BLK) / den[None]).reshape(num.shape)


def _times_row(x, row8):
    r = x.shape[0] // SUBLANES
    return (x.reshape(r, SUBLANES, BLK) * row8[None]).reshape(x.shape)


ONES_ROWS = 16
MOBA_ONES_ROWS = 64


def _moba_streams(j, r, q_ref, k_ref, bias_ref, g_ref, o_ref, kmt_ref, vaug_ref, topk):
    rows = _block_rows(r)

    def head_query(h):
        qp = q_ref[rows, (h // 2) * LANES:(h // 2 + 1) * LANES]
        return jnp.where(_half_mask(qp.shape, h % 2), qp, jnp.zeros_like(qp))

    sel_rows = None
    if j > topk:
        kmt = kmt_ref[...]
        kmt_hi = kmt.astype(BF16)
        kmt_lo = (kmt - kmt_hi.astype(F32)).astype(BF16)
        q_all = q_ref[rows, :]
        gate = _dot_nt(kmt_hi, q_all) + _dot_nt(kmt_lo, q_all)
        slabs = [gate[n * A_HEADS:(n + 1) * A_HEADS, :] for n in range(j)]
        sel_rows = []
        for n in range(j):
            rank = jnp.zeros(slabs[n].shape, F32)
            for m in range(j):
                if m != n:
                    beats = (slabs[m] >= slabs[n]) if m < n else (slabs[m] > slabs[n])
                    rank = rank + jnp.where(beats, 1.0, 0.0)
            sel_rows.append(jnp.where(rank < topk, 0.0, NEG_INF))

    def selection_bias(h, n):
        if sel_rows is None or n >= j:
            return None
        return sel_rows[n][h:h + 1, :]

    heads = []

    def head_done(h, result):
        heads.append(_over_denominator(result[:A_HEAD_DIM], result[A_HEAD_DIM:A_HEAD_DIM + SUBLANES]))
        if len(heads) == A_HEADS:
            o = jnp.concatenate(heads, axis=0)
            inv = lax.rsqrt(jnp.mean(o * o, axis=0, keepdims=True) + EPS)
            g = g_ref[...]
            y = o * inv * jnp.concatenate([g, g], axis=1)
            o_ref[rows, :] = y.T.astype(BF16)

    return [_Stream(j=j, slot=h,
                    q_diag=functools.partial(head_query, h),
                    q_past=functools.partial(head_query, h),
                    k_tile=lambda n, h=h: k_ref[_block_rows(n), (h // 2) * LANES:(h // 2 + 1) * LANES],
                    v_tile=lambda n, h=h: vaug_ref[h, :, _block_rows(n)],
                    bias_tile=lambda which, h=h: bias_ref[h, which],
                    row_bias=functools.partial(selection_bias, h),
                    done=functools.partial(head_done, h))
            for h in range(A_HEADS)]


def _moba_kernel(q_ref, k_ref, v_ref, bias_ref, g_ref, o_ref, kmt_ref, vaug_ref, s_ref, *, topk, per_step):
    t = pl.program_id(1)
    seq = k_ref.shape[0]
    nb = seq // BLK
    n_pairs = A_WIDTH // LANES

    def tiles(t_static):
        streams = []
        for r in range(per_step):
            streams += _moba_streams(per_step * t_static + r, r, q_ref, k_ref, bias_ref, g_ref, o_ref,
                                     kmt_ref, vaug_ref, topk)
        _two_pass_attention(streams, s_ref, n_diag_first=A_HEADS if per_step * t_static > topk else 0)

    @pl.when(t == 0)
    def _first_tiles_of_batch():
        row = lax.broadcasted_iota(jnp.int32, (A_HEADS, A_WIDTH), 0)
        lane = lax.broadcasted_iota(jnp.int32, (A_HEADS, A_WIDTH), 1)
        head_mask = lax.shift_right_logical(lane, 6) == row
        kmt_ref[...] = jnp.zeros(kmt_ref.shape, F32)
        for n in range(nb):
            kb = k_ref[n * BLK:(n + 1) * BLK, :].astype(F32)
            km = jnp.sum(kb, axis=0, keepdims=True) * (1.0 / BLK)
            kmt_ref[n * A_HEADS:(n + 1) * A_HEADS, :] = jnp.where(head_mask, km, 0.0)
        for p in range(n_pairs):
            vt = v_ref[:, p * LANES:(p + 1) * LANES].astype(F32).T
            for half in range(2):
                h = 2 * p + half
                vaug_ref[h, :A_HEAD_DIM, :] = vt[half * A_HEAD_DIM:(half + 1) * A_HEAD_DIM].astype(BF16)
                vaug_ref[h, A_HEAD_DIM:, :] = jnp.ones((MOBA_ONES_ROWS, seq), BF16)
        tiles(0)

    for t_static in range(1, nb // per_step):
        pl.when(t == t_static)(functools.partial(tiles, t_static))


def _moba_attention(proj, bias, g, batch, seq):
    nq = seq // BLK
    per_step = min(TILES_PER_STEP, nq)
    assert seq % (per_step * BLK) == 0 and nq <= 8
    topk = min(MOBA_TOPK, nq)
    steps = nq // per_step
    return pl.pallas_call(
        functools.partial(_moba_kernel, topk=topk, per_step=per_step),
        grid=(batch, steps),
        in_specs=[pl.BlockSpec((per_step * BLK, A_WIDTH), lambda b, t: (b * steps + t, 0)),
                  pl.BlockSpec((seq, A_WIDTH), lambda b, t: (b, 1)),
                  pl.BlockSpec((seq, A_WIDTH), lambda b, t: (b, 2)),
                  pl.BlockSpec((A_HEADS, 2, BLK, BLK), lambda b, t: (0, 0, 0, 0),
                               pipeline_mode=pl.Buffered(1)),
                  _resident(g.shape)],
        out_specs=pl.BlockSpec((per_step * BLK, A_WIDTH), lambda b, t: (b * steps + t, 0)),
        out_shape=jax.ShapeDtypeStruct((batch * seq, A_WIDTH), BF16),
        scratch_shapes=[pltpu.VMEM((8 * A_HEADS, A_WIDTH), F32),
                        pltpu.VMEM((A_HEADS, A_HEAD_DIM + MOBA_ONES_ROWS, seq), BF16),
                        pltpu.VMEM((A_HEADS, nq, BLK, BLK), F32)],
        compiler_params=_params(2),
        name="moba_attention",
    )(proj, proj, proj, bias, g)


def _diff_streams(j, r, q_ref, k_ref, bias_ref, lam8, g_ref, o_ref, vaug_ref):
    rows = _block_rows(r)

    def map_query(s):
        qp = q_ref[rows, (s // 2) * LANES:(s // 2 + 1) * LANES]
        return jnp.where(_half_mask(qp.shape, s % 2), qp, jnp.zeros_like(qp))

    acc = [None] * (2 * B_HEADS)

    def map_done(s, result):
        acc[s] = result
        if s % 2 == 1:
            h = s // 2
            a0, a1 = acc[s - 1], acc[s]
            n0, l0 = a0[:B_V_DIM], a0[B_V_DIM:B_V_DIM + SUBLANES]
            n1, l1 = a1[:B_V_DIM], a1[B_V_DIM:B_V_DIM + SUBLANES]
            o = _over_denominator(_times_row(n0, l1) - _times_row(n1, lam8 * l0), l0 * l1)
            inv = lax.rsqrt(jnp.mean(o * o, axis=0, keepdims=True) + EPS)
            g = g_ref[...]
            y = o * inv * jnp.concatenate([g, g], axis=1) * (1.0 - LAMBDA_INIT)
            o_ref[rows, h * B_V_DIM:(h + 1) * B_V_DIM] = y.T.astype(BF16)

    return [_Stream(j=j, slot=s,
                    q_diag=functools.partial(map_query, s),
                    q_past=functools.partial(map_query, s),
                    k_tile=lambda n, s=s: k_ref[_block_rows(n), (s // 2) * LANES:(s // 2 + 1) * LANES],
                    v_tile=lambda n, s=s: vaug_ref[s // 2, :, _block_rows(n)],
                    bias_tile=lambda which, s=s: bias_ref[s // 2, which],
                    row_bias=lambda n: None,
                    done=functools.partial(map_done, s))
            for s in range(2 * B_HEADS)]


def _diff_kernel(q_ref, k_ref, v_ref, bias_ref, lam_ref, g_ref, o_ref, vaug_ref, s_ref, *, per_step):
    t = pl.program_id(1)
    seq = k_ref.shape[0]

    def tiles(t_static):
        lp = lam_ref[...]
        lam = (jnp.exp(jnp.sum(lp[0:1] * lp[1:2], axis=1, keepdims=True))
               - jnp.exp(jnp.sum(lp[2:3] * lp[3:4], axis=1, keepdims=True)) + LAMBDA_INIT)
        lam8 = jnp.broadcast_to(lam, (SUBLANES, BLK))
        streams = []
        for r in range(per_step):
            streams += _diff_streams(per_step * t_static + r, r, q_ref, k_ref, bias_ref, lam8, g_ref,
                                     o_ref, vaug_ref)
        _two_pass_attention(streams, s_ref, n_diag_first=0)

    @pl.when(t == 0)
    def _first_tiles_of_batch():
        for h in range(B_HEADS):
            vt = v_ref[:, h * B_V_DIM:(h + 1) * B_V_DIM].astype(F32).T
            vaug_ref[h, :B_V_DIM, :] = vt.astype(BF16)
            vaug_ref[h, B_V_DIM:, :] = jnp.ones((ONES_ROWS, seq), BF16)
        tiles(0)

    for t_static in range(1, seq // (per_step * BLK)):
        pl.when(t == t_static)(functools.partial(tiles, t_static))


def _diff_attention(proj, bias, lam, g, batch, seq):
    nq = seq // BLK
    per_step = min(TILES_PER_STEP, nq)
    assert seq % (per_step * BLK) == 0 and B_V_DIM == LANES
    steps = nq // per_step
    first = (3 * A_WIDTH) // B_WIDTH
    return pl.pallas_call(
        functools.partial(_diff_kernel, per_step=per_step),
        grid=(batch, steps),
        in_specs=[pl.BlockSpec((per_step * BLK, B_WIDTH), lambda b, t: (b * steps + t, first)),
                  pl.BlockSpec((seq, B_WIDTH), lambda b, t: (b, first + 1)),
                  pl.BlockSpec((seq, B_WIDTH), lambda b, t: (b, first + 2)),
                  pl.BlockSpec((B_HEADS, 2, BLK, BLK), lambda b, t: (A_HEADS // B_HEADS, 0, 0, 0),
                               pipeline_mode=pl.Buffered(1)),
                  _resident(lam.shape),
                  _resident(g.shape)],
        out_specs=pl.BlockSpec((per_step * BLK, B_WIDTH), lambda b, t: (b * steps + t, 0)),
        out_shape=jax.ShapeDtypeStruct((batch * seq, B_WIDTH), BF16),
        scratch_shapes=[pltpu.VMEM((B_HEADS, B_V_DIM + ONES_ROWS, seq), BF16),
                        pltpu.VMEM((2 * B_HEADS, nq, BLK, BLK), F32)],
        compiler_params=_params(2),
        name="diff_attention",
    )(proj, proj, proj, bias, lam, g)


def _cross_kernel(x_ref, oa_ref, ob_ref, wo_ref, g_ref, wq_ref, mem_ref, gm_ref, wk_ref, wv_ref, wco_ref,
                  o_ref, kc_ref, vc_ref, *, scale, per_seq):
    @pl.when(pl.program_id(0) % per_seq == 0)
    def _memory_keys_values():
        m = _rms(mem_ref[...], gm_ref[...]).astype(BF16)
        kc_ref[...] = _wdot(m, wk_ref[...]).astype(BF16)
        vc_ref[...] = _wdot(m, wv_ref[...]).astype(BF16)

    x1 = (x_ref[...] + _wdot(oa_ref[...], wo_ref[:A_WIDTH, :]) + _wdot(ob_ref[...], wo_ref[A_WIDTH:, :]))
    hb = _rms(x1, g_ref[...]).astype(BF16)
    q = (_wdot(hb, wq_ref[...]) * scale).astype(BF16)
    hd = q.shape[1] // MEM_HEADS
    heads = []
    for h in range(MEM_HEADS):
        cols = slice(h * hd, (h + 1) * hd)
        s = _dot_nt(q[:, cols], kc_ref[:, cols])
        p = jnp.exp2(s - jnp.max(s, axis=1, keepdims=True))
        l = jnp.sum(p, axis=1, keepdims=True)
        heads.append((_dot(p.astype(BF16), vc_ref[:, cols]) / l).astype(BF16))
    o = jnp.concatenate(heads, axis=1)
    o_ref[...] = x1 + _wdot(o, wco_ref[...])


def _outproj_cross(x2d, oa, ob, wo, g, wq, mem2d, gm, wk, wv, wco, seq, mem_len):
    t, d = x2d.shape
    per_seq = seq // TM_CROSS
    assert seq % TM_CROSS == 0
    n = wk.shape[1]
    hd = wq.shape[1] // MEM_HEADS
    scale = hd ** -0.5 * LOG2E
    return pl.pallas_call(
        functools.partial(_cross_kernel, scale=scale, per_seq=per_seq),
        grid=(t // TM_CROSS,),
        in_specs=[pl.BlockSpec((TM_CROSS, d), lambda i: (i, 0)),
                  pl.BlockSpec((TM_CROSS, A_WIDTH), lambda i: (i, 0)),
                  pl.BlockSpec((TM_CROSS, B_WIDTH), lambda i: (i, 0)),
                  _resident(wo.shape), _resident((1, d)), _resident(wq.shape),
                  pl.BlockSpec((mem_len, d), lambda i: (i // per_seq, 0)),
                  _resident((1, d)), _resident(wk.shape), _resident(wv.shape),
                  _resident(wco.shape)],
        out_specs=pl.BlockSpec((TM_CROSS, d), lambda i: (i, 0)),
        out_shape=jax.ShapeDtypeStruct((t, d), F32),
        scratch_shapes=[pltpu.VMEM((mem_len, n), BF16),
                        pltpu.VMEM((mem_len, n), BF16)],
        compiler_params=_params(1),
        name="outproj_cross_attention",
    )(x2d, oa, ob, wo, g, wq, mem2d, gm, wk, wv, wco)


def _ffn_kernel(x_ref, g_ref, wg_ref, wu_ref, wd_ref, gf_ref, o_ref):
    x = x_ref[...]
    hb = _rms(x, g_ref[...]).astype(BF16)
    o_ref[...] = x
    for c in range(wg_ref.shape[1] // FF_CHUNK):
        cols = slice(c * FF_CHUNK, (c + 1) * FF_CHUNK)
        a = (jax.nn.silu(_wdot(hb, wg_ref[:, cols])) * _wdot(hb, wu_ref[:, cols])).astype(BF16)
        o_ref[...] += _wdot(a, wd_ref[cols, :])
    o_ref[...] = _rms(o_ref[...], gf_ref[...])


def _swiglu_final(x2d, g, wg, wu, wd, gf):
    t, d = x2d.shape
    assert t % TM_FFN == 0 and wg.shape[1] % FF_CHUNK == 0
    return pl.pallas_call(
        _ffn_kernel,
        grid=(t // TM_FFN,),
        in_specs=[pl.BlockSpec((TM_FFN, d), lambda i: (i, 0)),
                  _resident((1, d)), _resident(wg.shape), _resident(wu.shape), _resident(wd.shape),
                  _resident((1, d))],
        out_specs=pl.BlockSpec((TM_FFN, d), lambda i: (i, 0)),
        out_shape=jax.ShapeDtypeStruct((t, d), F32),
        compiler_params=_params(1),
        name="swiglu_final_norm",
    )(x2d, g, wg, wu, wd, gf)


def kernel(x, mem, mix_norm_g, w_in, moba_out_g, diff_lambda, diff_subln_g, w_out, rel_bias_table,
           cross_norm_g, mem_norm_g, w_cq, w_ck, w_cv, w_co, ffn_norm_g, w_gate, w_up, w_down,
           final_norm_g):
    batch, seq, d = x.shape
    mem_len = mem.shape[1]
    assert mix_norm_g.shape[0] == 1, "single-layer trunk"
    x2d = x.reshape(batch * seq, d)
    mem2d = mem.reshape(batch * mem_len, d)
    row = lambda v: v.reshape(1, -1).astype(F32)
    wb = lambda w: w[0].astype(F32)

    bias = _bias_tiles(rel_bias_table.astype(F32))
    proj = _in_projection(x2d, row(mix_norm_g[0]), wb(w_in))
    col = lambda v: jnp.broadcast_to(v.reshape(-1, 1).astype(F32), (v.size, LANES))
    oa = _moba_attention(proj, bias, col(moba_out_g[0]), batch, seq)
    ob = _diff_attention(proj, bias, diff_lambda[0].astype(F32), col(diff_subln_g[0]), batch, seq)
    x2 = _outproj_cross(x2d, oa, ob, wb(w_out), row(cross_norm_g[0]), wb(w_cq),
                        mem2d, row(mem_norm_g[0]), wb(w_ck), wb(w_cv), wb(w_co), seq, mem_len)
    out = _swiglu_final(x2, row(ffn_norm_g[0]), wb(w_gate), wb(w_up), wb(w_down), row(final_norm_g))
    return out.reshape(batch, seq, d)
```

```python
import functools
import math
from typing import Callable, NamedTuple

import numpy as np
import jax
import jax.numpy as jnp
from jax import lax
from jax.experimental import pallas as pl
from jax.experimental.pallas import tpu as pltpu

F32 = jnp.float32
BF16 = jnp.bfloat16

A_HEADS = 8
A_HEAD_DIM = 64
A_WIDTH = A_HEADS * A_HEAD_DIM
MOBA_BLOCK = 256
MOBA_TOPK = 3
B_HEADS = 4
B_QK_DIM = 64
B_V_DIM = 2 * B_QK_DIM
B_WIDTH = B_HEADS * B_V_DIM
MEM_HEADS = 4
REL_BUCKETS = 32
REL_MAX_DIST = 128
EPS = 1e-6
NEG_INF = -1e30
LAMBDA_INIT = 0.8 - 0.6 * math.exp(-0.3 * 0)
QK_SCALE = A_HEAD_DIM ** -0.5
LOG2E = math.log2(math.e)

LANES = 128
SUBLANES = 8
VMEM_LIMIT_BYTES = 56 * 1024 * 1024

BLK = MOBA_BLOCK
TILES_PER_STEP = 4
TM_PROJ = 1024
TM_FFN = 1024
TM_CROSS = 1024
FF_CHUNK = 256
PROJ_CHUNK = 512


def _dot(a, b):
    return jnp.dot(a, b, preferred_element_type=F32)


def _wdot(a, w):
    return jnp.dot(a, w.astype(BF16), preferred_element_type=F32)


def _dot_nt(a, b):
    return lax.dot_general(a, b, (((1,), (1,)), ((), ())), preferred_element_type=F32)


def _rms(x, g):
    return x * lax.rsqrt(jnp.mean(x * x, axis=-1, keepdims=True) + EPS) * g


def _params(n_axes):
    return pltpu.CompilerParams(dimension_semantics=("arbitrary",) * n_axes,
                                vmem_limit_bytes=VMEM_LIMIT_BYTES)


def _resident(shape):
    return pl.BlockSpec(shape, lambda *_: (0,) * len(shape), pipeline_mode=pl.Buffered(1))


def _rel_bucket_np(dist):
    n = np.maximum(dist, 0)
    max_exact = REL_BUCKETS // 2
    ratio = np.maximum(n, max_exact).astype(np.float32) / np.float32(max_exact)
    log_ratio = np.log(ratio) / np.float32(math.log(REL_MAX_DIST / max_exact))
    large = max_exact + (log_ratio * np.float32(REL_BUCKETS - max_exact)).astype(np.int32)
    large = np.minimum(large, REL_BUCKETS - 1)
    return np.where(n < max_exact, n, large).astype(np.int32)


HALF = BLK // 2


def _bucket_tiles():
    k = np.arange(HALF)[:, None]
    q = np.arange(HALF)[None, :]
    band = np.where(q - k >= 0, _rel_bucket_np(q - k), -1)
    corner = _rel_bucket_np(HALF + q - k)
    return np.stack([band, corner]).astype(np.int32)


def _fill_bias_tiles(tab_ref, idx_ref, o_ref, first_head):
    assert HALF == REL_MAX_DIST
    idx = idx_ref[...]
    zero = jnp.zeros((HALF, HALF), F32)
    lo, hi = slice(0, HALF), slice(HALF, BLK)
    for h in range(o_ref.shape[0]):
        far = tab_ref[REL_BUCKETS - 1, first_head + h]
        acc = jnp.zeros(idx.shape, F32)
        for b in range(REL_BUCKETS - 1):
            acc = jnp.where(idx == b, (tab_ref[b, first_head + h] - far) * LOG2E, acc)
        band = jnp.where(idx[0] < 0, NEG_INF, acc[0])
        corner = acc[1]
        o_ref[h, 0, lo, lo] = band
        o_ref[h, 0, lo, hi] = corner
        o_ref[h, 0, hi, lo] = jnp.full((HALF, HALF), NEG_INF, F32)
        o_ref[h, 0, hi, hi] = band
        o_ref[h, 1, lo, lo] = zero
        o_ref[h, 1, lo, hi] = zero
        o_ref[h, 1, hi, lo] = corner
        o_ref[h, 1, hi, hi] = zero


def _bias_inputs(table):
    idx = jnp.asarray(_bucket_tiles())
    return (table.astype(F32), idx), [pl.BlockSpec(memory_space=pltpu.SMEM), _resident(idx.shape)]


def _inproj_kernel(x_ref, g_ref, w_ref, o_ref, *, q_chunks):
    hb = _rms(x_ref[...], g_ref[...]).astype(BF16)
    for j in range(w_ref.shape[1] // PROJ_CHUNK):
        cols = slice(j * PROJ_CHUNK, (j + 1) * PROJ_CHUNK)
        acc = _wdot(hb, w_ref[:, cols])
        if j in q_chunks:
            acc = acc * (QK_SCALE * LOG2E)
        o_ref[:, cols] = acc.astype(BF16)


def _in_projection(x2d, g, w_bf16):
    t, d = x2d.shape
    n = w_bf16.shape[1]
    assert t % TM_PROJ == 0 and n % PROJ_CHUNK == 0
    q_chunks = (0, (3 * A_WIDTH) // PROJ_CHUNK)
    return pl.pallas_call(
        functools.partial(_inproj_kernel, q_chunks=q_chunks),
        grid=(t // TM_PROJ,),
        in_specs=[pl.BlockSpec((TM_PROJ, d), lambda i: (i, 0)),
                  _resident((1, d)),
                  _resident((d, n))],
        out_specs=pl.BlockSpec((TM_PROJ, n), lambda i: (i, 0)),
        out_shape=jax.ShapeDtypeStruct((t, n), BF16),
        compiler_params=_params(1),
        name="in_projection",
    )(x2d, g, w_bf16)


def _half_mask(shape, half):
    lane = lax.broadcasted_iota(jnp.int32, shape, 1)
    return lax.shift_right_logical(lane, 6) == half


def _block_rows(n):
    return slice(n * BLK, (n + 1) * BLK)


class _Stream(NamedTuple):
    j: int
    slot: int
    q_diag: Callable
    q_past: Callable
    k_tile: Callable
    v_tile: Callable
    bias_tile: Callable
    row_bias: Callable
    done: Callable


def _two_pass_attention(streams, s_ref, n_diag_first):
    lead = 2
    col_max = [None] * len(streams)
    results = [None] * len(streams)

    def blocks(i):
        return list(range(streams[i].j, -1, -1))

    def score_step(i, n):
        st = streams[i]
        s = _dot_nt(st.k_tile(n), st.q_diag() if n == st.j else st.q_past())
        if n >= st.j - 1:
            s = s + st.bias_tile(st.j - n)
        if st.row_bias(n) is not None:
            s = s + st.row_bias(n)
        s_ref[st.slot, n] = s
        mx = jnp.max(s.reshape(BLK // SUBLANES, SUBLANES, BLK), axis=0)
        mx = mx if col_max[i] is None else jnp.maximum(col_max[i], mx)
        if n == 0:
            mx = jnp.max(mx, axis=0, keepdims=True)
        col_max[i] = mx

    def value_step(i, n):
        st = streams[i]
        p = jnp.exp2(s_ref[st.slot, n] - col_max[i]).astype(BF16)
        pv = _dot(st.v_tile(n), p)
        results[i] = pv if results[i] is None else results[i] + pv

    early = max(lead, n_diag_first)
    for i in range(early):
        score_step(i, streams[i].j)
    for i in range(lead):
        for n in blocks(i)[1:]:
            score_step(i, n)
    for i in range(len(streams)):
        a = i + lead
        ahead = [] if a >= len(streams) else blocks(a)[1:] if a < early else blocks(a)
        mine = blocks(i)
        for k in range(max(len(mine), len(ahead))):
            if k < len(mine):
                value_step(i, mine[k])
            if k < len(ahead):
                score_step(a, ahead[k])
        streams[i].done(results[i])


def _over_denominator(num, den):
    r = num.shape[0] // SUBLANES
    return (num.reshape(r, SUBLANES, BLK) / den[None]).reshape(num.shape)


def _times_row(x, row8):
    r = x.shape[0] // SUBLANES
    return (x.reshape(r, SUBLANES, BLK) * row8[None]).reshape(x.shape)


ONES_ROWS = 16
MOBA_ONES_ROWS = 64


def _moba_streams(j, r, q_ref, k_ref, bias_ref, g_ref, o_ref, kmt_ref, vaug_ref, topk):
    rows = _block_rows(r)

    def head_query(h):
        qp = q_ref[rows, (h // 2) * LANES:(h // 2 + 1) * LANES]
        return jnp.where(_half_mask(qp.shape, h % 2), qp, jnp.zeros_like(qp))

    sel_rows = None
    if j > topk:
        kmt = kmt_ref[...]
        kmt_hi = kmt.astype(BF16)
        kmt_lo = (kmt - kmt_hi.astype(F32)).astype(BF16)
        q_all = q_ref[rows, :]
        gate = _dot_nt(kmt_hi, q_all) + _dot_nt(kmt_lo, q_all)
        slabs = [gate[n * A_HEADS:(n + 1) * A_HEADS, :] for n in range(j)]
        sel_rows = []
        for n in range(j):
            rank = jnp.zeros(slabs[n].shape, F32)
            for m in range(j):
                if m != n:
                    beats = (slabs[m] >= slabs[n]) if m < n else (slabs[m] > slabs[n])
                    rank = rank + jnp.where(beats, 1.0, 0.0)
            sel_rows.append(jnp.where(rank < topk, 0.0, NEG_INF))

    def selection_bias(h, n):
        if sel_rows is None or n >= j:
            return None
        return sel_rows[n][h:h + 1, :]

    heads = []

    def head_done(h, result):
        heads.append(_over_denominator(result[:A_HEAD_DIM], result[A_HEAD_DIM:A_HEAD_DIM + SUBLANES]))
        if len(heads) == A_HEADS:
            o = jnp.concatenate(heads, axis=0)
            inv = lax.rsqrt(jnp.mean(o * o, axis=0, keepdims=True) + EPS)
            g = g_ref[...]
            y = o * inv * jnp.concatenate([g, g], axis=1)
            o_ref[rows, :] = y.T.astype(BF16)

    return [_Stream(j=j, slot=h,
                    q_diag=functools.partial(head_query, h),
                    q_past=functools.partial(head_query, h),
                    k_tile=lambda n, h=h: k_ref[_block_rows(n), (h // 2) * LANES:(h // 2 + 1) * LANES],
                    v_tile=lambda n, h=h: vaug_ref[h, :, _block_rows(n)],
                    bias_tile=lambda which, h=h: bias_ref[h, which],
                    row_bias=functools.partial(selection_bias, h),
                    done=functools.partial(head_done, h))
            for h in range(A_HEADS)]


def _moba_kernel(q_ref, k_ref, v_ref, tab_ref, idx_ref, g_ref, o_ref, bias_ref, kmt_ref, vaug_ref, s_ref,
                 *, topk, per_step):
    t = pl.program_id(1)

    @pl.when((pl.program_id(0) == 0) & (t == 0))
    def _first_step():
        _fill_bias_tiles(tab_ref, idx_ref, bias_ref, first_head=0)

    seq = k_ref.shape[0]
    nb = seq // BLK
    n_pairs = A_WIDTH // LANES

    def tiles(t_static):
        streams = []
        for r in range(per_step):
            streams += _moba_streams(per_step * t_static + r, r, q_ref, k_ref, bias_ref, g_ref, o_ref,
                                     kmt_ref, vaug_ref, topk)
        _two_pass_attention(streams, s_ref, n_diag_first=A_HEADS if per_step * t_static > topk else 0)

    @pl.when(t == 0)
    def _first_tiles_of_batch():
        row = lax.broadcasted_iota(jnp.int32, (A_HEADS, A_WIDTH), 0)
        lane = lax.broadcasted_iota(jnp.int32, (A_HEADS, A_WIDTH), 1)
        head_mask = lax.shift_right_logical(lane, 6) == row
        kmt_ref[...] = jnp.zeros(kmt_ref.shape, F32)
        ones = jnp.ones((A_HEADS, BLK), BF16)
        for n in range(nb):
            km = _dot(ones, k_ref[n * BLK:(n + 1) * BLK, :]) * (1.0 / BLK)
            kmt_ref[n * A_HEADS:(n + 1) * A_HEADS, :] = jnp.where(head_mask, km, 0.0)
        for p in range(n_pairs):
            vt = v_ref[:, p * LANES:(p + 1) * LANES].T
            for half in range(2):
                h = 2 * p + half
                vaug_ref[h, :A_HEAD_DIM, :] = vt[half * A_HEAD_DIM:(half + 1) * A_HEAD_DIM]
                vaug_ref[h, A_HEAD_DIM:, :] = jnp.ones((MOBA_ONES_ROWS, seq), BF16)
        tiles(0)

    for t_static in range(1, nb // per_step):
        pl.when(t == t_static)(functools.partial(tiles, t_static))


def _moba_attention(proj, table, g, batch, seq):
    nq = seq // BLK
    per_step = min(TILES_PER_STEP, nq)
    assert seq % (per_step * BLK) == 0 and nq <= 8
    topk = min(MOBA_TOPK, nq)
    steps = nq // per_step
    bias_operands, bias_specs = _bias_inputs(table)
    return pl.pallas_call(
        functools.partial(_moba_kernel, topk=topk, per_step=per_step),
        grid=(batch, steps),
        in_specs=[pl.BlockSpec((per_step * BLK, A_WIDTH), lambda b, t: (b * steps + t, 0)),
                  pl.BlockSpec((seq, A_WIDTH), lambda b, t: (b, 1)),
                  pl.BlockSpec((seq, A_WIDTH), lambda b, t: (b, 2)),
                  *bias_specs,
                  _resident(g.shape)],
        out_specs=pl.BlockSpec((per_step * BLK, A_WIDTH), lambda b, t: (b * steps + t, 0)),
        out_shape=jax.ShapeDtypeStruct((batch * seq, A_WIDTH), BF16),
        scratch_shapes=[pltpu.VMEM((A_HEADS, 2, BLK, BLK), F32),
                        pltpu.VMEM((8 * A_HEADS, A_WIDTH), F32),
                        pltpu.VMEM((A_HEADS, A_HEAD_DIM + MOBA_ONES_ROWS, seq), BF16),
                        pltpu.VMEM((A_HEADS, nq, BLK, BLK), F32)],
        compiler_params=_params(2),
        name="moba_attention",
    )(proj, proj, proj, *bias_operands, g)


def _diff_streams(j, r, q_ref, k_ref, bias_ref, lam8, g_ref, o_ref, vaug_ref):
    rows = _block_rows(r)

    def map_query(s):
        qp = q_ref[rows, (s // 2) * LANES:(s // 2 + 1) * LANES]
        return jnp.where(_half_mask(qp.shape, s % 2), qp, jnp.zeros_like(qp))

    acc = [None] * (2 * B_HEADS)

    def map_done(s, result):
        acc[s] = result
        if s % 2 == 1:
            h = s // 2
            a0, a1 = acc[s - 1], acc[s]
            n0, l0 = a0[:B_V_DIM], a0[B_V_DIM:B_V_DIM + SUBLANES]
            n1, l1 = a1[:B_V_DIM], a1[B_V_DIM:B_V_DIM + SUBLANES]
            o = _over_denominator(_times_row(n0, l1) - _times_row(n1, lam8 * l0), l0 * l1)
            inv = lax.rsqrt(jnp.mean(o * o, axis=0, keepdims=True) + EPS)
            g = g_ref[...]
            y = o * inv * jnp.concatenate([g, g], axis=1) * (1.0 - LAMBDA_INIT)
            o_ref[rows, h * B_V_DIM:(h + 1) * B_V_DIM] = y.T.astype(BF16)

    return [_Stream(j=j, slot=s,
                    q_diag=functools.partial(map_query, s),
                    q_past=functools.partial(map_query, s),
                    k_tile=lambda n, s=s: k_ref[_block_rows(n), (s // 2) * LANES:(s // 2 + 1) * LANES],
                    v_tile=lambda n, s=s: vaug_ref[s // 2, :, _block_rows(n)],
                    bias_tile=lambda which, s=s: bias_ref[s // 2, which],
                    row_bias=lambda n: None,
                    done=functools.partial(map_done, s))
            for s in range(2 * B_HEADS)]


def _diff_kernel(q_ref, k_ref, v_ref, tab_ref, idx_ref, lam_ref, g_ref, o_ref, bias_ref, vaug_ref, s_ref,
                 *, per_step):
    t = pl.program_id(1)

    @pl.when((pl.program_id(0) == 0) & (t == 0))
    def _first_step():
        _fill_bias_tiles(tab_ref, idx_ref, bias_ref, first_head=A_HEADS)

    seq = k_ref.shape[0]

    def tiles(t_static):
        lp = lam_ref[...]
        lam = (jnp.exp(jnp.sum(lp[0:1] * lp[1:2], axis=1, keepdims=True))
               - jnp.exp(jnp.sum(lp[2:3] * lp[3:4], axis=1, keepdims=True)) + LAMBDA_INIT)
        lam8 = jnp.broadcast_to(lam, (SUBLANES, BLK))
        streams = []
        for r in range(per_step):
            streams += _diff_streams(per_step * t_static + r, r, q_ref, k_ref, bias_ref, lam8, g_ref,
                                     o_ref, vaug_ref)
        _two_pass_attention(streams, s_ref, n_diag_first=0)

    @pl.when(t == 0)
    def _first_tiles_of_batch():
        for h in range(B_HEADS):
            vaug_ref[h, :B_V_DIM, :] = v_ref[:, h * B_V_DIM:(h + 1) * B_V_DIM].T
            vaug_ref[h, B_V_DIM:, :] = jnp.ones((ONES_ROWS, seq), BF16)
        tiles(0)

    for t_static in range(1, seq // (per_step * BLK)):
        pl.when(t == t_static)(functools.partial(tiles, t_static))


def _diff_attention(proj, table, lam, g, batch, seq):
    nq = seq // BLK
    per_step = min(TILES_PER_STEP, nq)
    assert seq % (per_step * BLK) == 0 and B_V_DIM == LANES
    steps = nq // per_step
    first = (3 * A_WIDTH) // B_WIDTH
    bias_operands, bias_specs = _bias_inputs(table)
    return pl.pallas_call(
        functools.partial(_diff_kernel, per_step=per_step),
        grid=(batch, steps),
        in_specs=[pl.BlockSpec((per_step * BLK, B_WIDTH), lambda b, t: (b * steps + t, first)),
                  pl.BlockSpec((seq, B_WIDTH), lambda b, t: (b, first + 1)),
                  pl.BlockSpec((seq, B_WIDTH), lambda b, t: (b, first + 2)),
                  *bias_specs,
                  _resident(lam.shape),
                  _resident(g.shape)],
        out_specs=pl.BlockSpec((per_step * BLK, B_WIDTH), lambda b, t: (b * steps + t, 0)),
        out_shape=jax.ShapeDtypeStruct((batch * seq, B_WIDTH), BF16),
        scratch_shapes=[pltpu.VMEM((B_HEADS, 2, BLK, BLK), F32),
                        pltpu.VMEM((B_HEADS, B_V_DIM + ONES_ROWS, seq), BF16),
                        pltpu.VMEM((2 * B_HEADS, nq, BLK, BLK), F32)],
        compiler_params=_params(2),
        name="diff_attention",
    )(proj, proj, proj, *bias_operands, lam, g)


def _cross_kernel(x_ref, oa_ref, ob_ref, wo_ref, g_ref, wq_ref, mem_ref, gm_ref, wk_ref, wv_ref, wco_ref,
                  o_ref, kc_ref, vc_ref, *, scale, per_seq):
    @pl.when(pl.program_id(0) % per_seq == 0)
    def _memory_keys_values():
        m = _rms(mem_ref[...], gm_ref[...]).astype(BF16)
        kc_ref[...] = _wdot(m, wk_ref[...]).astype(BF16)
        vc_ref[...] = _wdot(m, wv_ref[...]).astype(BF16)

    x1 = (x_ref[...] + _wdot(oa_ref[...], wo_ref[:A_WIDTH, :]) + _wdot(ob_ref[...], wo_ref[A_WIDTH:, :]))
    hb = _rms(x1, g_ref[...]).astype(BF16)
    q = (_wdot(hb, wq_ref[...]) * scale).astype(BF16)
    hd = q.shape[1] // MEM_HEADS
    heads = []
    for h in range(MEM_HEADS):
        cols = slice(h * hd, (h + 1) * hd)
        s = _dot_nt(q[:, cols], kc_ref[:, cols])
        p = jnp.exp2(s - jnp.max(s, axis=1, keepdims=True))
        l = jnp.sum(p, axis=1, keepdims=True)
        heads.append((_dot(p.astype(BF16), vc_ref[:, cols]) / l).astype(BF16))
    o = jnp.concatenate(heads, axis=1)
    o_ref[...] = x1 + _wdot(o, wco_ref[...])


def _outproj_cross(x2d, oa, ob, wo, g, wq, mem2d, gm, wk, wv, wco, seq, mem_len):
    t, d = x2d.shape
    per_seq = seq // TM_CROSS
    assert seq % TM_CROSS == 0
    n = wk.shape[1]
    hd = wq.shape[1] // MEM_HEADS
    scale = hd ** -0.5 * LOG2E
    return pl.pallas_call(
        functools.partial(_cross_kernel, scale=scale, per_seq=per_seq),
        grid=(t // TM_CROSS,),
        in_specs=[pl.BlockSpec((TM_CROSS, d), lambda i: (i, 0)),
                  pl.BlockSpec((TM_CROSS, A_WIDTH), lambda i: (i, 0)),
                  pl.BlockSpec((TM_CROSS, B_WIDTH), lambda i: (i, 0)),
                  _resident(wo.shape), _resident((1, d)), _resident(wq.shape),
                  pl.BlockSpec((mem_len, d), lambda i: (i // per_seq, 0)),
                  _resident((1, d)), _resident(wk.shape), _resident(wv.shape),
                  _resident(wco.shape)],
        out_specs=pl.BlockSpec((TM_CROSS, d), lambda i: (i, 0)),
        out_shape=jax.ShapeDtypeStruct((t, d), F32),
        scratch_shapes=[pltpu.VMEM((mem_len, n), BF16),
                        pltpu.VMEM((mem_len, n), BF16)],
        compiler_params=_params(1),
        name="outproj_cross_attention",
    )(x2d, oa, ob, wo, g, wq, mem2d, gm, wk, wv, wco)


def _ffn_kernel(x_ref, g_ref, wg_ref, wu_ref, wd_ref, gf_ref, o_ref):
    x = x_ref[...]
    hb = _rms(x, g_ref[...]).astype(BF16)
    o_ref[...] = x
    for c in range(wg_ref.shape[1] // FF_CHUNK):
        cols = slice(c * FF_CHUNK, (c + 1) * FF_CHUNK)
        a = (jax.nn.silu(_wdot(hb, wg_ref[:, cols])) * _wdot(hb, wu_ref[:, cols])).astype(BF16)
        o_ref[...] += _wdot(a, wd_ref[cols, :])
    o_ref[...] = _rms(o_ref[...], gf_ref[...])


def _swiglu_final(x2d, g, wg, wu, wd, gf):
    t, d = x2d.shape
    assert t % TM_FFN == 0 and wg.shape[1] % FF_CHUNK == 0
    return pl.pallas_call(
        _ffn_kernel,
        grid=(t // TM_FFN,),
        in_specs=[pl.BlockSpec((TM_FFN, d), lambda i: (i, 0)),
                  _resident((1, d)), _resident(wg.shape), _resident(wu.shape), _resident(wd.shape),
                  _resident((1, d))],
        out_specs=pl.BlockSpec((TM_FFN, d), lambda i: (i, 0)),
        out_shape=jax.ShapeDtypeStruct((t, d), F32),
        compiler_params=_params(1),
        name="swiglu_final_norm",
    )(x2d, g, wg, wu, wd, gf)


def kernel(x, mem, mix_norm_g, w_in, moba_out_g, diff_lambda, diff_subln_g, w_out, rel_bias_table,
           cross_norm_g, mem_norm_g, w_cq, w_ck, w_cv, w_co, ffn_norm_g, w_gate, w_up, w_down,
           final_norm_g):
    batch, seq, d = x.shape
    mem_len = mem.shape[1]
    assert mix_norm_g.shape[0] == 1, "single-layer trunk"
    x2d = x.reshape(batch * seq, d)
    mem2d = mem.reshape(batch * mem_len, d)
    row = lambda v: v.reshape(1, -1).astype(F32)
    wb = lambda w: w[0].astype(F32)

    proj = _in_projection(x2d, row(mix_norm_g[0]), wb(w_in))
    col = lambda v: jnp.broadcast_to(v.reshape(-1, 1).astype(F32), (v.size, LANES))
    oa = _moba_attention(proj, rel_bias_table, col(moba_out_g[0]), batch, seq)
    ob = _diff_attention(proj, rel_bias_table, diff_lambda[0].astype(F32), col(diff_subln_g[0]), batch, seq)
    x2 = _outproj_cross(x2d, oa, ob, wb(w_out), row(cross_norm_g[0]), wb(w_cq),
                        mem2d, row(mem_norm_g[0]), wb(w_ck), wb(w_cv), wb(w_co), seq, mem_len)
    out = _swiglu_final(x2, row(ffn_norm_g[0]), wb(w_gate), wb(w_up), wb(w_down), row(final_norm_g))
    return out.reshape(batch, seq, d)
```

```python
import functools
import math
from typing import Callable, NamedTuple

import numpy as np
import jax
import jax.numpy as jnp
from jax import lax
from jax.experimental import pallas as pl
from jax.experimental.pallas import tpu as pltpu

F32 = jnp.float32
BF16 = jnp.bfloat16

A_HEADS = 8
A_HEAD_DIM = 64
A_WIDTH = A_HEADS * A_HEAD_DIM
MOBA_BLOCK = 256
MOBA_TOPK = 3
B_HEADS = 4
B_QK_DIM = 64
B_V_DIM = 2 * B_QK_DIM
B_WIDTH = B_HEADS * B_V_DIM
MEM_HEADS = 4
REL_BUCKETS = 32
REL_MAX_DIST = 128
EPS = 1e-6
NEG_INF = -1e30
LAMBDA_INIT = 0.8 - 0.6 * math.exp(-0.3 * 0)
QK_SCALE = A_HEAD_DIM ** -0.5
LOG2E = math.log2(math.e)

LANES = 128
SUBLANES = 8
VMEM_LIMIT_BYTES = 56 * 1024 * 1024

BLK = MOBA_BLOCK
TILES_PER_STEP = 4
TM_PROJ = 1024
TM_FFN = 1024
TM_CROSS = 1024
FF_CHUNK = 256
PROJ_CHUNK = 512


def _dot(a, b):
    return jnp.dot(a, b, preferred_element_type=F32)


def _wdot(a, w):
    return jnp.dot(a, w.astype(BF16), preferred_element_type=F32)


def _dot_nt(a, b):
    return lax.dot_general(a, b, (((1,), (1,)), ((), ())), preferred_element_type=F32)


def _rms(x, g):
    return x * lax.rsqrt(jnp.mean(x * x, axis=-1, keepdims=True) + EPS) * g


def _params(n_axes):
    return pltpu.CompilerParams(dimension_semantics=("arbitrary",) * n_axes,
                                vmem_limit_bytes=VMEM_LIMIT_BYTES)


def _resident(shape):
    return pl.BlockSpec(shape, lambda *_: (0,) * len(shape), pipeline_mode=pl.Buffered(1))


def _rel_bucket_np(dist):
    n = np.maximum(dist, 0)
    max_exact = REL_BUCKETS // 2
    ratio = np.maximum(n, max_exact).astype(np.float32) / np.float32(max_exact)
    log_ratio = np.log(ratio) / np.float32(math.log(REL_MAX_DIST / max_exact))
    large = max_exact + (log_ratio * np.float32(REL_BUCKETS - max_exact)).astype(np.int32)
    large = np.minimum(large, REL_BUCKETS - 1)
    return np.where(n < max_exact, n, large).astype(np.int32)


HALF = BLK // 2


def _bucket_tiles():
    k = np.arange(HALF)[:, None]
    q = np.arange(HALF)[None, :]
    band = np.where(q - k >= 0, _rel_bucket_np(q - k), -1)
    corner = _rel_bucket_np(HALF + q - k)
    return np.stack([band, corner]).astype(np.int32)


def _fill_bias_tiles(tab_ref, idx_ref, o_ref, first_head):
    assert HALF == REL_MAX_DIST
    idx = idx_ref[...]
    zero = jnp.zeros((HALF, HALF), F32)
    lo, hi = slice(0, HALF), slice(HALF, BLK)
    for h in range(o_ref.shape[0]):
        far = tab_ref[REL_BUCKETS - 1, first_head + h]
        acc = jnp.zeros(idx.shape, F32)
        for b in range(REL_BUCKETS - 1):
            acc = jnp.where(idx == b, (tab_ref[b, first_head + h] - far) * LOG2E, acc)
        band = jnp.where(idx[0] < 0, NEG_INF, acc[0])
        corner = acc[1]
        o_ref[h, 0, lo, lo] = band
        o_ref[h, 0, lo, hi] = corner
        o_ref[h, 0, hi, lo] = jnp.full((HALF, HALF), NEG_INF, F32)
        o_ref[h, 0, hi, hi] = band
        o_ref[h, 1, lo, lo] = zero
        o_ref[h, 1, lo, hi] = zero
        o_ref[h, 1, hi, lo] = corner
        o_ref[h, 1, hi, hi] = zero


def _bias_inputs(table):
    idx = jnp.asarray(_bucket_tiles())
    return (table.astype(F32), idx), [pl.BlockSpec(memory_space=pltpu.SMEM), _resident(idx.shape)]


def _inproj_kernel(x_ref, g_ref, w_ref, o_ref, *, q_chunks):
    hb = _rms(x_ref[...], g_ref[...]).astype(BF16)
    for j in range(w_ref.shape[1] // PROJ_CHUNK):
        cols = slice(j * PROJ_CHUNK, (j + 1) * PROJ_CHUNK)
        acc = _wdot(hb, w_ref[:, cols])
        if j in q_chunks:
            acc = acc * (QK_SCALE * LOG2E)
        o_ref[:, cols] = acc.astype(BF16)


def _in_projection(x2d, g, w_bf16):
    t, d = x2d.shape
    n = w_bf16.shape[1]
    assert t % TM_PROJ == 0 and n % PROJ_CHUNK == 0
    q_chunks = (0, (3 * A_WIDTH) // PROJ_CHUNK)
    return pl.pallas_call(
        functools.partial(_inproj_kernel, q_chunks=q_chunks),
        grid=(t // TM_PROJ,),
        in_specs=[pl.BlockSpec((TM_PROJ, d), lambda i: (i, 0)),
                  _resident((1, d)),
                  _resident((d, n))],
        out_specs=pl.BlockSpec((TM_PROJ, n), lambda i: (i, 0)),
        out_shape=jax.ShapeDtypeStruct((t, n), BF16),
        compiler_params=_params(1),
        name="in_projection",
    )(x2d, g, w_bf16)


def _half_mask(shape, half):
    lane = lax.broadcasted_iota(jnp.int32, shape, 1)
    return lax.shift_right_logical(lane, 6) == half


def _block_rows(n):
    return slice(n * BLK, (n + 1) * BLK)


class _Stream(NamedTuple):
    j: int
    slot: int
    q_diag: Callable
    q_past: Callable
    k_tile: Callable
    v_tile: Callable
    bias_tile: Callable
    row_bias: Callable
    done: Callable


def _two_pass_attention(streams, s_ref, n_diag_first):
    lead = 2
    col_max = [None] * len(streams)
    results = [None] * len(streams)

    def blocks(i):
        return list(range(streams[i].j, -1, -1))

    def score_step(i, n):
        st = streams[i]
        s = _dot_nt(st.k_tile(n), st.q_diag() if n == st.j else st.q_past())
        if n >= st.j - 1:
            s = s + st.bias_tile(st.j - n)
        if st.row_bias(n) is not None:
            s = s + st.row_bias(n)
        s_ref[st.slot, n] = s
        mx = jnp.max(s.reshape(BLK // SUBLANES, SUBLANES, BLK), axis=0)
        mx = mx if col_max[i] is None else jnp.maximum(col_max[i], mx)
        if n == 0:
            mx = jnp.max(mx, axis=0, keepdims=True)
        col_max[i] = mx

    def value_step(i, n):
        st = streams[i]
        p = jnp.exp2(jnp.maximum(s_ref[st.slot, n] - col_max[i], -150.0)).astype(BF16)
        pv = _dot(st.v_tile(n), p)
        results[i] = pv if results[i] is None else results[i] + pv

    early = max(lead, n_diag_first)
    for i in range(early):
        score_step(i, streams[i].j)
    for i in range(lead):
        for n in blocks(i)[1:]:
            score_step(i, n)
    for i in range(len(streams)):
        a = i + lead
        ahead = [] if a >= len(streams) else blocks(a)[1:] if a < early else blocks(a)
        mine = blocks(i)
        for k in range(max(len(mine), len(ahead))):
            if k < len(mine):
                value_step(i, mine[k])
            if k < len(ahead):
                score_step(a, ahead[k])
        streams[i].done(results[i])


def _over_denominator(num, den):
    r = num.shape[0] // SUBLANES
    return (num.reshape(r, SUBLANES, BLK) / den[None]).reshape(num.shape)


def _times_row(x, row8):
    r = x.shape[0] // SUBLANES
    return (x.reshape(r, SUBLANES, BLK) * row8[None]).reshape(x.shape)


ONES_ROWS = 16
MOBA_ONES_ROWS = 64


def _moba_streams(j, r, q_ref, k_ref, bias_ref, g_ref, o_ref, kmt_ref, vaug_ref, topk):
    rows = _block_rows(r)

    def head_query(h):
        qp = q_ref[rows, (h // 2) * LANES:(h // 2 + 1) * LANES]
        return jnp.where(_half_mask(qp.shape, h % 2), qp, jnp.zeros_like(qp))

    sel_rows = None
    if j > topk:
        kmt = kmt_ref[...]
        kmt_hi = kmt.astype(BF16)
        kmt_lo = (kmt - kmt_hi.astype(F32)).astype(BF16)
        q_all = q_ref[rows, :]
        gate = _dot_nt(kmt_hi, q_all) + _dot_nt(kmt_lo, q_all)
        slabs = [gate[n * A_HEADS:(n + 1) * A_HEADS, :] for n in range(j)]
        sel_rows = []
        for n in range(j):
            rank = jnp.zeros(slabs[n].shape, F32)
            for m in range(j):
                if m != n:
                    beats = (slabs[m] >= slabs[n]) if m < n else (slabs[m] > slabs[n])
                    rank = rank + jnp.where(beats, 1.0, 0.0)
            sel_rows.append(jnp.where(rank < topk, 0.0, NEG_INF))

    def selection_bias(h, n):
        if sel_rows is None or n >= j:
            return None
        return sel_rows[n][h:h + 1, :]

    heads = []

    def head_done(h, result):
        heads.append(_over_denominator(result[:A_HEAD_DIM], result[A_HEAD_DIM:A_HEAD_DIM + SUBLANES]))
        if len(heads) == A_HEADS:
            o = jnp.concatenate(heads, axis=0)
            inv = lax.rsqrt(jnp.mean(o * o, axis=0, keepdims=True) + EPS)
            g = g_ref[...]
            y = o * inv * jnp.concatenate([g, g], axis=1)
            o_ref[rows, :] = y.T.astype(BF16)

    return [_Stream(j=j, slot=h,
                    q_diag=functools.partial(head_query, h),
                    q_past=functools.partial(head_query, h),
                    k_tile=lambda n, h=h: k_ref[_block_rows(n), (h // 2) * LANES:(h // 2 + 1) * LANES],
                    v_tile=lambda n, h=h: vaug_ref[h, :, _block_rows(n)],
                    bias_tile=lambda which, h=h: bias_ref[h, which],
                    row_bias=functools.partial(selection_bias, h),
                    done=functools.partial(head_done, h))
            for h in range(A_HEADS)]


def _moba_kernel(q_ref, k_ref, v_ref, tab_ref, idx_ref, g_ref, o_ref, bias_ref, kmt_ref, vaug_ref, s_ref,
                 *, topk, per_step):
    t = pl.program_id(1)

    @pl.when((pl.program_id(0) == 0) & (t == 0))
    def _first_step():
        _fill_bias_tiles(tab_ref, idx_ref, bias_ref, first_head=0)

    seq = k_ref.shape[0]
    nb = seq // BLK
    n_pairs = A_WIDTH // LANES

    def tiles(t_static):
        streams = []
        for r in range(per_step):
            streams += _moba_streams(per_step * t_static + r, r, q_ref, k_ref, bias_ref, g_ref, o_ref,
                                     kmt_ref, vaug_ref, topk)
        _two_pass_attention(streams, s_ref, n_diag_first=A_HEADS if per_step * t_static > topk else 0)

    @pl.when(t == 0)
    def _first_tiles_of_batch():
        row = lax.broadcasted_iota(jnp.int32, (A_HEADS, A_WIDTH), 0)
        lane = lax.broadcasted_iota(jnp.int32, (A_HEADS, A_WIDTH), 1)
        head_mask = lax.shift_right_logical(lane, 6) == row
        kmt_ref[...] = jnp.zeros(kmt_ref.shape, F32)
        ones = jnp.ones((A_HEADS, BLK), BF16)
        for n in range(nb):
            km = _dot(ones, k_ref[n * BLK:(n + 1) * BLK, :]) * (1.0 / BLK)
            kmt_ref[n * A_HEADS:(n + 1) * A_HEADS, :] = jnp.where(head_mask, km, 0.0)
        for p in range(n_pairs):
            vt = v_ref[:, p * LANES:(p + 1) * LANES].T
            for half in range(2):
                h = 2 * p + half
                vaug_ref[h, :A_HEAD_DIM, :] = vt[half * A_HEAD_DIM:(half + 1) * A_HEAD_DIM]
                vaug_ref[h, A_HEAD_DIM:, :] = jnp.ones((MOBA_ONES_ROWS, seq), BF16)
        tiles(0)

    for t_static in range(1, nb // per_step):
        pl.when(t == t_static)(functools.partial(tiles, t_static))


def _moba_attention(proj, table, g, batch, seq):
    nq = seq // BLK
    per_step = min(TILES_PER_STEP, nq)
    assert seq % (per_step * BLK) == 0 and nq <= 8
    topk = min(MOBA_TOPK, nq)
    steps = nq // per_step
    bias_operands, bias_specs = _bias_inputs(table)
    return pl.pallas_call(
        functools.partial(_moba_kernel, topk=topk, per_step=per_step),
        grid=(batch, steps),
        in_specs=[pl.BlockSpec((per_step * BLK, A_WIDTH), lambda b, t: (b * steps + t, 0)),
                  pl.BlockSpec((seq, A_WIDTH), lambda b, t: (b, 1)),
                  pl.BlockSpec((seq, A_WIDTH), lambda b, t: (b, 2)),
                  *bias_specs,
                  _resident(g.shape)],
        out_specs=pl.BlockSpec((per_step * BLK, A_WIDTH), lambda b, t: (b * steps + t, 0)),
        out_shape=jax.ShapeDtypeStruct((batch * seq, A_WIDTH), BF16),
        scratch_shapes=[pltpu.VMEM((A_HEADS, 2, BLK, BLK), F32),
                        pltpu.VMEM((8 * A_HEADS, A_WIDTH), F32),
                        pltpu.VMEM((A_HEADS, A_HEAD_DIM + MOBA_ONES_ROWS, seq), BF16),
                        pltpu.VMEM((A_HEADS, nq, BLK, BLK), F32)],
        compiler_params=_params(2),
        name="moba_attention",
    )(proj, proj, proj, *bias_operands, g)


def _diff_streams(j, r, q_ref, k_ref, bias_ref, lam8, g_ref, o_ref, vaug_ref):
    rows = _block_rows(r)

    def map_query(s):
        qp = q_ref[rows, (s // 2) * LANES:(s // 2 + 1) * LANES]
        return jnp.where(_half_mask(qp.shape, s % 2), qp, jnp.zeros_like(qp))

    acc = [None] * (2 * B_HEADS)

    def map_done(s, result):
        acc[s] = result
        if s % 2 == 1:
            h = s // 2
            a0, a1 = acc[s - 1], acc[s]
            n0, l0 = a0[:B_V_DIM], a0[B_V_DIM:B_V_DIM + SUBLANES]
            n1, l1 = a1[:B_V_DIM], a1[B_V_DIM:B_V_DIM + SUBLANES]
            o = _over_denominator(_times_row(n0, l1) - _times_row(n1, lam8 * l0), l0 * l1)
            inv = lax.rsqrt(jnp.mean(o * o, axis=0, keepdims=True) + EPS)
            g = g_ref[...]
            y = o * inv * jnp.concatenate([g, g], axis=1) * (1.0 - LAMBDA_INIT)
            o_ref[rows, h * B_V_DIM:(h + 1) * B_V_DIM] = y.T.astype(BF16)

    return [_Stream(j=j, slot=s,
                    q_diag=functools.partial(map_query, s),
                    q_past=functools.partial(map_query, s),
                    k_tile=lambda n, s=s: k_ref[_block_rows(n), (s // 2) * LANES:(s // 2 + 1) * LANES],
                    v_tile=lambda n, s=s: vaug_ref[s // 2, :, _block_rows(n)],
                    bias_tile=lambda which, s=s: bias_ref[s // 2, which],
                    row_bias=lambda n: None,
                    done=functools.partial(map_done, s))
            for s in range(2 * B_HEADS)]


def _diff_kernel(q_ref, k_ref, v_ref, tab_ref, idx_ref, lam_ref, g_ref, o_ref, bias_ref, vaug_ref, s_ref,
                 *, per_step):
    t = pl.program_id(1)

    @pl.when((pl.program_id(0) == 0) & (t == 0))
    def _first_step():
        _fill_bias_tiles(tab_ref, idx_ref, bias_ref, first_head=A_HEADS)

    seq = k_ref.shape[0]

    def tiles(t_static):
        lp = lam_ref[...]
        lam = (jnp.exp(jnp.sum(lp[0:1] * lp[1:2], axis=1, keepdims=True))
               - jnp.exp(jnp.sum(lp[2:3] * lp[3:4], axis=1, keepdims=True)) + LAMBDA_INIT)
        lam8 = jnp.broadcast_to(lam, (SUBLANES, BLK))
        streams = []
        for r in range(per_step):
            streams += _diff_streams(per_step * t_static + r, r, q_ref, k_ref, bias_ref, lam8, g_ref,
                                     o_ref, vaug_ref)
        _two_pass_attention(streams, s_ref, n_diag_first=0)

    @pl.when(t == 0)
    def _first_tiles_of_batch():
        for h in range(B_HEADS):
            vaug_ref[h, :B_V_DIM, :] = v_ref[:, h * B_V_DIM:(h + 1) * B_V_DIM].T
            vaug_ref[h, B_V_DIM:, :] = jnp.ones((ONES_ROWS, seq), BF16)
        tiles(0)

    for t_static in range(1, seq // (per_step * BLK)):
        pl.when(t == t_static)(functools.partial(tiles, t_static))


def _diff_attention(proj, table, lam, g, batch, seq):
    nq = seq // BLK
    per_step = min(TILES_PER_STEP, nq)
    assert seq % (per_step * BLK) == 0 and B_V_DIM == LANES
    steps = nq // per_step
    first = (3 * A_WIDTH) // B_WIDTH
    bias_operands, bias_specs = _bias_inputs(table)
    return pl.pallas_call(
        functools.partial(_diff_kernel, per_step=per_step),
        grid=(batch, steps),
        in_specs=[pl.BlockSpec((per_step * BLK, B_WIDTH), lambda b, t: (b * steps + t, first)),
                  pl.BlockSpec((seq, B_WIDTH), lambda b, t: (b, first + 1)),
                  pl.BlockSpec((seq, B_WIDTH), lambda b, t: (b, first + 2)),
                  *bias_specs,
                  _resident(lam.shape),
                  _resident(g.shape)],
        out_specs=pl.BlockSpec((per_step * BLK, B_WIDTH), lambda b, t: (b * steps + t, 0)),
        out_shape=jax.ShapeDtypeStruct((batch * seq, B_WIDTH), BF16),
        scratch_shapes=[pltpu.VMEM((B_HEADS, 2, BLK, BLK), F32),
                        pltpu.VMEM((B_HEADS, B_V_DIM + ONES_ROWS, seq), BF16),
                        pltpu.VMEM((2 * B_HEADS, nq, BLK, BLK), F32)],
        compiler_params=_params(2),
        name="diff_attention",
    )(proj, proj, proj, *bias_operands, lam, g)


def _cross_kernel(x_ref, oa_ref, ob_ref, wo_ref, g_ref, wq_ref, mem_ref, gm_ref, wk_ref, wv_ref, wco_ref,
                  o_ref, kc_ref, vc_ref, *, scale, per_seq):
    @pl.when(pl.program_id(0) % per_seq == 0)
    def _memory_keys_values():
        m = _rms(mem_ref[...], gm_ref[...]).astype(BF16)
        kc_ref[...] = _wdot(m, wk_ref[...]).astype(BF16)
        vc_ref[...] = _wdot(m, wv_ref[...]).astype(BF16)

    x1 = (x_ref[...] + _wdot(oa_ref[...], wo_ref[:A_WIDTH, :]) + _wdot(ob_ref[...], wo_ref[A_WIDTH:, :]))
    hb = _rms(x1, g_ref[...]).astype(BF16)
    q = (_wdot(hb, wq_ref[...]) * scale).astype(BF16)
    hd = q.shape[1] // MEM_HEADS
    heads = []
    for h in range(MEM_HEADS):
        cols = slice(h * hd, (h + 1) * hd)
        s = _dot_nt(q[:, cols], kc_ref[:, cols])
        p = jnp.exp2(s - jnp.max(s, axis=1, keepdims=True))
        l = jnp.sum(p, axis=1, keepdims=True)
        heads.append((_dot(p.astype(BF16), vc_ref[:, cols]) / l).astype(BF16))
    o = jnp.concatenate(heads, axis=1)
    o_ref[...] = x1 + _wdot(o, wco_ref[...])


def _outproj_cross(x2d, oa, ob, wo, g, wq, mem2d, gm, wk, wv, wco, seq, mem_len):
    t, d = x2d.shape
    per_seq = seq // TM_CROSS
    assert seq % TM_CROSS == 0
    n = wk.shape[1]
    hd = wq.shape[1] // MEM_HEADS
    scale = hd ** -0.5 * LOG2E
    return pl.pallas_call(
        functools.partial(_cross_kernel, scale=scale, per_seq=per_seq),
        grid=(t // TM_CROSS,),
        in_specs=[pl.BlockSpec((TM_CROSS, d), lambda i: (i, 0)),
                  pl.BlockSpec((TM_CROSS, A_WIDTH), lambda i: (i, 0)),
                  pl.BlockSpec((TM_CROSS, B_WIDTH), lambda i: (i, 0)),
                  _resident(wo.shape), _resident((1, d)), _resident(wq.shape),
                  pl.BlockSpec((mem_len, d), lambda i: (i // per_seq, 0)),
                  _resident((1, d)), _resident(wk.shape), _resident(wv.shape),
                  _resident(wco.shape)],
        out_specs=pl.BlockSpec((TM_CROSS, d), lambda i: (i, 0)),
        out_shape=jax.ShapeDtypeStruct((t, d), F32),
        scratch_shapes=[pltpu.VMEM((mem_len, n), BF16),
                        pltpu.VMEM((mem_len, n), BF16)],
        compiler_params=_params(1),
        name="outproj_cross_attention",
    )(x2d, oa, ob, wo, g, wq, mem2d, gm, wk, wv, wco)


def _ffn_kernel(x_ref, g_ref, wg_ref, wu_ref, wd_ref, gf_ref, o_ref):
    x = x_ref[...]
    hb = _rms(x, g_ref[...]).astype(BF16)
    o_ref[...] = x
    for c in range(wg_ref.shape[1] // FF_CHUNK):
        cols = slice(c * FF_CHUNK, (c + 1) * FF_CHUNK)
        a = (jax.nn.silu(_wdot(hb, wg_ref[:, cols])) * _wdot(hb, wu_ref[:, cols])).astype(BF16)
        o_ref[...] += _wdot(a, wd_ref[cols, :])
    o_ref[...] = _rms(o_ref[...], gf_ref[...])


def _swiglu_final(x2d, g, wg, wu, wd, gf):
    t, d = x2d.shape
    assert t % TM_FFN == 0 and wg.shape[1] % FF_CHUNK == 0
    return pl.pallas_call(
        _ffn_kernel,
        grid=(t // TM_FFN,),
        in_specs=[pl.BlockSpec((TM_FFN, d), lambda i: (i, 0)),
                  _resident((1, d)), _resident(wg.shape), _resident(wu.shape), _resident(wd.shape),
                  _resident((1, d))],
        out_specs=pl.BlockSpec((TM_FFN, d), lambda i: (i, 0)),
        out_shape=jax.ShapeDtypeStruct((t, d), F32),
        compiler_params=_params(1),
        name="swiglu_final_norm",
    )(x2d, g, wg, wu, wd, gf)


def kernel(x, mem, mix_norm_g, w_in, moba_out_g, diff_lambda, diff_subln_g, w_out, rel_bias_table,
           cross_norm_g, mem_norm_g, w_cq, w_ck, w_cv, w_co, ffn_norm_g, w_gate, w_up, w_down,
           final_norm_g):
    batch, seq, d = x.shape
    mem_len = mem.shape[1]
    assert mix_norm_g.shape[0] == 1, "single-layer trunk"
    x2d = x.reshape(batch * seq, d)
    mem2d = mem.reshape(batch * mem_len, d)
    row = lambda v: v.reshape(1, -1).astype(F32)
    wb = lambda w: w[0].astype(F32)

    proj = _in_projection(x2d, row(mix_norm_g[0]), wb(w_in))
    col = lambda v: jnp.broadcast_to(v.reshape(-1, 1).astype(F32), (v.size, LANES))
    oa = _moba_attention(proj, rel_bias_table, col(moba_out_g[0]), batch, seq)
    ob = _diff_attention(proj, rel_bias_table, diff_lambda[0].astype(F32), col(diff_subln_g[0]), batch, seq)
    x2 = _outproj_cross(x2d, oa, ob, wb(w_out), row(cross_norm_g[0]), wb(w_cq),
                        mem2d, row(mem_norm_g[0]), wb(w_ck), wb(w_cv), wb(w_co), seq, mem_len)
    out = _swiglu_final(x2, row(ffn_norm_g[0]), wb(w_gate), wb(w_up), wb(w_down), row(final_norm_g))
    return out.reshape(batch, seq, d)
```

```python
import functools
import math
from typing import Callable, NamedTuple

import numpy as np
import jax
import jax.numpy as jnp
from jax import lax
from jax.experimental import pallas as pl
from jax.experimental.pallas import tpu as pltpu

F32 = jnp.float32
BF16 = jnp.bfloat16

A_HEADS = 8
A_HEAD_DIM = 64
A_WIDTH = A_HEADS * A_HEAD_DIM
MOBA_BLOCK = 256
MOBA_TOPK = 3
B_HEADS = 4
B_QK_DIM = 64
B_V_DIM = 2 * B_QK_DIM
B_WIDTH = B_HEADS * B_V_DIM
MEM_HEADS = 4
REL_BUCKETS = 32
REL_MAX_DIST = 128
EPS = 1e-6
NEG_INF = -1e30
LAMBDA_INIT = 0.8 - 0.6 * math.exp(-0.3 * 0)
QK_SCALE = A_HEAD_DIM ** -0.5
LOG2E = math.log2(math.e)

LANES = 128
SUBLANES = 8
VMEM_LIMIT_BYTES = 56 * 1024 * 1024

BLK = MOBA_BLOCK
TILES_PER_STEP = 4
TM_PROJ = 1024
TM_FFN = 1024
TM_CROSS = 1024
FF_CHUNK = 256
PROJ_CHUNK = 512


def _dot(a, b):
    return jnp.dot(a, b, preferred_element_type=F32)


def _wdot(a, w):
    return jnp.dot(a, w.astype(BF16), preferred_element_type=F32)


def _dot_nt(a, b):
    return lax.dot_general(a, b, (((1,), (1,)), ((), ())), preferred_element_type=F32)


def _rms(x, g):
    return x * lax.rsqrt(jnp.mean(x * x, axis=-1, keepdims=True) + EPS) * g


def _params(n_axes):
    return pltpu.CompilerParams(dimension_semantics=("arbitrary",) * n_axes,
                                vmem_limit_bytes=VMEM_LIMIT_BYTES)


def _resident(shape):
    return pl.BlockSpec(shape, lambda *_: (0,) * len(shape), pipeline_mode=pl.Buffered(1))


def _rel_bucket_np(dist):
    n = np.maximum(dist, 0)
    max_exact = REL_BUCKETS // 2
    ratio = np.maximum(n, max_exact).astype(np.float32) / np.float32(max_exact)
    log_ratio = np.log(ratio) / np.float32(math.log(REL_MAX_DIST / max_exact))
    large = max_exact + (log_ratio * np.float32(REL_BUCKETS - max_exact)).astype(np.int32)
    large = np.minimum(large, REL_BUCKETS - 1)
    return np.where(n < max_exact, n, large).astype(np.int32)


HALF = BLK // 2


def _bucket_tiles():
    k = np.arange(HALF)[:, None]
    q = np.arange(HALF)[None, :]
    band = np.where(q - k >= 0, _rel_bucket_np(q - k), -1)
    corner = _rel_bucket_np(HALF + q - k)
    return np.stack([band, corner]).astype(np.int32)


def _fill_bias_tiles(tab_ref, idx_ref, o_ref, first_head):
    assert HALF == REL_MAX_DIST
    idx = idx_ref[...]
    zero = jnp.zeros((HALF, HALF), F32)
    lo, hi = slice(0, HALF), slice(HALF, BLK)
    for h in range(o_ref.shape[0]):
        far = tab_ref[REL_BUCKETS - 1, first_head + h]
        acc = jnp.zeros(idx.shape, F32)
        for b in range(REL_BUCKETS - 1):
            acc = jnp.where(idx == b, (tab_ref[b, first_head + h] - far) * LOG2E, acc)
        band = jnp.where(idx[0] < 0, NEG_INF, acc[0])
        corner = acc[1]
        o_ref[h, 0, lo, lo] = band
        o_ref[h, 0, lo, hi] = corner
        o_ref[h, 0, hi, lo] = jnp.full((HALF, HALF), NEG_INF, F32)
        o_ref[h, 0, hi, hi] = band
        o_ref[h, 1, lo, lo] = zero
        o_ref[h, 1, lo, hi] = zero
        o_ref[h, 1, hi, lo] = corner
        o_ref[h, 1, hi, hi] = zero


def _bias_inputs(table):
    idx = jnp.asarray(_bucket_tiles())
    return (table.astype(F32), idx), [pl.BlockSpec(memory_space=pltpu.SMEM), _resident(idx.shape)]


def _inproj_kernel(x_ref, g_ref, w_ref, o_ref, *, q_chunks):
    hb = _rms(x_ref[...], g_ref[...]).astype(BF16)
    for j in range(w_ref.shape[1] // PROJ_CHUNK):
        cols = slice(j * PROJ_CHUNK, (j + 1) * PROJ_CHUNK)
        acc = _wdot(hb, w_ref[:, cols])
        if j in q_chunks:
            acc = acc * (QK_SCALE * LOG2E)
        o_ref[:, cols] = acc.astype(BF16)


def _in_projection(x2d, g, w_bf16):
    t, d = x2d.shape
    n = w_bf16.shape[1]
    assert t % TM_PROJ == 0 and n % PROJ_CHUNK == 0
    q_chunks = (0, (3 * A_WIDTH) // PROJ_CHUNK)
    return pl.pallas_call(
        functools.partial(_inproj_kernel, q_chunks=q_chunks),
        grid=(t // TM_PROJ,),
        in_specs=[pl.BlockSpec((TM_PROJ, d), lambda i: (i, 0)),
                  _resident((1, d)),
                  _resident((d, n))],
        out_specs=pl.BlockSpec((TM_PROJ, n), lambda i: (i, 0)),
        out_shape=jax.ShapeDtypeStruct((t, n), BF16),
        compiler_params=_params(1),
        name="in_projection",
    )(x2d, g, w_bf16)


def _half_mask(shape, half):
    lane = lax.broadcasted_iota(jnp.int32, shape, 1)
    return lax.shift_right_logical(lane, 6) == half


def _block_rows(n):
    return slice(n * BLK, (n + 1) * BLK)


class _Stream(NamedTuple):
    j: int
    slot: int
    q: Callable
    k_tile: Callable
    v_tile: Callable
    bias_tile: Callable
    row_bias: Callable
    done: Callable


def _two_pass_attention(streams, s_ref, n_diag_first):
    lead = 2
    queries = [None] * len(streams)
    col_max = [None] * len(streams)
    results = [None] * len(streams)

    def blocks(i):
        return list(range(streams[i].j, -1, -1))

    def score_step(i, n):
        st = streams[i]
        if queries[i] is None:
            queries[i] = st.q()
        s = _dot_nt(st.k_tile(n), queries[i])
        if n >= st.j - 1:
            s = s + st.bias_tile(st.j - n)
        if st.row_bias(n) is not None:
            s = s + st.row_bias(n)
        s_ref[st.slot, n] = s
        mx = jnp.max(s.reshape(BLK // SUBLANES, SUBLANES, BLK), axis=0)
        mx = mx if col_max[i] is None else jnp.maximum(col_max[i], mx)
        if n == 0:
            mx = jnp.max(mx, axis=0, keepdims=True)
        col_max[i] = mx

    def value_step(i, n):
        st = streams[i]
        p = jnp.exp2(s_ref[st.slot, n] - col_max[i]).astype(BF16)
        pv = _dot(st.v_tile(n), p)
        results[i] = pv if results[i] is None else results[i] + pv

    early = max(lead, n_diag_first)
    for i in range(early):
        score_step(i, streams[i].j)
    for i in range(lead):
        for n in blocks(i)[1:]:
            score_step(i, n)
    for i in range(len(streams)):
        a = i + lead
        ahead = [] if a >= len(streams) else blocks(a)[1:] if a < early else blocks(a)
        mine = blocks(i)
        for k in range(max(len(mine), len(ahead))):
            if k < len(mine):
                value_step(i, mine[k])
            if k < len(ahead):
                score_step(a, ahead[k])
        streams[i].done(results[i])


def _over_denominator(num, den):
    r = num.shape[0] // SUBLANES
    return (num.reshape(r, SUBLANES, BLK) / den[None]).reshape(num.shape)


def _times_row(x, row8):
    r = x.shape[0] // SUBLANES
    return (x.reshape(r, SUBLANES, BLK) * row8[None]).reshape(x.shape)


ONES_ROWS = 16
MOBA_ONES_ROWS = 64


def _moba_streams(j, r, q_ref, k_ref, bias_ref, g_ref, o_ref, kmt_ref, vaug_ref, topk):
    rows = _block_rows(r)

    def head_query(h):
        qp = q_ref[rows, (h // 2) * LANES:(h // 2 + 1) * LANES]
        return jnp.where(_half_mask(qp.shape, h % 2), qp, jnp.zeros_like(qp))

    sel_rows = None
    if j > topk:
        kmt = kmt_ref[...]
        kmt_hi = kmt.astype(BF16)
        kmt_lo = (kmt - kmt_hi.astype(F32)).astype(BF16)
        q_all = q_ref[rows, :]
        gate = _dot_nt(kmt_hi, q_all) + _dot_nt(kmt_lo, q_all)
        slabs = [gate[n * A_HEADS:(n + 1) * A_HEADS, :] for n in range(j)]
        sel_rows = []
        for n in range(j):
            rank = jnp.zeros(slabs[n].shape, F32)
            for m in range(j):
                if m != n:
                    beats = (slabs[m] >= slabs[n]) if m < n else (slabs[m] > slabs[n])
                    rank = rank + jnp.where(beats, 1.0, 0.0)
            sel_rows.append(jnp.where(rank < topk, 0.0, NEG_INF))

    def selection_bias(h, n):
        if sel_rows is None or n >= j:
            return None
        return sel_rows[n][h:h + 1, :]

    heads = []

    def head_done(h, result):
        heads.append(_over_denominator(result[:A_HEAD_DIM], result[A_HEAD_DIM:A_HEAD_DIM + SUBLANES]))
        if len(heads) == A_HEADS:
            o = jnp.concatenate(heads, axis=0)
            inv = lax.rsqrt(jnp.mean(o * o, axis=0, keepdims=True) + EPS)
            g = g_ref[...]
            y = o * inv * jnp.concatenate([g, g], axis=1)
            o_ref[rows, :] = y.T.astype(BF16)

    return [_Stream(j=j, slot=h,
                    q=functools.partial(head_query, h),
                    k_tile=lambda n, h=h: k_ref[_block_rows(n), (h // 2) * LANES:(h // 2 + 1) * LANES],
                    v_tile=lambda n, h=h: vaug_ref[h, :, _block_rows(n)],
                    bias_tile=lambda which, h=h: bias_ref[h, which],
                    row_bias=functools.partial(selection_bias, h),
                    done=functools.partial(head_done, h))
            for h in range(A_HEADS)]


def _moba_kernel(q_ref, k_ref, v_ref, tab_ref, idx_ref, g_ref, o_ref, bias_ref, kmt_ref, vaug_ref, s_ref,
                 *, topk, per_step):
    t = pl.program_id(1)

    @pl.when((pl.program_id(0) == 0) & (t == 0))
    def _first_step():
        _fill_bias_tiles(tab_ref, idx_ref, bias_ref, first_head=0)

    seq = k_ref.shape[0]
    nb = seq // BLK
    n_pairs = A_WIDTH // LANES

    def tiles(t_static):
        streams = []
        for r in range(per_step):
            streams += _moba_streams(per_step * t_static + r, r, q_ref, k_ref, bias_ref, g_ref, o_ref,
                                     kmt_ref, vaug_ref, topk)
        _two_pass_attention(streams, s_ref, n_diag_first=A_HEADS if per_step * t_static > topk else 0)

    @pl.when(t == 0)
    def _first_tiles_of_batch():
        row = lax.broadcasted_iota(jnp.int32, (A_HEADS, A_WIDTH), 0)
        lane = lax.broadcasted_iota(jnp.int32, (A_HEADS, A_WIDTH), 1)
        head_mask = lax.shift_right_logical(lane, 6) == row
        kmt_ref[...] = jnp.zeros(kmt_ref.shape, F32)
        ones = jnp.ones((A_HEADS, BLK), BF16)
        for n in range(nb):
            km = _dot(ones, k_ref[n * BLK:(n + 1) * BLK, :]) * (1.0 / BLK)
            kmt_ref[n * A_HEADS:(n + 1) * A_HEADS, :] = jnp.where(head_mask, km, 0.0)
        for p in range(n_pairs):
            vt = v_ref[:, p * LANES:(p + 1) * LANES].T
            for half in range(2):
                h = 2 * p + half
                vaug_ref[h, :A_HEAD_DIM, :] = vt[half * A_HEAD_DIM:(half + 1) * A_HEAD_DIM]
                vaug_ref[h, A_HEAD_DIM:, :] = jnp.ones((MOBA_ONES_ROWS, seq), BF16)
        tiles(0)

    for t_static in range(1, nb // per_step):
        pl.when(t == t_static)(functools.partial(tiles, t_static))


def _moba_attention(proj, table, g, batch, seq):
    nq = seq // BLK
    per_step = min(TILES_PER_STEP, nq)
    assert seq % (per_step * BLK) == 0 and nq <= 8
    topk = min(MOBA_TOPK, nq)
    steps = nq // per_step
    bias_operands, bias_specs = _bias_inputs(table)
    return pl.pallas_call(
        functools.partial(_moba_kernel, topk=topk, per_step=per_step),
        grid=(batch, steps),
        in_specs=[pl.BlockSpec((per_step * BLK, A_WIDTH), lambda b, t: (b * steps + t, 0)),
                  pl.BlockSpec((seq, A_WIDTH), lambda b, t: (b, 1)),
                  pl.BlockSpec((seq, A_WIDTH), lambda b, t: (b, 2)),
                  *bias_specs,
                  _resident(g.shape)],
        out_specs=pl.BlockSpec((per_step * BLK, A_WIDTH), lambda b, t: (b * steps + t, 0)),
        out_shape=jax.ShapeDtypeStruct((batch * seq, A_WIDTH), BF16),
        scratch_shapes=[pltpu.VMEM((A_HEADS, 2, BLK, BLK), F32),
                        pltpu.VMEM((8 * A_HEADS, A_WIDTH), F32),
                        pltpu.VMEM((A_HEADS, A_HEAD_DIM + MOBA_ONES_ROWS, seq), BF16),
                        pltpu.VMEM((A_HEADS, nq, BLK, BLK), F32)],
        compiler_params=_params(2),
        name="moba_attention",
    )(proj, proj, proj, *bias_operands, g)


def _diff_streams(j, r, q_ref, k_ref, bias_ref, lam8, g_ref, o_ref, vaug_ref):
    rows = _block_rows(r)

    def map_query(s):
        qp = q_ref[rows, (s // 2) * LANES:(s // 2 + 1) * LANES]
        return jnp.where(_half_mask(qp.shape, s % 2), qp, jnp.zeros_like(qp))

    acc = [None] * (2 * B_HEADS)

    def map_done(s, result):
        acc[s] = result
        if s % 2 == 1:
            h = s // 2
            a0, a1 = acc[s - 1], acc[s]
            n0, l0 = a0[:B_V_DIM], a0[B_V_DIM:B_V_DIM + SUBLANES]
            n1, l1 = a1[:B_V_DIM], a1[B_V_DIM:B_V_DIM + SUBLANES]
            o = _over_denominator(_times_row(n0, l1) - _times_row(n1, lam8 * l0), l0 * l1)
            inv = lax.rsqrt(jnp.mean(o * o, axis=0, keepdims=True) + EPS)
            g = g_ref[...]
            y = o * inv * jnp.concatenate([g, g], axis=1) * (1.0 - LAMBDA_INIT)
            o_ref[rows, h * B_V_DIM:(h + 1) * B_V_DIM] = y.T.astype(BF16)

    return [_Stream(j=j, slot=s,
                    q=functools.partial(map_query, s),
                    k_tile=lambda n, s=s: k_ref[_block_rows(n), (s // 2) * LANES:(s // 2 + 1) * LANES],
                    v_tile=lambda n, s=s: vaug_ref[s // 2, :, _block_rows(n)],
                    bias_tile=lambda which, s=s: bias_ref[s // 2, which],
                    row_bias=lambda n: None,
                    done=functools.partial(map_done, s))
            for s in range(2 * B_HEADS)]


def _diff_kernel(q_ref, k_ref, v_ref, tab_ref, idx_ref, lam_ref, g_ref, o_ref, bias_ref, vaug_ref, s_ref,
                 *, per_step):
    t = pl.program_id(1)

    @pl.when((pl.program_id(0) == 0) & (t == 0))
    def _first_step():
        _fill_bias_tiles(tab_ref, idx_ref, bias_ref, first_head=A_HEADS)

    seq = k_ref.shape[0]

    def tiles(t_static):
        lp = lam_ref[...]
        lam = (jnp.exp(jnp.sum(lp[0:1] * lp[1:2], axis=1, keepdims=True))
               - jnp.exp(jnp.sum(lp[2:3] * lp[3:4], axis=1, keepdims=True)) + LAMBDA_INIT)
        lam8 = jnp.broadcast_to(lam, (SUBLANES, BLK))
        streams = []
        for r in range(per_step):
            streams += _diff_streams(per_step * t_static + r, r, q_ref, k_ref, bias_ref, lam8, g_ref,
                                     o_ref, vaug_ref)
        _two_pass_attention(streams, s_ref, n_diag_first=0)

    @pl.when(t == 0)
    def _first_tiles_of_batch():
        for h in range(B_HEADS):
            vaug_ref[h, :B_V_DIM, :] = v_ref[:, h * B_V_DIM:(h + 1) * B_V_DIM].T
            vaug_ref[h, B_V_DIM:, :] = jnp.ones((ONES_ROWS, seq), BF16)
        tiles(0)

    for t_static in range(1, seq // (per_step * BLK)):
        pl.when(t == t_static)(functools.partial(tiles, t_static))


def _diff_attention(proj, table, lam, g, batch, seq):
    nq = seq // BLK
    per_step = min(TILES_PER_STEP, nq)
    assert seq % (per_step * BLK) == 0 and B_V_DIM == LANES
    steps = nq // per_step
    first = (3 * A_WIDTH) // B_WIDTH
    bias_operands, bias_specs = _bias_inputs(table)
    return pl.pallas_call(
        functools.partial(_diff_kernel, per_step=per_step),
        grid=(batch, steps),
        in_specs=[pl.BlockSpec((per_step * BLK, B_WIDTH), lambda b, t: (b * steps + t, first)),
                  pl.BlockSpec((seq, B_WIDTH), lambda b, t: (b, first + 1)),
                  pl.BlockSpec((seq, B_WIDTH), lambda b, t: (b, first + 2)),
                  *bias_specs,
                  _resident(lam.shape),
                  _resident(g.shape)],
        out_specs=pl.BlockSpec((per_step * BLK, B_WIDTH), lambda b, t: (b * steps + t, 0)),
        out_shape=jax.ShapeDtypeStruct((batch * seq, B_WIDTH), BF16),
        scratch_shapes=[pltpu.VMEM((B_HEADS, 2, BLK, BLK), F32),
                        pltpu.VMEM((B_HEADS, B_V_DIM + ONES_ROWS, seq), BF16),
                        pltpu.VMEM((2 * B_HEADS, nq, BLK, BLK), F32)],
        compiler_params=_params(2),
        name="diff_attention",
    )(proj, proj, proj, *bias_operands, lam, g)


def _cross_kernel(x_ref, oa_ref, ob_ref, wo_ref, g_ref, wq_ref, mem_ref, gm_ref, wk_ref, wv_ref, wco_ref,
                  o_ref, kc_ref, vc_ref, *, scale, per_seq):
    @pl.when(pl.program_id(0) % per_seq == 0)
    def _memory_keys_values():
        m = _rms(mem_ref[...], gm_ref[...]).astype(BF16)
        kc_ref[...] = _wdot(m, wk_ref[...]).astype(BF16)
        vc_ref[...] = _wdot(m, wv_ref[...]).astype(BF16)

    x1 = (x_ref[...] + _wdot(oa_ref[...], wo_ref[:A_WIDTH, :]) + _wdot(ob_ref[...], wo_ref[A_WIDTH:, :]))
    hb = _rms(x1, g_ref[...]).astype(BF16)
    q = (_wdot(hb, wq_ref[...]) * scale).astype(BF16)
    hd = q.shape[1] // MEM_HEADS
    heads = []
    for h in range(MEM_HEADS):
        cols = slice(h * hd, (h + 1) * hd)
        s = _dot_nt(q[:, cols], kc_ref[:, cols])
        p = jnp.exp2(s - jnp.max(s, axis=1, keepdims=True))
        l = jnp.sum(p, axis=1, keepdims=True)
        heads.append((_dot(p.astype(BF16), vc_ref[:, cols]) / l).astype(BF16))
    o = jnp.concatenate(heads, axis=1)
    o_ref[...] = x1 + _wdot(o, wco_ref[...])


def _outproj_cross(x2d, oa, ob, wo, g, wq, mem2d, gm, wk, wv, wco, seq, mem_len):
    t, d = x2d.shape
    per_seq = seq // TM_CROSS
    assert seq % TM_CROSS == 0
    n = wk.shape[1]
    hd = wq.shape[1] // MEM_HEADS
    scale = hd ** -0.5 * LOG2E
    return pl.pallas_call(
        functools.partial(_cross_kernel, scale=scale, per_seq=per_seq),
        grid=(t // TM_CROSS,),
        in_specs=[pl.BlockSpec((TM_CROSS, d), lambda i: (i, 0)),
                  pl.BlockSpec((TM_CROSS, A_WIDTH), lambda i: (i, 0)),
                  pl.BlockSpec((TM_CROSS, B_WIDTH), lambda i: (i, 0)),
                  _resident(wo.shape), _resident((1, d)), _resident(wq.shape),
                  pl.BlockSpec((mem_len, d), lambda i: (i // per_seq, 0)),
                  _resident((1, d)), _resident(wk.shape), _resident(wv.shape),
                  _resident(wco.shape)],
        out_specs=pl.BlockSpec((TM_CROSS, d), lambda i: (i, 0)),
        out_shape=jax.ShapeDtypeStruct((t, d), F32),
        scratch_shapes=[pltpu.VMEM((mem_len, n), BF16),
                        pltpu.VMEM((mem_len, n), BF16)],
        compiler_params=_params(1),
        name="outproj_cross_attention",
    )(x2d, oa, ob, wo, g, wq, mem2d, gm, wk, wv, wco)


def _ffn_kernel(x_ref, g_ref, wg_ref, wu_ref, wd_ref, gf_ref, o_ref):
    x = x_ref[...]
    hb = _rms(x, g_ref[...]).astype(BF16)
    o_ref[...] = x
    for c in range(wg_ref.shape[1] // FF_CHUNK):
        cols = slice(c * FF_CHUNK, (c + 1) * FF_CHUNK)
        a = (jax.nn.silu(_wdot(hb, wg_ref[:, cols])) * _wdot(hb, wu_ref[:, cols])).astype(BF16)
        o_ref[...] += _wdot(a, wd_ref[cols, :])
    o_ref[...] = _rms(o_ref[...], gf_ref[...])


def _swiglu_final(x2d, g, wg, wu, wd, gf):
    t, d = x2d.shape
    assert t % TM_FFN == 0 and wg.shape[1] % FF_CHUNK == 0
    return pl.pallas_call(
        _ffn_kernel,
        grid=(t // TM_FFN,),
        in_specs=[pl.BlockSpec((TM_FFN, d), lambda i: (i, 0)),
                  _resident((1, d)), _resident(wg.shape), _resident(wu.shape), _resident(wd.shape),
                  _resident((1, d))],
        out_specs=pl.BlockSpec((TM_FFN, d), lambda i: (i, 0)),
        out_shape=jax.ShapeDtypeStruct((t, d), F32),
        compiler_params=_params(1),
        name="swiglu_final_norm",
    )(x2d, g, wg, wu, wd, gf)


def kernel(x, mem, mix_norm_g, w_in, moba_out_g, diff_lambda, diff_subln_g, w_out, rel_bias_table,
           cross_norm_g, mem_norm_g, w_cq, w_ck, w_cv, w_co, ffn_norm_g, w_gate, w_up, w_down,
           final_norm_g):
    batch, seq, d = x.shape
    mem_len = mem.shape[1]
    assert mix_norm_g.shape[0] == 1, "single-layer trunk"
    x2d = x.reshape(batch * seq, d)
    mem2d = mem.reshape(batch * mem_len, d)
    row = lambda v: v.reshape(1, -1).astype(F32)
    wb = lambda w: w[0].astype(F32)

    proj = _in_projection(x2d, row(mix_norm_g[0]), wb(w_in))
    col = lambda v: jnp.broadcast_to(v.reshape(-1, 1).astype(F32), (v.size, LANES))
    oa = _moba_attention(proj, rel_bias_table, col(moba_out_g[0]), batch, seq)
    ob = _diff_attention(proj, rel_bias_table, diff_lambda[0].astype(F32), col(diff_subln_g[0]), batch, seq)
    x2 = _outproj_cross(x2d, oa, ob, wb(w_out), row(cross_norm_g[0]), wb(w_cq),
                        mem2d, row(mem_norm_g[0]), wb(w_ck), wb(w_cv), wb(w_co), seq, mem_len)
    out = _swiglu_final(x2, row(ffn_norm_g[0]), wb(w_gate), wb(w_up), wb(w_down), row(final_norm_g))
    return out.reshape(batch, seq, d)
```

```python
import functools
import math
from typing import Callable, NamedTuple

import numpy as np
import jax
import jax.numpy as jnp
from jax import lax
from jax.experimental import pallas as pl
from jax.experimental.pallas import tpu as pltpu

F32 = jnp.float32
BF16 = jnp.bfloat16

A_HEADS = 8
A_HEAD_DIM = 64
A_WIDTH = A_HEADS * A_HEAD_DIM
MOBA_BLOCK = 256
MOBA_TOPK = 3
B_HEADS = 4
B_QK_DIM = 64
B_V_DIM = 2 * B_QK_DIM
B_WIDTH = B_HEADS * B_V_DIM
MEM_HEADS = 4
REL_BUCKETS = 32
REL_MAX_DIST = 128
EPS = 1e-6
NEG_INF = -1e30
LAMBDA_INIT = 0.8 - 0.6 * math.exp(-0.3 * 0)
QK_SCALE = A_HEAD_DIM ** -0.5
LOG2E = math.log2(math.e)

LANES = 128
SUBLANES = 8
VMEM_LIMIT_BYTES = 56 * 1024 * 1024

BLK = MOBA_BLOCK
TILES_PER_STEP = 4
TM_PROJ = 1024
TM_FFN = 1024
TM_CROSS = 1024
FF_CHUNK = 256
PROJ_CHUNK = 512


def _dot(a, b):
    return jnp.dot(a, b, preferred_element_type=F32)


def _wdot(a, w):
    return jnp.dot(a, w.astype(BF16), preferred_element_type=F32)


def _dot_nt(a, b):
    return lax.dot_general(a, b, (((1,), (1,)), ((), ())), preferred_element_type=F32)


def _rms(x, g):
    return x * lax.rsqrt(jnp.mean(x * x, axis=-1, keepdims=True) + EPS) * g


def _params(n_axes):
    return pltpu.CompilerParams(dimension_semantics=("arbitrary",) * n_axes,
                                vmem_limit_bytes=VMEM_LIMIT_BYTES)


def _resident(shape):
    return pl.BlockSpec(shape, lambda *_: (0,) * len(shape), pipeline_mode=pl.Buffered(1))


def _rel_bucket_np(dist):
    n = np.maximum(dist, 0)
    max_exact = REL_BUCKETS // 2
    ratio = np.maximum(n, max_exact).astype(np.float32) / np.float32(max_exact)
    log_ratio = np.log(ratio) / np.float32(math.log(REL_MAX_DIST / max_exact))
    large = max_exact + (log_ratio * np.float32(REL_BUCKETS - max_exact)).astype(np.int32)
    large = np.minimum(large, REL_BUCKETS - 1)
    return np.where(n < max_exact, n, large).astype(np.int32)


HALF = BLK // 2


def _bucket_tiles():
    k = np.arange(HALF)[:, None]
    q = np.arange(HALF)[None, :]
    band = np.where(q - k >= 0, _rel_bucket_np(q - k), -1)
    corner = _rel_bucket_np(HALF + q - k)
    return np.stack([band, corner]).astype(np.int32)


def _fill_bias_tiles(tab_ref, idx_ref, o_ref, first_head):
    assert HALF == REL_MAX_DIST
    idx = idx_ref[...]
    zero = jnp.zeros((HALF, HALF), F32)
    lo, hi = slice(0, HALF), slice(HALF, BLK)
    for h in range(o_ref.shape[0]):
        far = tab_ref[REL_BUCKETS - 1, first_head + h]
        acc = jnp.zeros(idx.shape, F32)
        for b in range(REL_BUCKETS - 1):
            acc = jnp.where(idx == b, (tab_ref[b, first_head + h] - far) * LOG2E, acc)
        band = jnp.where(idx[0] < 0, NEG_INF, acc[0])
        corner = acc[1]
        o_ref[h, 0, lo, lo] = band
        o_ref[h, 0, lo, hi] = corner
        o_ref[h, 0, hi, lo] = jnp.full((HALF, HALF), NEG_INF, F32)
        o_ref[h, 0, hi, hi] = band
        o_ref[h, 1, lo, lo] = zero
        o_ref[h, 1, lo, hi] = zero
        o_ref[h, 1, hi, lo] = corner
        o_ref[h, 1, hi, hi] = zero


def _bias_inputs(table):
    idx = jnp.asarray(_bucket_tiles())
    return (table.astype(F32), idx), [pl.BlockSpec(memory_space=pltpu.SMEM), _resident(idx.shape)]


def _inproj_kernel(x_ref, g_ref, w_ref, o_ref, *, q_chunks):
    hb = _rms(x_ref[...], g_ref[...]).astype(BF16)
    for j in range(w_ref.shape[1] // PROJ_CHUNK):
        cols = slice(j * PROJ_CHUNK, (j + 1) * PROJ_CHUNK)
        acc = _wdot(hb, w_ref[:, cols])
        if j in q_chunks:
            acc = acc * (QK_SCALE * LOG2E)
        o_ref[:, cols] = acc.astype(BF16)


def _in_projection(x2d, g, w_bf16):
    t, d = x2d.shape
    n = w_bf16.shape[1]
    assert t % TM_PROJ == 0 and n % PROJ_CHUNK == 0
    q_chunks = (0, (3 * A_WIDTH) // PROJ_CHUNK)
    return pl.pallas_call(
        functools.partial(_inproj_kernel, q_chunks=q_chunks),
        grid=(t // TM_PROJ,),
        in_specs=[pl.BlockSpec((TM_PROJ, d), lambda i: (i, 0)),
                  _resident((1, d)),
                  _resident((d, n))],
        out_specs=pl.BlockSpec((TM_PROJ, n), lambda i: (i, 0)),
        out_shape=jax.ShapeDtypeStruct((t, n), BF16),
        compiler_params=_params(1),
        name="in_projection",
    )(x2d, g, w_bf16)


def _half_mask(shape, half):
    lane = lax.broadcasted_iota(jnp.int32, shape, 1)
    return lax.shift_right_logical(lane, 6) == half


def _block_rows(n):
    return slice(n * BLK, (n + 1) * BLK)


class _Stream(NamedTuple):
    j: int
    slot: int
    q: Callable
    k_tile: Callable
    v_tile: Callable
    bias_tile: Callable
    row_bias: Callable
    done: Callable


def _two_pass_attention(streams, s_ref, n_diag_first):
    lead = 2
    queries = [None] * len(streams)
    col_max = [None] * len(streams)
    results = [None] * len(streams)

    def blocks(i):
        return list(range(streams[i].j, -1, -1))

    def score_step(i, n):
        st = streams[i]
        if queries[i] is None:
            queries[i] = st.q()
        s = _dot_nt(st.k_tile(n), queries[i])
        if n == st.j:
            s = s + st.bias_tile(0)
        elif n == st.j - 1:
            near = s[HALF:, :HALF] + st.bias_tile(1)
            s = jnp.concatenate([s[:HALF], jnp.concatenate([near, s[HALF:, HALF:]], axis=1)], axis=0)
        if st.row_bias(n) is not None:
            s = s + st.row_bias(n)
        s_ref[st.slot, n] = s
        mx = jnp.max(s.reshape(BLK // SUBLANES, SUBLANES, BLK), axis=0)
        mx = mx if col_max[i] is None else jnp.maximum(col_max[i], mx)
        if n == 0:
            mx = jnp.max(mx, axis=0, keepdims=True)
        col_max[i] = mx

    def value_step(i, n):
        st = streams[i]
        p = jnp.exp2(s_ref[st.slot, n] - col_max[i]).astype(BF16)
        pv = _dot(st.v_tile(n), p)
        results[i] = pv if results[i] is None else results[i] + pv

    early = max(lead, n_diag_first)
    for i in range(early):
        score_step(i, streams[i].j)
    for i in range(lead):
        for n in blocks(i)[1:]:
            score_step(i, n)
    for i in range(len(streams)):
        a = i + lead
        ahead = [] if a >= len(streams) else blocks(a)[1:] if a < early else blocks(a)
        mine = blocks(i)
        for k in range(max(len(mine), len(ahead))):
            if k < len(mine):
                value_step(i, mine[k])
            if k < len(ahead):
                score_step(a, ahead[k])
        streams[i].done(results[i])


def _over_denominator(num, den):
    r = num.shape[0] // SUBLANES
    return (num.reshape(r, SUBLANES, BLK) / den[None]).reshape(num.shape)


def _times_row(x, row8):
    r = x.shape[0] // SUBLANES
    return (x.reshape(r, SUBLANES, BLK) * row8[None]).reshape(x.shape)


ONES_ROWS = 16
MOBA_ONES_ROWS = 64


def _moba_streams(j, r, q_ref, k_ref, bias_ref, g_ref, o_ref, kmt_ref, vaug_ref, topk):
    rows = _block_rows(r)

    def head_query(h):
        qp = q_ref[rows, (h // 2) * LANES:(h // 2 + 1) * LANES]
        return jnp.where(_half_mask(qp.shape, h % 2), qp, jnp.zeros_like(qp))

    sel_rows = None
    if j > topk:
        kmt = kmt_ref[...]
        kmt_hi = kmt.astype(BF16)
        kmt_lo = (kmt - kmt_hi.astype(F32)).astype(BF16)
        q_all = q_ref[rows, :]
        gate = _dot_nt(kmt_hi, q_all) + _dot_nt(kmt_lo, q_all)
        slabs = [gate[n * A_HEADS:(n + 1) * A_HEADS, :] for n in range(j)]
        sel_rows = []
        for n in range(j):
            rank = jnp.zeros(slabs[n].shape, F32)
            for m in range(j):
                if m != n:
                    beats = (slabs[m] >= slabs[n]) if m < n else (slabs[m] > slabs[n])
                    rank = rank + jnp.where(beats, 1.0, 0.0)
            sel_rows.append(jnp.where(rank < topk, 0.0, NEG_INF))

    def selection_bias(h, n):
        if sel_rows is None or n >= j:
            return None
        return sel_rows[n][h:h + 1, :]

    heads = []

    def head_done(h, result):
        heads.append(_over_denominator(result[:A_HEAD_DIM], result[A_HEAD_DIM:A_HEAD_DIM + SUBLANES]))
        if len(heads) == A_HEADS:
            o = jnp.concatenate(heads, axis=0)
            inv = lax.rsqrt(jnp.mean(o * o, axis=0, keepdims=True) + EPS)
            g = g_ref[...]
            y = o * inv * jnp.concatenate([g, g], axis=1)
            o_ref[rows, :] = y.T.astype(BF16)

    return [_Stream(j=j, slot=h,
                    q=functools.partial(head_query, h),
                    k_tile=lambda n, h=h: k_ref[_block_rows(n), (h // 2) * LANES:(h // 2 + 1) * LANES],
                    v_tile=lambda n, h=h: vaug_ref[h, :, _block_rows(n)],
                    bias_tile=lambda which, h=h: bias_ref[h, 0] if which == 0 else bias_ref[h, 1, HALF:, :HALF],
                    row_bias=functools.partial(selection_bias, h),
                    done=functools.partial(head_done, h))
            for h in range(A_HEADS)]


def _moba_kernel(q_ref, k_ref, v_ref, tab_ref, idx_ref, g_ref, o_ref, bias_ref, kmt_ref, vaug_ref, s_ref,
                 *, topk, per_step):
    t = pl.program_id(1)

    @pl.when((pl.program_id(0) == 0) & (t == 0))
    def _first_step():
        _fill_bias_tiles(tab_ref, idx_ref, bias_ref, first_head=0)

    seq = k_ref.shape[0]
    nb = seq // BLK
    n_pairs = A_WIDTH // LANES

    def tiles(t_static):
        streams = []
        for r in range(per_step):
            streams += _moba_streams(per_step * t_static + r, r, q_ref, k_ref, bias_ref, g_ref, o_ref,
                                     kmt_ref, vaug_ref, topk)
        _two_pass_attention(streams, s_ref, n_diag_first=A_HEADS if per_step * t_static > topk else 0)

    @pl.when(t == 0)
    def _first_tiles_of_batch():
        row = lax.broadcasted_iota(jnp.int32, (A_HEADS, A_WIDTH), 0)
        lane = lax.broadcasted_iota(jnp.int32, (A_HEADS, A_WIDTH), 1)
        head_mask = lax.shift_right_logical(lane, 6) == row
        kmt_ref[...] = jnp.zeros(kmt_ref.shape, F32)
        ones = jnp.ones((A_HEADS, BLK), BF16)
        for n in range(nb):
            km = _dot(ones, k_ref[n * BLK:(n + 1) * BLK, :]) * (1.0 / BLK)
            kmt_ref[n * A_HEADS:(n + 1) * A_HEADS, :] = jnp.where(head_mask, km, 0.0)
        for p in range(n_pairs):
            vt = v_ref[:, p * LANES:(p + 1) * LANES].T
            for half in range(2):
                h = 2 * p + half
                vaug_ref[h, :A_HEAD_DIM, :] = vt[half * A_HEAD_DIM:(half + 1) * A_HEAD_DIM]
                vaug_ref[h, A_HEAD_DIM:, :] = jnp.ones((MOBA_ONES_ROWS, seq), BF16)
        tiles(0)

    for t_static in range(1, nb // per_step):
        pl.when(t == t_static)(functools.partial(tiles, t_static))


def _moba_attention(proj, table, g, batch, seq):
    nq = seq // BLK
    per_step = min(TILES_PER_STEP, nq)
    assert seq % (per_step * BLK) == 0 and nq <= 8
    topk = min(MOBA_TOPK, nq)
    steps = nq // per_step
    bias_operands, bias_specs = _bias_inputs(table)
    return pl.pallas_call(
        functools.partial(_moba_kernel, topk=topk, per_step=per_step),
        grid=(batch, steps),
        in_specs=[pl.BlockSpec((per_step * BLK, A_WIDTH), lambda b, t: (b * steps + t, 0)),
                  pl.BlockSpec((seq, A_WIDTH), lambda b, t: (b, 1)),
                  pl.BlockSpec((seq, A_WIDTH), lambda b, t: (b, 2)),
                  *bias_specs,
                  _resident(g.shape)],
        out_specs=pl.BlockSpec((per_step * BLK, A_WIDTH), lambda b, t: (b * steps + t, 0)),
        out_shape=jax.ShapeDtypeStruct((batch * seq, A_WIDTH), BF16),
        scratch_shapes=[pltpu.VMEM((A_HEADS, 2, BLK, BLK), F32),
                        pltpu.VMEM((8 * A_HEADS, A_WIDTH), F32),
                        pltpu.VMEM((A_HEADS, A_HEAD_DIM + MOBA_ONES_ROWS, seq), BF16),
                        pltpu.VMEM((A_HEADS, nq, BLK, BLK), F32)],
        compiler_params=_params(2),
        name="moba_attention",
    )(proj, proj, proj, *bias_operands, g)


def _diff_streams(j, r, q_ref, k_ref, bias_ref, lam8, g_ref, o_ref, vaug_ref):
    rows = _block_rows(r)

    def map_query(s):
        qp = q_ref[rows, (s // 2) * LANES:(s // 2 + 1) * LANES]
        return jnp.where(_half_mask(qp.shape, s % 2), qp, jnp.zeros_like(qp))

    acc = [None] * (2 * B_HEADS)

    def map_done(s, result):
        acc[s] = result
        if s % 2 == 1:
            h = s // 2
            a0, a1 = acc[s - 1], acc[s]
            n0, l0 = a0[:B_V_DIM], a0[B_V_DIM:B_V_DIM + SUBLANES]
            n1, l1 = a1[:B_V_DIM], a1[B_V_DIM:B_V_DIM + SUBLANES]
            o = _over_denominator(_times_row(n0, l1) - _times_row(n1, lam8 * l0), l0 * l1)
            inv = lax.rsqrt(jnp.mean(o * o, axis=0, keepdims=True) + EPS)
            g = g_ref[...]
            y = o * inv * jnp.concatenate([g, g], axis=1) * (1.0 - LAMBDA_INIT)
            o_ref[rows, h * B_V_DIM:(h + 1) * B_V_DIM] = y.T.astype(BF16)

    return [_Stream(j=j, slot=s,
                    q=functools.partial(map_query, s),
                    k_tile=lambda n, s=s: k_ref[_block_rows(n), (s // 2) * LANES:(s // 2 + 1) * LANES],
                    v_tile=lambda n, s=s: vaug_ref[s // 2, :, _block_rows(n)],
                    bias_tile=lambda which, s=s: (bias_ref[s // 2, 0] if which == 0
                                                  else bias_ref[s // 2, 1, HALF:, :HALF]),
                    row_bias=lambda n: None,
                    done=functools.partial(map_done, s))
            for s in range(2 * B_HEADS)]


def _diff_kernel(q_ref, k_ref, v_ref, tab_ref, idx_ref, lam_ref, g_ref, o_ref, bias_ref, vaug_ref, s_ref,
                 *, per_step):
    t = pl.program_id(1)

    @pl.when((pl.program_id(0) == 0) & (t == 0))
    def _first_step():
        _fill_bias_tiles(tab_ref, idx_ref, bias_ref, first_head=A_HEADS)

    seq = k_ref.shape[0]

    def tiles(t_static):
        lp = lam_ref[...]
        lam = (jnp.exp(jnp.sum(lp[0:1] * lp[1:2], axis=1, keepdims=True))
               - jnp.exp(jnp.sum(lp[2:3] * lp[3:4], axis=1, keepdims=True)) + LAMBDA_INIT)
        lam8 = jnp.broadcast_to(lam, (SUBLANES, BLK))
        streams = []
        for r in range(per_step):
            streams += _diff_streams(per_step * t_static + r, r, q_ref, k_ref, bias_ref, lam8, g_ref,
                                     o_ref, vaug_ref)
        _two_pass_attention(streams, s_ref, n_diag_first=0)

    @pl.when(t == 0)
    def _first_tiles_of_batch():
        for h in range(B_HEADS):
            vaug_ref[h, :B_V_DIM, :] = v_ref[:, h * B_V_DIM:(h + 1) * B_V_DIM].T
            vaug_ref[h, B_V_DIM:, :] = jnp.ones((ONES_ROWS, seq), BF16)
        tiles(0)

    for t_static in range(1, seq // (per_step * BLK)):
        pl.when(t == t_static)(functools.partial(tiles, t_static))


def _diff_attention(proj, table, lam, g, batch, seq):
    nq = seq // BLK
    per_step = min(TILES_PER_STEP, nq)
    assert seq % (per_step * BLK) == 0 and B_V_DIM == LANES
    steps = nq // per_step
    first = (3 * A_WIDTH) // B_WIDTH
    bias_operands, bias_specs = _bias_inputs(table)
    return pl.pallas_call(
        functools.partial(_diff_kernel, per_step=per_step),
        grid=(batch, steps),
        in_specs=[pl.BlockSpec((per_step * BLK, B_WIDTH), lambda b, t: (b * steps + t, first)),
                  pl.BlockSpec((seq, B_WIDTH), lambda b, t: (b, first + 1)),
                  pl.BlockSpec((seq, B_WIDTH), lambda b, t: (b, first + 2)),
                  *bias_specs,
                  _resident(lam.shape),
                  _resident(g.shape)],
        out_specs=pl.BlockSpec((per_step * BLK, B_WIDTH), lambda b, t: (b * steps + t, 0)),
        out_shape=jax.ShapeDtypeStruct((batch * seq, B_WIDTH), BF16),
        scratch_shapes=[pltpu.VMEM((B_HEADS, 2, BLK, BLK), F32),
                        pltpu.VMEM((B_HEADS, B_V_DIM + ONES_ROWS, seq), BF16),
                        pltpu.VMEM((2 * B_HEADS, nq, BLK, BLK), F32)],
        compiler_params=_params(2),
        name="diff_attention",
    )(proj, proj, proj, *bias_operands, lam, g)


def _cross_kernel(x_ref, oa_ref, ob_ref, wo_ref, g_ref, wq_ref, mem_ref, gm_ref, wk_ref, wv_ref, wco_ref,
                  o_ref, kc_ref, vc_ref, *, scale, per_seq):
    @pl.when(pl.program_id(0) % per_seq == 0)
    def _memory_keys_values():
        m = _rms(mem_ref[...], gm_ref[...]).astype(BF16)
        kc_ref[...] = _wdot(m, wk_ref[...]).astype(BF16)
        vc_ref[...] = _wdot(m, wv_ref[...]).astype(BF16)

    x1 = (x_ref[...] + _wdot(oa_ref[...], wo_ref[:A_WIDTH, :]) + _wdot(ob_ref[...], wo_ref[A_WIDTH:, :]))
    hb = _rms(x1, g_ref[...]).astype(BF16)
    q = (_wdot(hb, wq_ref[...]) * scale).astype(BF16)
    hd = q.shape[1] // MEM_HEADS
    heads = []
    for h in range(MEM_HEADS):
        cols = slice(h * hd, (h + 1) * hd)
        s = _dot_nt(q[:, cols], kc_ref[:, cols])
        p = jnp.exp2(s - jnp.max(s, axis=1, keepdims=True))
        l = jnp.sum(p, axis=1, keepdims=True)
        heads.append((_dot(p.astype(BF16), vc_ref[:, cols]) / l).astype(BF16))
    o = jnp.concatenate(heads, axis=1)
    o_ref[...] = x1 + _wdot(o, wco_ref[...])


def _outproj_cross(x2d, oa, ob, wo, g, wq, mem2d, gm, wk, wv, wco, seq, mem_len):
    t, d = x2d.shape
    per_seq = seq // TM_CROSS
    assert seq % TM_CROSS == 0
    n = wk.shape[1]
    hd = wq.shape[1] // MEM_HEADS
    scale = hd ** -0.5 * LOG2E
    return pl.pallas_call(
        functools.partial(_cross_kernel, scale=scale, per_seq=per_seq),
        grid=(t // TM_CROSS,),
        in_specs=[pl.BlockSpec((TM_CROSS, d), lambda i: (i, 0)),
                  pl.BlockSpec((TM_CROSS, A_WIDTH), lambda i: (i, 0)),
                  pl.BlockSpec((TM_CROSS, B_WIDTH), lambda i: (i, 0)),
                  _resident(wo.shape), _resident((1, d)), _resident(wq.shape),
                  pl.BlockSpec((mem_len, d), lambda i: (i // per_seq, 0)),
                  _resident((1, d)), _resident(wk.shape), _resident(wv.shape),
                  _resident(wco.shape)],
        out_specs=pl.BlockSpec((TM_CROSS, d), lambda i: (i, 0)),
        out_shape=jax.ShapeDtypeStruct((t, d), F32),
        scratch_shapes=[pltpu.VMEM((mem_len, n), BF16),
                        pltpu.VMEM((mem_len, n), BF16)],
        compiler_params=_params(1),
        name="outproj_cross_attention",
    )(x2d, oa, ob, wo, g, wq, mem2d, gm, wk, wv, wco)


def _ffn_kernel(x_ref, g_ref, wg_ref, wu_ref, wd_ref, gf_ref, o_ref):
    x = x_ref[...]
    hb = _rms(x, g_ref[...]).astype(BF16)
    o_ref[...] = x
    for c in range(wg_ref.shape[1] // FF_CHUNK):
        cols = slice(c * FF_CHUNK, (c + 1) * FF_CHUNK)
        a = (jax.nn.silu(_wdot(hb, wg_ref[:, cols])) * _wdot(hb, wu_ref[:, cols])).astype(BF16)
        o_ref[...] += _wdot(a, wd_ref[cols, :])
    o_ref[...] = _rms(o_ref[...], gf_ref[...])


def _swiglu_final(x2d, g, wg, wu, wd, gf):
    t, d = x2d.shape
    assert t % TM_FFN == 0 and wg.shape[1] % FF_CHUNK == 0
    return pl.pallas_call(
        _ffn_kernel,
        grid=(t // TM_FFN,),
        in_specs=[pl.BlockSpec((TM_FFN, d), lambda i: (i, 0)),
                  _resident((1, d)), _resident(wg.shape), _resident(wu.shape), _resident(wd.shape),
                  _resident((1, d))],
        out_specs=pl.BlockSpec((TM_FFN, d), lambda i: (i, 0)),
        out_shape=jax.ShapeDtypeStruct((t, d), F32),
        compiler_params=_params(1),
        name="swiglu_final_norm",
    )(x2d, g, wg, wu, wd, gf)


def kernel(x, mem, mix_norm_g, w_in, moba_out_g, diff_lambda, diff_subln_g, w_out, rel_bias_table,
           cross_norm_g, mem_norm_g, w_cq, w_ck, w_cv, w_co, ffn_norm_g, w_gate, w_up, w_down,
           final_norm_g):
    batch, seq, d = x.shape
    mem_len = mem.shape[1]
    assert mix_norm_g.shape[0] == 1, "single-layer trunk"
    x2d = x.reshape(batch * seq, d)
    mem2d = mem.reshape(batch * mem_len, d)
    row = lambda v: v.reshape(1, -1).astype(F32)
    wb = lambda w: w[0].astype(F32)

    proj = _in_projection(x2d, row(mix_norm_g[0]), wb(w_in))
    col = lambda v: jnp.broadcast_to(v.reshape(-1, 1).astype(F32), (v.size, LANES))
    oa = _moba_attention(proj, rel_bias_table, col(moba_out_g[0]), batch, seq)
    ob = _diff_attention(proj, rel_bias_table, diff_lambda[0].astype(F32), col(diff_subln_g[0]), batch, seq)
    x2 = _outproj_cross(x2d, oa, ob, wb(w_out), row(cross_norm_g[0]), wb(w_cq),
                        mem2d, row(mem_norm_g[0]), wb(w_ck), wb(w_cv), wb(w_co), seq, mem_len)
    out = _swiglu_final(x2, row(ffn_norm_g[0]), wb(w_gate), wb(w_up), wb(w_down), row(final_norm_g))
    return out.reshape(batch, seq, d)
```

```python
import functools
import math
from typing import Callable, NamedTuple

import numpy as np
import jax
import jax.numpy as jnp
from jax import lax
from jax.experimental import pallas as pl
from jax.experimental.pallas import tpu as pltpu

F32 = jnp.float32
BF16 = jnp.bfloat16

A_HEADS = 8
A_HEAD_DIM = 64
A_WIDTH = A_HEADS * A_HEAD_DIM
MOBA_BLOCK = 256
MOBA_TOPK = 3
B_HEADS = 4
B_QK_DIM = 64
B_V_DIM = 2 * B_QK_DIM
B_WIDTH = B_HEADS * B_V_DIM
MEM_HEADS = 4
REL_BUCKETS = 32
REL_MAX_DIST = 128
EPS = 1e-6
NEG_INF = -1e30
LAMBDA_INIT = 0.8 - 0.6 * math.exp(-0.3 * 0)
QK_SCALE = A_HEAD_DIM ** -0.5
LOG2E = math.log2(math.e)

LANES = 128
SUBLANES = 8
VMEM_LIMIT_BYTES = 56 * 1024 * 1024

BLK = MOBA_BLOCK
TILES_PER_STEP = 4
TM_PROJ = 1024
TM_FFN = 1024
TM_CROSS = 1024
FF_CHUNK = 256
PROJ_CHUNK = 512


def _dot(a, b):
    return jnp.dot(a, b, preferred_element_type=F32)


def _wdot(a, w):
    return jnp.dot(a, w.astype(BF16), preferred_element_type=F32)


def _dot_nt(a, b):
    return lax.dot_general(a, b, (((1,), (1,)), ((), ())), preferred_element_type=F32)


def _rms(x, g):
    return x * lax.rsqrt(jnp.mean(x * x, axis=-1, keepdims=True) + EPS) * g


def _params(n_axes):
    return pltpu.CompilerParams(dimension_semantics=("arbitrary",) * n_axes,
                                vmem_limit_bytes=VMEM_LIMIT_BYTES)


def _resident(shape):
    return pl.BlockSpec(shape, lambda *_: (0,) * len(shape), pipeline_mode=pl.Buffered(1))


def _rel_bucket_np(dist):
    n = np.maximum(dist, 0)
    max_exact = REL_BUCKETS // 2
    ratio = np.maximum(n, max_exact).astype(np.float32) / np.float32(max_exact)
    log_ratio = np.log(ratio) / np.float32(math.log(REL_MAX_DIST / max_exact))
    large = max_exact + (log_ratio * np.float32(REL_BUCKETS - max_exact)).astype(np.int32)
    large = np.minimum(large, REL_BUCKETS - 1)
    return np.where(n < max_exact, n, large).astype(np.int32)


HALF = BLK // 2


def _bucket_tiles():
    k = np.arange(HALF)[:, None]
    q = np.arange(HALF)[None, :]
    band = np.where(q - k >= 0, _rel_bucket_np(q - k), -1)
    corner = _rel_bucket_np(HALF + q - k)
    return np.stack([band, corner]).astype(np.int32)


def _fill_bias_tiles(tab_ref, idx_ref, o_ref, first_head):
    assert HALF == REL_MAX_DIST
    idx = idx_ref[...]
    zero = jnp.zeros((HALF, HALF), F32)
    lo, hi = slice(0, HALF), slice(HALF, BLK)
    for h in range(o_ref.shape[0]):
        far = tab_ref[REL_BUCKETS - 1, first_head + h]
        acc = jnp.zeros(idx.shape, F32)
        for b in range(REL_BUCKETS - 1):
            acc = jnp.where(idx == b, (tab_ref[b, first_head + h] - far) * LOG2E, acc)
        band = jnp.where(idx[0] < 0, NEG_INF, acc[0])
        corner = acc[1]
        o_ref[h, 0, lo, lo] = band
        o_ref[h, 0, lo, hi] = corner
        o_ref[h, 0, hi, lo] = jnp.full((HALF, HALF), NEG_INF, F32)
        o_ref[h, 0, hi, hi] = band
        o_ref[h, 1, lo, lo] = zero
        o_ref[h, 1, lo, hi] = zero
        o_ref[h, 1, hi, lo] = corner
        o_ref[h, 1, hi, hi] = zero


def _bias_inputs(table):
    idx = jnp.asarray(_bucket_tiles())
    return (table.astype(F32), idx), [pl.BlockSpec(memory_space=pltpu.SMEM), _resident(idx.shape)]


def _inproj_kernel(x_ref, g_ref, w_ref, o_ref, *, q_chunks):
    hb = _rms(x_ref[...], g_ref[...]).astype(BF16)
    for j in range(w_ref.shape[1] // PROJ_CHUNK):
        cols = slice(j * PROJ_CHUNK, (j + 1) * PROJ_CHUNK)
        acc = _wdot(hb, w_ref[:, cols])
        if j in q_chunks:
            acc = acc * (QK_SCALE * LOG2E)
        o_ref[:, cols] = acc.astype(BF16)


def _in_projection(x2d, g, w_bf16):
    t, d = x2d.shape
    n = w_bf16.shape[1]
    assert t % TM_PROJ == 0 and n % PROJ_CHUNK == 0
    q_chunks = (0, (3 * A_WIDTH) // PROJ_CHUNK)
    return pl.pallas_call(
        functools.partial(_inproj_kernel, q_chunks=q_chunks),
        grid=(t // TM_PROJ,),
        in_specs=[pl.BlockSpec((TM_PROJ, d), lambda i: (i, 0)),
                  _resident((1, d)),
                  _resident((d, n))],
        out_specs=pl.BlockSpec((TM_PROJ, n), lambda i: (i, 0)),
        out_shape=jax.ShapeDtypeStruct((t, n), BF16),
        compiler_params=_params(1),
        name="in_projection",
    )(x2d, g, w_bf16)


def _half_mask(shape, half):
    lane = lax.broadcasted_iota(jnp.int32, shape, 1)
    return lax.shift_right_logical(lane, 6) == half


def _block_rows(n):
    return slice(n * BLK, (n + 1) * BLK)


class _Stream(NamedTuple):
    j: int
    slot: int
    q: Callable
    k_tile: Callable
    v_tile: Callable
    bias_tile: Callable
    row_bias: Callable
    done: Callable


def _two_pass_attention(streams, s_ref, n_diag_first):
    lead = 2
    queries = [None] * len(streams)
    col_max = [None] * len(streams)
    results = [None] * len(streams)

    def blocks(i):
        return list(range(streams[i].j, -1, -1))

    def score_step(i, n):
        st = streams[i]
        if queries[i] is None:
            queries[i] = st.q()
        s = _dot_nt(st.k_tile(n), queries[i])
        if n == st.j:
            bias = st.bias_tile(0)
            top = s[:HALF] + bias[:HALF]
            right = s[HALF:, HALF:] + bias[HALF:, HALF:]
            s_ref[st.slot, n, :HALF, :] = top
            s_ref[st.slot, n, HALF:, HALF:] = right
            mx = jnp.max(top.reshape(HALF // SUBLANES, SUBLANES, BLK), axis=0)
            mx_right = jnp.max(right.reshape(HALF // SUBLANES, SUBLANES, HALF), axis=0)
            mx = jnp.concatenate([mx[:, :HALF], jnp.maximum(mx[:, HALF:], mx_right)], axis=1)
            if n == 0:
                mx = jnp.max(mx, axis=0, keepdims=True)
            col_max[i] = mx
            return
        if n == st.j - 1:
            near = s[HALF:, :HALF] + st.bias_tile(1)
            s = jnp.concatenate([s[:HALF], jnp.concatenate([near, s[HALF:, HALF:]], axis=1)], axis=0)
        if st.row_bias(n) is not None:
            s = s + st.row_bias(n)
        s_ref[st.slot, n] = s
        mx = jnp.max(s.reshape(BLK // SUBLANES, SUBLANES, BLK), axis=0)
        mx = mx if col_max[i] is None else jnp.maximum(col_max[i], mx)
        if n == 0:
            mx = jnp.max(mx, axis=0, keepdims=True)
        col_max[i] = mx

    def value_step(i, n):
        st = streams[i]
        m = col_max[i]
        if n == st.j:
            top = jnp.exp2(s_ref[st.slot, n, :HALF, :] - m)
            right = jnp.exp2(s_ref[st.slot, n, HALF:, HALF:] - m[:, HALF:])
            bottom = jnp.concatenate([jnp.zeros((HALF, HALF), F32), right], axis=1)
            p = jnp.concatenate([top, bottom], axis=0).astype(BF16)
        else:
            p = jnp.exp2(s_ref[st.slot, n] - m).astype(BF16)
        pv = _dot(st.v_tile(n), p)
        results[i] = pv if results[i] is None else results[i] + pv

    early = max(lead, n_diag_first)
    for i in range(early):
        score_step(i, streams[i].j)
    for i in range(lead):
        for n in blocks(i)[1:]:
            score_step(i, n)
    for i in range(len(streams)):
        a = i + lead
        ahead = [] if a >= len(streams) else blocks(a)[1:] if a < early else blocks(a)
        mine = blocks(i)
        for k in range(max(len(mine), len(ahead))):
            if k < len(mine):
                value_step(i, mine[k])
            if k < len(ahead):
                score_step(a, ahead[k])
        streams[i].done(results[i])


def _over_denominator(num, den):
    r = num.shape[0] // SUBLANES
    return (num.reshape(r, SUBLANES, BLK) / den[None]).reshape(num.shape)


def _times_row(x, row8):
    r = x.shape[0] // SUBLANES
    return (x.reshape(r, SUBLANES, BLK) * row8[None]).reshape(x.shape)


ONES_ROWS = 16
MOBA_ONES_ROWS = 64


def _moba_streams(j, r, q_ref, k_ref, bias_ref, g_ref, o_ref, kmt_ref, vaug_ref, topk):
    rows = _block_rows(r)

    def head_query(h):
        qp = q_ref[rows, (h // 2) * LANES:(h // 2 + 1) * LANES]
        return jnp.where(_half_mask(qp.shape, h % 2), qp, jnp.zeros_like(qp))

    sel_rows = None
    if j > topk:
        kmt = kmt_ref[...]
        kmt_hi = kmt.astype(BF16)
        kmt_lo = (kmt - kmt_hi.astype(F32)).astype(BF16)
        q_all = q_ref[rows, :]
        gate = _dot_nt(kmt_hi, q_all) + _dot_nt(kmt_lo, q_all)
        slabs = [gate[n * A_HEADS:(n + 1) * A_HEADS, :] for n in range(j)]
        sel_rows = []
        for n in range(j):
            rank = jnp.zeros(slabs[n].shape, F32)
            for m in range(j):
                if m != n:
                    beats = (slabs[m] >= slabs[n]) if m < n else (slabs[m] > slabs[n])
                    rank = rank + jnp.where(beats, 1.0, 0.0)
            sel_rows.append(jnp.where(rank < topk, 0.0, NEG_INF))

    def selection_bias(h, n):
        if sel_rows is None or n >= j:
            return None
        return sel_rows[n][h:h + 1, :]

    heads = []

    def head_done(h, result):
        heads.append(_over_denominator(result[:A_HEAD_DIM], result[A_HEAD_DIM:A_HEAD_DIM + SUBLANES]))
        if len(heads) == A_HEADS:
            o = jnp.concatenate(heads, axis=0)
            inv = lax.rsqrt(jnp.mean(o * o, axis=0, keepdims=True) + EPS)
            g = g_ref[...]
            y = o * inv * jnp.concatenate([g, g], axis=1)
            o_ref[rows, :] = y.T.astype(BF16)

    return [_Stream(j=j, slot=h,
                    q=functools.partial(head_query, h),
                    k_tile=lambda n, h=h: k_ref[_block_rows(n), (h // 2) * LANES:(h // 2 + 1) * LANES],
                    v_tile=lambda n, h=h: vaug_ref[h, :, _block_rows(n)],
                    bias_tile=lambda which, h=h: bias_ref[h, 0] if which == 0 else bias_ref[h, 1, HALF:, :HALF],
                    row_bias=functools.partial(selection_bias, h),
                    done=functools.partial(head_done, h))
            for h in range(A_HEADS)]


def _moba_kernel(q_ref, k_ref, v_ref, tab_ref, idx_ref, g_ref, o_ref, bias_ref, kmt_ref, vaug_ref, s_ref,
                 *, topk, per_step):
    t = pl.program_id(1)

    @pl.when((pl.program_id(0) == 0) & (t == 0))
    def _first_step():
        _fill_bias_tiles(tab_ref, idx_ref, bias_ref, first_head=0)

    seq = k_ref.shape[0]
    nb = seq // BLK
    n_pairs = A_WIDTH // LANES

    def tiles(t_static):
        streams = []
        for r in range(per_step):
            streams += _moba_streams(per_step * t_static + r, r, q_ref, k_ref, bias_ref, g_ref, o_ref,
                                     kmt_ref, vaug_ref, topk)
        _two_pass_attention(streams, s_ref, n_diag_first=A_HEADS if per_step * t_static > topk else 0)

    @pl.when(t == 0)
    def _first_tiles_of_batch():
        row = lax.broadcasted_iota(jnp.int32, (A_HEADS, A_WIDTH), 0)
        lane = lax.broadcasted_iota(jnp.int32, (A_HEADS, A_WIDTH), 1)
        head_mask = lax.shift_right_logical(lane, 6) == row
        kmt_ref[...] = jnp.zeros(kmt_ref.shape, F32)
        ones = jnp.ones((A_HEADS, BLK), BF16)
        for n in range(nb):
            km = _dot(ones, k_ref[n * BLK:(n + 1) * BLK, :]) * (1.0 / BLK)
            kmt_ref[n * A_HEADS:(n + 1) * A_HEADS, :] = jnp.where(head_mask, km, 0.0)
        for p in range(n_pairs):
            vt = v_ref[:, p * LANES:(p + 1) * LANES].T
            for half in range(2):
                h = 2 * p + half
                vaug_ref[h, :A_HEAD_DIM, :] = vt[half * A_HEAD_DIM:(half + 1) * A_HEAD_DIM]
                vaug_ref[h, A_HEAD_DIM:, :] = jnp.ones((MOBA_ONES_ROWS, seq), BF16)
        tiles(0)

    for t_static in range(1, nb // per_step):
        pl.when(t == t_static)(functools.partial(tiles, t_static))


def _moba_attention(proj, table, g, batch, seq):
    nq = seq // BLK
    per_step = min(TILES_PER_STEP, nq)
    assert seq % (per_step * BLK) == 0 and nq <= 8
    topk = min(MOBA_TOPK, nq)
    steps = nq // per_step
    bias_operands, bias_specs = _bias_inputs(table)
    return pl.pallas_call(
        functools.partial(_moba_kernel, topk=topk, per_step=per_step),
        grid=(batch, steps),
        in_specs=[pl.BlockSpec((per_step * BLK, A_WIDTH), lambda b, t: (b * steps + t, 0)),
                  pl.BlockSpec((seq, A_WIDTH), lambda b, t: (b, 1)),
                  pl.BlockSpec((seq, A_WIDTH), lambda b, t: (b, 2)),
                  *bias_specs,
                  _resident(g.shape)],
        out_specs=pl.BlockSpec((per_step * BLK, A_WIDTH), lambda b, t: (b * steps + t, 0)),
        out_shape=jax.ShapeDtypeStruct((batch * seq, A_WIDTH), BF16),
        scratch_shapes=[pltpu.VMEM((A_HEADS, 2, BLK, BLK), F32),
                        pltpu.VMEM((8 * A_HEADS, A_WIDTH), F32),
                        pltpu.VMEM((A_HEADS, A_HEAD_DIM + MOBA_ONES_ROWS, seq), BF16),
                        pltpu.VMEM((A_HEADS, nq, BLK, BLK), F32)],
        compiler_params=_params(2),
        name="moba_attention",
    )(proj, proj, proj, *bias_operands, g)


def _diff_streams(j, r, q_ref, k_ref, bias_ref, lam8, g_ref, o_ref, vaug_ref):
    rows = _block_rows(r)

    def map_query(s):
        qp = q_ref[rows, (s // 2) * LANES:(s // 2 + 1) * LANES]
        return jnp.where(_half_mask(qp.shape, s % 2), qp, jnp.zeros_like(qp))

    acc = [None] * (2 * B_HEADS)

    def map_done(s, result):
        acc[s] = result
        if s % 2 == 1:
            h = s // 2
            a0, a1 = acc[s - 1], acc[s]
            n0, l0 = a0[:B_V_DIM], a0[B_V_DIM:B_V_DIM + SUBLANES]
            n1, l1 = a1[:B_V_DIM], a1[B_V_DIM:B_V_DIM + SUBLANES]
            o = _over_denominator(_times_row(n0, l1) - _times_row(n1, lam8 * l0), l0 * l1)
            inv = lax.rsqrt(jnp.mean(o * o, axis=0, keepdims=True) + EPS)
            g = g_ref[...]
            y = o * inv * jnp.concatenate([g, g], axis=1) * (1.0 - LAMBDA_INIT)
            o_ref[rows, h * B_V_DIM:(h + 1) * B_V_DIM] = y.T.astype(BF16)

    return [_Stream(j=j, slot=s,
                    q=functools.partial(map_query, s),
                    k_tile=lambda n, s=s: k_ref[_block_rows(n), (s // 2) * LANES:(s // 2 + 1) * LANES],
                    v_tile=lambda n, s=s: vaug_ref[s // 2, :, _block_rows(n)],
                    bias_tile=lambda which, s=s: (bias_ref[s // 2, 0] if which == 0
                                                  else bias_ref[s // 2, 1, HALF:, :HALF]),
                    row_bias=lambda n: None,
                    done=functools.partial(map_done, s))
            for s in range(2 * B_HEADS)]


def _diff_kernel(q_ref, k_ref, v_ref, tab_ref, idx_ref, lam_ref, g_ref, o_ref, bias_ref, vaug_ref, s_ref,
                 *, per_step):
    t = pl.program_id(1)

    @pl.when((pl.program_id(0) == 0) & (t == 0))
    def _first_step():
        _fill_bias_tiles(tab_ref, idx_ref, bias_ref, first_head=A_HEADS)

    seq = k_ref.shape[0]

    def tiles(t_static):
        lp = lam_ref[...]
        lam = (jnp.exp(jnp.sum(lp[0:1] * lp[1:2], axis=1, keepdims=True))
               - jnp.exp(jnp.sum(lp[2:3] * lp[3:4], axis=1, keepdims=True)) + LAMBDA_INIT)
        lam8 = jnp.broadcast_to(lam, (SUBLANES, BLK))
        streams = []
        for r in range(per_step):
            streams += _diff_streams(per_step * t_static + r, r, q_ref, k_ref, bias_ref, lam8, g_ref,
                                     o_ref, vaug_ref)
        _two_pass_attention(streams, s_ref, n_diag_first=0)

    @pl.when(t == 0)
    def _first_tiles_of_batch():
        for h in range(B_HEADS):
            vaug_ref[h, :B_V_DIM, :] = v_ref[:, h * B_V_DIM:(h + 1) * B_V_DIM].T
            vaug_ref[h, B_V_DIM:, :] = jnp.ones((ONES_ROWS, seq), BF16)
        tiles(0)

    for t_static in range(1, seq // (per_step * BLK)):
        pl.when(t == t_static)(functools.partial(tiles, t_static))


def _diff_attention(proj, table, lam, g, batch, seq):
    nq = seq // BLK
    per_step = min(TILES_PER_STEP, nq)
    assert seq % (per_step * BLK) == 0 and B_V_DIM == LANES
    steps = nq // per_step
    first = (3 * A_WIDTH) // B_WIDTH
    bias_operands, bias_specs = _bias_inputs(table)
    return pl.pallas_call(
        functools.partial(_diff_kernel, per_step=per_step),
        grid=(batch, steps),
        in_specs=[pl.BlockSpec((per_step * BLK, B_WIDTH), lambda b, t: (b * steps + t, first)),
                  pl.BlockSpec((seq, B_WIDTH), lambda b, t: (b, first + 1)),
                  pl.BlockSpec((seq, B_WIDTH), lambda b, t: (b, first + 2)),
                  *bias_specs,
                  _resident(lam.shape),
                  _resident(g.shape)],
        out_specs=pl.BlockSpec((per_step * BLK, B_WIDTH), lambda b, t: (b * steps + t, 0)),
        out_shape=jax.ShapeDtypeStruct((batch * seq, B_WIDTH), BF16),
        scratch_shapes=[pltpu.VMEM((B_HEADS, 2, BLK, BLK), F32),
                        pltpu.VMEM((B_HEADS, B_V_DIM + ONES_ROWS, seq), BF16),
                        pltpu.VMEM((2 * B_HEADS, nq, BLK, BLK), F32)],
        compiler_params=_params(2),
        name="diff_attention",
    )(proj, proj, proj, *bias_operands, lam, g)


def _cross_kernel(x_ref, oa_ref, ob_ref, wo_ref, g_ref, wq_ref, mem_ref, gm_ref, wk_ref, wv_ref, wco_ref,
                  o_ref, kc_ref, vc_ref, *, scale, per_seq):
    @pl.when(pl.program_id(0) % per_seq == 0)
    def _memory_keys_values():
        m = _rms(mem_ref[...], gm_ref[...]).astype(BF16)
        kc_ref[...] = _wdot(m, wk_ref[...]).astype(BF16)
        vc_ref[...] = _wdot(m, wv_ref[...]).astype(BF16)

    x1 = (x_ref[...] + _wdot(oa_ref[...], wo_ref[:A_WIDTH, :]) + _wdot(ob_ref[...], wo_ref[A_WIDTH:, :]))
    hb = _rms(x1, g_ref[...]).astype(BF16)
    q = (_wdot(hb, wq_ref[...]) * scale).astype(BF16)
    hd = q.shape[1] // MEM_HEADS
    heads = []
    for h in range(MEM_HEADS):
        cols = slice(h * hd, (h + 1) * hd)
        s = _dot_nt(q[:, cols], kc_ref[:, cols])
        p = jnp.exp2(s - jnp.max(s, axis=1, keepdims=True))
        l = jnp.sum(p, axis=1, keepdims=True)
        heads.append((_dot(p.astype(BF16), vc_ref[:, cols]) / l).astype(BF16))
    o = jnp.concatenate(heads, axis=1)
    o_ref[...] = x1 + _wdot(o, wco_ref[...])


def _outproj_cross(x2d, oa, ob, wo, g, wq, mem2d, gm, wk, wv, wco, seq, mem_len):
    t, d = x2d.shape
    per_seq = seq // TM_CROSS
    assert seq % TM_CROSS == 0
    n = wk.shape[1]
    hd = wq.shape[1] // MEM_HEADS
    scale = hd ** -0.5 * LOG2E
    return pl.pallas_call(
        functools.partial(_cross_kernel, scale=scale, per_seq=per_seq),
        grid=(t // TM_CROSS,),
        in_specs=[pl.BlockSpec((TM_CROSS, d), lambda i: (i, 0)),
                  pl.BlockSpec((TM_CROSS, A_WIDTH), lambda i: (i, 0)),
                  pl.BlockSpec((TM_CROSS, B_WIDTH), lambda i: (i, 0)),
                  _resident(wo.shape), _resident((1, d)), _resident(wq.shape),
                  pl.BlockSpec((mem_len, d), lambda i: (i // per_seq, 0)),
                  _resident((1, d)), _resident(wk.shape), _resident(wv.shape),
                  _resident(wco.shape)],
        out_specs=pl.BlockSpec((TM_CROSS, d), lambda i: (i, 0)),
        out_shape=jax.ShapeDtypeStruct((t, d), F32),
        scratch_shapes=[pltpu.VMEM((mem_len, n), BF16),
                        pltpu.VMEM((mem_len, n), BF16)],
        compiler_params=_params(1),
        name="outproj_cross_attention",
    )(x2d, oa, ob, wo, g, wq, mem2d, gm, wk, wv, wco)


def _ffn_kernel(x_ref, g_ref, wg_ref, wu_ref, wd_ref, gf_ref, o_ref):
    x = x_ref[...]
    hb = _rms(x, g_ref[...]).astype(BF16)
    o_ref[...] = x
    for c in range(wg_ref.shape[1] // FF_CHUNK):
        cols = slice(c * FF_CHUNK, (c + 1) * FF_CHUNK)
        a = (jax.nn.silu(_wdot(hb, wg_ref[:, cols])) * _wdot(hb, wu_ref[:, cols])).astype(BF16)
        o_ref[...] += _wdot(a, wd_ref[cols, :])
    o_ref[...] = _rms(o_ref[...], gf_ref[...])


def _swiglu_final(x2d, g, wg, wu, wd, gf):
    t, d = x2d.shape
    assert t % TM_FFN == 0 and wg.shape[1] % FF_CHUNK == 0
    return pl.pallas_call(
        _ffn_kernel,
        grid=(t // TM_FFN,),
        in_specs=[pl.BlockSpec((TM_FFN, d), lambda i: (i, 0)),
                  _resident((1, d)), _resident(wg.shape), _resident(wu.shape), _resident(wd.shape),
                  _resident((1, d))],
        out_specs=pl.BlockSpec((TM_FFN, d), lambda i: (i, 0)),
        out_shape=jax.ShapeDtypeStruct((t, d), F32),
        compiler_params=_params(1),
        name="swiglu_final_norm",
    )(x2d, g, wg, wu, wd, gf)


def kernel(x, mem, mix_norm_g, w_in, moba_out_g, diff_lambda, diff_subln_g, w_out, rel_bias_table,
           cross_norm_g, mem_norm_g, w_cq, w_ck, w_cv, w_co, ffn_norm_g, w_gate, w_up, w_down,
           final_norm_g):
    batch, seq, d = x.shape
    mem_len = mem.shape[1]
    assert mix_norm_g.shape[0] == 1, "single-layer trunk"
    x2d = x.reshape(batch * seq, d)
    mem2d = mem.reshape(batch * mem_len, d)
    row = lambda v: v.reshape(1, -1).astype(F32)
    wb = lambda w: w[0].astype(F32)

    proj = _in_projection(x2d, row(mix_norm_g[0]), wb(w_in))
    col = lambda v: jnp.broadcast_to(v.reshape(-1, 1).astype(F32), (v.size, LANES))
    oa = _moba_attention(proj, rel_bias_table, col(moba_out_g[0]), batch, seq)
    ob = _diff_attention(proj, rel_bias_table, diff_lambda[0].astype(F32), col(diff_subln_g[0]), batch, seq)
    x2 = _outproj_cross(x2d, oa, ob, wb(w_out), row(cross_norm_g[0]), wb(w_cq),
                        mem2d, row(mem_norm_g[0]), wb(w_ck), wb(w_cv), wb(w_co), seq, mem_len)
    out = _swiglu_final(x2, row(ffn_norm_g[0]), wb(w_gate), wb(w_up), wb(w_down), row(final_norm_g))
    return out.reshape(batch, seq, d)
```

```python
import functools
import math
from typing import Callable, NamedTuple

import numpy as np
import jax
import jax.numpy as jnp
from jax import lax
from jax.experimental import pallas as pl
from jax.experimental.pallas import tpu as pltpu

F32 = jnp.float32
BF16 = jnp.bfloat16

A_HEADS = 8
A_HEAD_DIM = 64
A_WIDTH = A_HEADS * A_HEAD_DIM
MOBA_BLOCK = 256
MOBA_TOPK = 3
B_HEADS = 4
B_QK_DIM = 64
B_V_DIM = 2 * B_QK_DIM
B_WIDTH = B_HEADS * B_V_DIM
MEM_HEADS = 4
REL_BUCKETS = 32
REL_MAX_DIST = 128
EPS = 1e-6
NEG_INF = -1e30
LAMBDA_INIT = 0.8 - 0.6 * math.exp(-0.3 * 0)
QK_SCALE = A_HEAD_DIM ** -0.5
LOG2E = math.log2(math.e)

LANES = 128
SUBLANES = 8
VMEM_LIMIT_BYTES = 56 * 1024 * 1024

BLK = MOBA_BLOCK
TILES_PER_STEP = 4
TM_PROJ = 1024
TM_FFN = 1024
TM_CROSS = 1024
FF_CHUNK = 256
PROJ_CHUNK = 512


def _dot(a, b):
    return jnp.dot(a, b, preferred_element_type=F32)


def _wdot(a, w):
    return jnp.dot(a, w.astype(BF16), preferred_element_type=F32)


def _dot_nt(a, b):
    return lax.dot_general(a, b, (((1,), (1,)), ((), ())), preferred_element_type=F32)


def _rms(x, g):
    return x * lax.rsqrt(jnp.mean(x * x, axis=-1, keepdims=True) + EPS) * g


def _params(n_axes):
    return pltpu.CompilerParams(dimension_semantics=("arbitrary",) * n_axes,
                                vmem_limit_bytes=VMEM_LIMIT_BYTES)


def _resident(shape):
    return pl.BlockSpec(shape, lambda *_: (0,) * len(shape), pipeline_mode=pl.Buffered(1))


def _rel_bucket_np(dist):
    n = np.maximum(dist, 0)
    max_exact = REL_BUCKETS // 2
    ratio = np.maximum(n, max_exact).astype(np.float32) / np.float32(max_exact)
    log_ratio = np.log(ratio) / np.float32(math.log(REL_MAX_DIST / max_exact))
    large = max_exact + (log_ratio * np.float32(REL_BUCKETS - max_exact)).astype(np.int32)
    large = np.minimum(large, REL_BUCKETS - 1)
    return np.where(n < max_exact, n, large).astype(np.int32)


HALF = BLK // 2


def _bucket_tiles():
    k = np.arange(HALF)[:, None]
    q = np.arange(HALF)[None, :]
    band = np.where(q - k >= 0, _rel_bucket_np(q - k), -1)
    corner = _rel_bucket_np(HALF + q - k)
    return np.stack([band, corner]).astype(np.int32)


def _fill_bias_tiles(tab_ref, idx_ref, o_ref, first_head):
    assert HALF == REL_MAX_DIST
    idx = idx_ref[...]
    zero = jnp.zeros((HALF, HALF), F32)
    lo, hi = slice(0, HALF), slice(HALF, BLK)
    for h in range(o_ref.shape[0]):
        far = tab_ref[REL_BUCKETS - 1, first_head + h]
        acc = jnp.zeros(idx.shape, F32)
        for b in range(REL_BUCKETS - 1):
            acc = jnp.where(idx == b, (tab_ref[b, first_head + h] - far) * LOG2E, acc)
        band = jnp.where(idx[0] < 0, NEG_INF, acc[0])
        corner = acc[1]
        o_ref[h, 0, lo, lo] = band
        o_ref[h, 0, lo, hi] = corner
        o_ref[h, 0, hi, lo] = jnp.full((HALF, HALF), NEG_INF, F32)
        o_ref[h, 0, hi, hi] = band
        o_ref[h, 1, lo, lo] = zero
        o_ref[h, 1, lo, hi] = zero
        o_ref[h, 1, hi, lo] = corner
        o_ref[h, 1, hi, hi] = zero


def _bias_inputs(table):
    idx = jnp.asarray(_bucket_tiles())
    return (table.astype(F32), idx), [pl.BlockSpec(memory_space=pltpu.SMEM), _resident(idx.shape)]


def _inproj_kernel(x_ref, g_ref, w_ref, o_ref, *, q_chunks):
    hb = _rms(x_ref[...], g_ref[...]).astype(BF16)
    for j in range(w_ref.shape[1] // PROJ_CHUNK):
        cols = slice(j * PROJ_CHUNK, (j + 1) * PROJ_CHUNK)
        acc = _wdot(hb, w_ref[:, cols])
        if j in q_chunks:
            acc = acc * (QK_SCALE * LOG2E)
        o_ref[:, cols] = acc.astype(BF16)


def _in_projection(x2d, g, w_bf16):
    t, d = x2d.shape
    n = w_bf16.shape[1]
    assert t % TM_PROJ == 0 and n % PROJ_CHUNK == 0
    q_chunks = (0, (3 * A_WIDTH) // PROJ_CHUNK)
    return pl.pallas_call(
        functools.partial(_inproj_kernel, q_chunks=q_chunks),
        grid=(t // TM_PROJ,),
        in_specs=[pl.BlockSpec((TM_PROJ, d), lambda i: (i, 0)),
                  _resident((1, d)),
                  _resident((d, n))],
        out_specs=pl.BlockSpec((TM_PROJ, n), lambda i: (i, 0)),
        out_shape=jax.ShapeDtypeStruct((t, n), BF16),
        compiler_params=_params(1),
        name="in_projection",
    )(x2d, g, w_bf16)


def _half_mask(shape, half):
    lane = lax.broadcasted_iota(jnp.int32, shape, 1)
    return lax.shift_right_logical(lane, 6) == half


def _block_rows(n):
    return slice(n * BLK, (n + 1) * BLK)


class _Stream(NamedTuple):
    j: int
    slot: int
    q: Callable
    k_tile: Callable
    v_tile: Callable
    bias_tile: Callable
    row_bias: Callable
    done: Callable


def _two_pass_attention(streams, s_ref, n_diag_first):
    lead = 2
    queries = [None] * len(streams)
    col_max = [None] * len(streams)
    results = [None] * len(streams)

    def blocks(i):
        return list(range(streams[i].j, -1, -1))

    def score_step(i, n):
        st = streams[i]
        if queries[i] is None:
            queries[i] = st.q()
        s = _dot_nt(st.k_tile(n), queries[i])
        if n == st.j:
            bias = st.bias_tile(0)
            top = s[:HALF] + bias[:HALF]
            right = s[HALF:, HALF:] + bias[HALF:, HALF:]
            s_ref[st.slot, n, :HALF, :] = top
            s_ref[st.slot, n, HALF:, HALF:] = right
            mx = jnp.max(top.reshape(HALF // SUBLANES, SUBLANES, BLK), axis=0)
            mx_right = jnp.max(right.reshape(HALF // SUBLANES, SUBLANES, HALF), axis=0)
            mx = jnp.concatenate([mx[:, :HALF], jnp.maximum(mx[:, HALF:], mx_right)], axis=1)
            if n == 0:
                mx = jnp.max(mx, axis=0, keepdims=True)
            col_max[i] = mx
            return
        if n == st.j - 1:
            near = s[HALF:, :HALF] + st.bias_tile(1)
            s = jnp.concatenate([s[:HALF], jnp.concatenate([near, s[HALF:, HALF:]], axis=1)], axis=0)
        s_ref[st.slot, n] = s
        mx = jnp.max(s.reshape(BLK // SUBLANES, SUBLANES, BLK), axis=0)
        if st.row_bias(n) is not None:
            mx = mx + st.row_bias(n)
        mx = mx if col_max[i] is None else jnp.maximum(col_max[i], mx)
        if n == 0:
            mx = jnp.max(mx, axis=0, keepdims=True)
        col_max[i] = mx

    def value_step(i, n):
        st = streams[i]
        m = col_max[i]
        if n == st.j:
            top = jnp.exp2(s_ref[st.slot, n, :HALF, :] - m)
            right = jnp.exp2(s_ref[st.slot, n, HALF:, HALF:] - m[:, HALF:])
            bottom = jnp.concatenate([jnp.zeros((HALF, HALF), F32), right], axis=1)
            p = jnp.concatenate([top, bottom], axis=0).astype(BF16)
        else:
            if st.row_bias(n) is not None:
                m = m - st.row_bias(n)
            p = jnp.exp2(s_ref[st.slot, n] - m).astype(BF16)
        pv = _dot(st.v_tile(n), p)
        results[i] = pv if results[i] is None else results[i] + pv

    early = max(lead, n_diag_first)
    for i in range(early):
        score_step(i, streams[i].j)
    for i in range(lead):
        for n in blocks(i)[1:]:
            score_step(i, n)
    for i in range(len(streams)):
        a = i + lead
        ahead = [] if a >= len(streams) else blocks(a)[1:] if a < early else blocks(a)
        mine = blocks(i)
        for k in range(max(len(mine), len(ahead))):
            if k < len(mine):
                value_step(i, mine[k])
            if k < len(ahead):
                score_step(a, ahead[k])
        streams[i].done(results[i])


def _over_denominator(num, den):
    r = num.shape[0] // SUBLANES
    return (num.reshape(r, SUBLANES, BLK) / den[None]).reshape(num.shape)


def _times_row(x, row8):
    r = x.shape[0] // SUBLANES
    return (x.reshape(r, SUBLANES, BLK) * row8[None]).reshape(x.shape)


ONES_ROWS = 16
MOBA_ONES_ROWS = 64


def _moba_streams(j, r, q_ref, k_ref, bias_ref, g_ref, o_ref, kmt_ref, vaug_ref, topk):
    rows = _block_rows(r)

    def head_query(h):
        qp = q_ref[rows, (h // 2) * LANES:(h // 2 + 1) * LANES]
        return jnp.where(_half_mask(qp.shape, h % 2), qp, jnp.zeros_like(qp))

    sel_rows = None
    if j > topk:
        kmt = kmt_ref[...]
        kmt_hi = kmt.astype(BF16)
        kmt_lo = (kmt - kmt_hi.astype(F32)).astype(BF16)
        q_all = q_ref[rows, :]
        gate = _dot_nt(kmt_hi, q_all) + _dot_nt(kmt_lo, q_all)
        slabs = [gate[n * A_HEADS:(n + 1) * A_HEADS, :] for n in range(j)]
        sel_rows = []
        for n in range(j):
            rank = jnp.zeros(slabs[n].shape, F32)
            for m in range(j):
                if m != n:
                    beats = (slabs[m] >= slabs[n]) if m < n else (slabs[m] > slabs[n])
                    rank = rank + jnp.where(beats, 1.0, 0.0)
            sel_rows.append(jnp.where(rank < topk, 0.0, NEG_INF))

    def selection_bias(h, n):
        if sel_rows is None or n >= j:
            return None
        return sel_rows[n][h:h + 1, :]

    heads = []

    def head_done(h, result):
        heads.append(_over_denominator(result[:A_HEAD_DIM], result[A_HEAD_DIM:A_HEAD_DIM + SUBLANES]))
        if len(heads) == A_HEADS:
            o = jnp.concatenate(heads, axis=0)
            inv = lax.rsqrt(jnp.mean(o * o, axis=0, keepdims=True) + EPS)
            g = g_ref[...]
            y = o * inv * jnp.concatenate([g, g], axis=1)
            o_ref[rows, :] = y.T.astype(BF16)

    return [_Stream(j=j, slot=h,
                    q=functools.partial(head_query, h),
                    k_tile=lambda n, h=h: k_ref[_block_rows(n), (h // 2) * LANES:(h // 2 + 1) * LANES],
                    v_tile=lambda n, h=h: vaug_ref[h, :, _block_rows(n)],
                    bias_tile=lambda which, h=h: bias_ref[h, 0] if which == 0 else bias_ref[h, 1, HALF:, :HALF],
                    row_bias=functools.partial(selection_bias, h),
                    done=functools.partial(head_done, h))
            for h in range(A_HEADS)]


def _moba_kernel(q_ref, k_ref, v_ref, tab_ref, idx_ref, g_ref, o_ref, bias_ref, kmt_ref, vaug_ref, s_ref,
                 *, topk, per_step):
    t = pl.program_id(1)

    @pl.when((pl.program_id(0) == 0) & (t == 0))
    def _first_step():
        _fill_bias_tiles(tab_ref, idx_ref, bias_ref, first_head=0)

    seq = k_ref.shape[0]
    nb = seq // BLK
    n_pairs = A_WIDTH // LANES

    def tiles(t_static):
        streams = []
        for r in range(per_step):
            streams += _moba_streams(per_step * t_static + r, r, q_ref, k_ref, bias_ref, g_ref, o_ref,
                                     kmt_ref, vaug_ref, topk)
        _two_pass_attention(streams, s_ref, n_diag_first=A_HEADS if per_step * t_static > topk else 0)

    @pl.when(t == 0)
    def _first_tiles_of_batch():
        row = lax.broadcasted_iota(jnp.int32, (A_HEADS, A_WIDTH), 0)
        lane = lax.broadcasted_iota(jnp.int32, (A_HEADS, A_WIDTH), 1)
        head_mask = lax.shift_right_logical(lane, 6) == row
        kmt_ref[...] = jnp.zeros(kmt_ref.shape, F32)
        ones = jnp.ones((A_HEADS, BLK), BF16)
        for n in range(nb):
            km = _dot(ones, k_ref[n * BLK:(n + 1) * BLK, :]) * (1.0 / BLK)
            kmt_ref[n * A_HEADS:(n + 1) * A_HEADS, :] = jnp.where(head_mask, km, 0.0)
        for p in range(n_pairs):
            vt = v_ref[:, p * LANES:(p + 1) * LANES].T
            for half in range(2):
                h = 2 * p + half
                vaug_ref[h, :A_HEAD_DIM, :] = vt[half * A_HEAD_DIM:(half + 1) * A_HEAD_DIM]
                vaug_ref[h, A_HEAD_DIM:, :] = jnp.ones((MOBA_ONES_ROWS, seq), BF16)
        tiles(0)

    for t_static in range(1, nb // per_step):
        pl.when(t == t_static)(functools.partial(tiles, t_static))


def _moba_attention(proj, table, g, batch, seq):
    nq = seq // BLK
    per_step = min(TILES_PER_STEP, nq)
    assert seq % (per_step * BLK) == 0 and nq <= 8
    topk = min(MOBA_TOPK, nq)
    steps = nq // per_step
    bias_operands, bias_specs = _bias_inputs(table)
    return pl.pallas_call(
        functools.partial(_moba_kernel, topk=topk, per_step=per_step),
        grid=(batch, steps),
        in_specs=[pl.BlockSpec((per_step * BLK, A_WIDTH), lambda b, t: (b * steps + t, 0)),
                  pl.BlockSpec((seq, A_WIDTH), lambda b, t: (b, 1)),
                  pl.BlockSpec((seq, A_WIDTH), lambda b, t: (b, 2)),
                  *bias_specs,
                  _resident(g.shape)],
        out_specs=pl.BlockSpec((per_step * BLK, A_WIDTH), lambda b, t: (b * steps + t, 0)),
        out_shape=jax.ShapeDtypeStruct((batch * seq, A_WIDTH), BF16),
        scratch_shapes=[pltpu.VMEM((A_HEADS, 2, BLK, BLK), F32),
                        pltpu.VMEM((8 * A_HEADS, A_WIDTH), F32),
                        pltpu.VMEM((A_HEADS, A_HEAD_DIM + MOBA_ONES_ROWS, seq), BF16),
                        pltpu.VMEM((A_HEADS, nq, BLK, BLK), F32)],
        compiler_params=_params(2),
        name="moba_attention",
    )(proj, proj, proj, *bias_operands, g)


def _diff_streams(j, r, q_ref, k_ref, bias_ref, lam8, g_ref, o_ref, vaug_ref):
    rows = _block_rows(r)

    def map_query(s):
        qp = q_ref[rows, (s // 2) * LANES:(s // 2 + 1) * LANES]
        return jnp.where(_half_mask(qp.shape, s % 2), qp, jnp.zeros_like(qp))

    acc = [None] * (2 * B_HEADS)

    def map_done(s, result):
        acc[s] = result
        if s % 2 == 1:
            h = s // 2
            a0, a1 = acc[s - 1], acc[s]
            n0, l0 = a0[:B_V_DIM], a0[B_V_DIM:B_V_DIM + SUBLANES]
            n1, l1 = a1[:B_V_DIM], a1[B_V_DIM:B_V_DIM + SUBLANES]
            o = _over_denominator(_times_row(n0, l1) - _times_row(n1, lam8 * l0), l0 * l1)
            inv = lax.rsqrt(jnp.mean(o * o, axis=0, keepdims=True) + EPS)
            g = g_ref[...]
            y = o * inv * jnp.concatenate([g, g], axis=1) * (1.0 - LAMBDA_INIT)
            o_ref[rows, h * B_V_DIM:(h + 1) * B_V_DIM] = y.T.astype(BF16)

    return [_Stream(j=j, slot=s,
                    q=functools.partial(map_query, s),
                    k_tile=lambda n, s=s: k_ref[_block_rows(n), (s // 2) * LANES:(s // 2 + 1) * LANES],
                    v_tile=lambda n, s=s: vaug_ref[s // 2, :, _block_rows(n)],
                    bias_tile=lambda which, s=s: (bias_ref[s // 2, 0] if which == 0
                                                  else bias_ref[s // 2, 1, HALF:, :HALF]),
                    row_bias=lambda n: None,
                    done=functools.partial(map_done, s))
            for s in range(2 * B_HEADS)]


def _diff_kernel(q_ref, k_ref, v_ref, tab_ref, idx_ref, lam_ref, g_ref, o_ref, bias_ref, vaug_ref, s_ref,
                 *, per_step):
    t = pl.program_id(1)

    @pl.when((pl.program_id(0) == 0) & (t == 0))
    def _first_step():
        _fill_bias_tiles(tab_ref, idx_ref, bias_ref, first_head=A_HEADS)

    seq = k_ref.shape[0]

    def tiles(t_static):
        lp = lam_ref[...]
        lam = (jnp.exp(jnp.sum(lp[0:1] * lp[1:2], axis=1, keepdims=True))
               - jnp.exp(jnp.sum(lp[2:3] * lp[3:4], axis=1, keepdims=True)) + LAMBDA_INIT)
        lam8 = jnp.broadcast_to(lam, (SUBLANES, BLK))
        streams = []
        for r in range(per_step):
            streams += _diff_streams(per_step * t_static + r, r, q_ref, k_ref, bias_ref, lam8, g_ref,
                                     o_ref, vaug_ref)
        _two_pass_attention(streams, s_ref, n_diag_first=0)

    @pl.when(t == 0)
    def _first_tiles_of_batch():
        for h in range(B_HEADS):
            vaug_ref[h, :B_V_DIM, :] = v_ref[:, h * B_V_DIM:(h + 1) * B_V_DIM].T
            vaug_ref[h, B_V_DIM:, :] = jnp.ones((ONES_ROWS, seq), BF16)
        tiles(0)

    for t_static in range(1, seq // (per_step * BLK)):
        pl.when(t == t_static)(functools.partial(tiles, t_static))


def _diff_attention(proj, table, lam, g, batch, seq):
    nq = seq // BLK
    per_step = min(TILES_PER_STEP, nq)
    assert seq % (per_step * BLK) == 0 and B_V_DIM == LANES
    steps = nq // per_step
    first = (3 * A_WIDTH) // B_WIDTH
    bias_operands, bias_specs = _bias_inputs(table)
    return pl.pallas_call(
        functools.partial(_diff_kernel, per_step=per_step),
        grid=(batch, steps),
        in_specs=[pl.BlockSpec((per_step * BLK, B_WIDTH), lambda b, t: (b * steps + t, first)),
                  pl.BlockSpec((seq, B_WIDTH), lambda b, t: (b, first + 1)),
                  pl.BlockSpec((seq, B_WIDTH), lambda b, t: (b, first + 2)),
                  *bias_specs,
                  _resident(lam.shape),
                  _resident(g.shape)],
        out_specs=pl.BlockSpec((per_step * BLK, B_WIDTH), lambda b, t: (b * steps + t, 0)),
        out_shape=jax.ShapeDtypeStruct((batch * seq, B_WIDTH), BF16),
        scratch_shapes=[pltpu.VMEM((B_HEADS, 2, BLK, BLK), F32),
                        pltpu.VMEM((B_HEADS, B_V_DIM + ONES_ROWS, seq), BF16),
                        pltpu.VMEM((2 * B_HEADS, nq, BLK, BLK), F32)],
        compiler_params=_params(2),
        name="diff_attention",
    )(proj, proj, proj, *bias_operands, lam, g)


def _cross_kernel(x_ref, oa_ref, ob_ref, wo_ref, g_ref, wq_ref, mem_ref, gm_ref, wk_ref, wv_ref, wco_ref,
                  o_ref, kc_ref, vc_ref, *, scale, per_seq):
    @pl.when(pl.program_id(0) % per_seq == 0)
    def _memory_keys_values():
        m = _rms(mem_ref[...], gm_ref[...]).astype(BF16)
        kc_ref[...] = _wdot(m, wk_ref[...]).astype(BF16)
        vc_ref[...] = _wdot(m, wv_ref[...]).astype(BF16)

    x1 = (x_ref[...] + _wdot(oa_ref[...], wo_ref[:A_WIDTH, :]) + _wdot(ob_ref[...], wo_ref[A_WIDTH:, :]))
    hb = _rms(x1, g_ref[...]).astype(BF16)
    q = (_wdot(hb, wq_ref[...]) * scale).astype(BF16)
    hd = q.shape[1] // MEM_HEADS
    heads = []
    for h in range(MEM_HEADS):
        cols = slice(h * hd, (h + 1) * hd)
        s = _dot_nt(q[:, cols], kc_ref[:, cols])
        p = jnp.exp2(s - jnp.max(s, axis=1, keepdims=True))
        l = jnp.sum(p, axis=1, keepdims=True)
        heads.append((_dot(p.astype(BF16), vc_ref[:, cols]) / l).astype(BF16))
    o = jnp.concatenate(heads, axis=1)
    o_ref[...] = x1 + _wdot(o, wco_ref[...])


def _outproj_cross(x2d, oa, ob, wo, g, wq, mem2d, gm, wk, wv, wco, seq, mem_len):
    t, d = x2d.shape
    per_seq = seq // TM_CROSS
    assert seq % TM_CROSS == 0
    n = wk.shape[1]
    hd = wq.shape[1] // MEM_HEADS
    scale = hd ** -0.5 * LOG2E
    return pl.pallas_call(
        functools.partial(_cross_kernel, scale=scale, per_seq=per_seq),
        grid=(t // TM_CROSS,),
        in_specs=[pl.BlockSpec((TM_CROSS, d), lambda i: (i, 0)),
                  pl.BlockSpec((TM_CROSS, A_WIDTH), lambda i: (i, 0)),
                  pl.BlockSpec((TM_CROSS, B_WIDTH), lambda i: (i, 0)),
                  _resident(wo.shape), _resident((1, d)), _resident(wq.shape),
                  pl.BlockSpec((mem_len, d), lambda i: (i // per_seq, 0)),
                  _resident((1, d)), _resident(wk.shape), _resident(wv.shape),
                  _resident(wco.shape)],
        out_specs=pl.BlockSpec((TM_CROSS, d), lambda i: (i, 0)),
        out_shape=jax.ShapeDtypeStruct((t, d), F32),
        scratch_shapes=[pltpu.VMEM((mem_len, n), BF16),
                        pltpu.VMEM((mem_len, n), BF16)],
        compiler_params=_params(1),
        name="outproj_cross_attention",
    )(x2d, oa, ob, wo, g, wq, mem2d, gm, wk, wv, wco)


def _ffn_kernel(x_ref, g_ref, wg_ref, wu_ref, wd_ref, gf_ref, o_ref):
    x = x_ref[...]
    hb = _rms(x, g_ref[...]).astype(BF16)
    o_ref[...] = x
    for c in range(wg_ref.shape[1] // FF_CHUNK):
        cols = slice(c * FF_CHUNK, (c + 1) * FF_CHUNK)
        a = (jax.nn.silu(_wdot(hb, wg_ref[:, cols])) * _wdot(hb, wu_ref[:, cols])).astype(BF16)
        o_ref[...] += _wdot(a, wd_ref[cols, :])
    o_ref[...] = _rms(o_ref[...], gf_ref[...])


def _swiglu_final(x2d, g, wg, wu, wd, gf):
    t, d = x2d.shape
    assert t % TM_FFN == 0 and wg.shape[1] % FF_CHUNK == 0
    return pl.pallas_call(
        _ffn_kernel,
        grid=(t // TM_FFN,),
        in_specs=[pl.BlockSpec((TM_FFN, d), lambda i: (i, 0)),
                  _resident((1, d)), _resident(wg.shape), _resident(wu.shape), _resident(wd.shape),
                  _resident((1, d))],
        out_specs=pl.BlockSpec((TM_FFN, d), lambda i: (i, 0)),
        out_shape=jax.ShapeDtypeStruct((t, d), F32),
        compiler_params=_params(1),
        name="swiglu_final_norm",
    )(x2d, g, wg, wu, wd, gf)


def kernel(x, mem, mix_norm_g, w_in, moba_out_g, diff_lambda, diff_subln_g, w_out, rel_bias_table,
           cross_norm_g, mem_norm_g, w_cq, w_ck, w_cv, w_co, ffn_norm_g, w_gate, w_up, w_down,
           final_norm_g):
    batch, seq, d = x.shape
    mem_len = mem.shape[1]
    assert mix_norm_g.shape[0] == 1, "single-layer trunk"
    x2d = x.reshape(batch * seq, d)
    mem2d = mem.reshape(batch * mem_len, d)
    row = lambda v: v.reshape(1, -1).astype(F32)
    wb = lambda w: w[0].astype(F32)

    proj = _in_projection(x2d, row(mix_norm_g[0]), wb(w_in))
    col = lambda v: jnp.broadcast_to(v.reshape(-1, 1).astype(F32), (v.size, LANES))
    oa = _moba_attention(proj, rel_bias_table, col(moba_out_g[0]), batch, seq)
    ob = _diff_attention(proj, rel_bias_table, diff_lambda[0].astype(F32), col(diff_subln_g[0]), batch, seq)
    x2 = _outproj_cross(x2d, oa, ob, wb(w_out), row(cross_norm_g[0]), wb(w_cq),
                        mem2d, row(mem_norm_g[0]), wb(w_ck), wb(w_cv), wb(w_co), seq, mem_len)
    out = _swiglu_final(x2, row(ffn_norm_g[0]), wb(w_gate), wb(w_up), wb(w_down), row(final_norm_g))
    return out.reshape(batch, seq, d)
```

```python
import functools
import math
from typing import Callable, NamedTuple

import numpy as np
import jax
import jax.numpy as jnp
from jax import lax
from jax.experimental import pallas as pl
from jax.experimental.pallas import tpu as pltpu

F32 = jnp.float32
BF16 = jnp.bfloat16

A_HEADS = 8
A_HEAD_DIM = 64
A_WIDTH = A_HEADS * A_HEAD_DIM
MOBA_BLOCK = 256
MOBA_TOPK = 3
B_HEADS = 4
B_QK_DIM = 64
B_V_DIM = 2 * B_QK_DIM
B_WIDTH = B_HEADS * B_V_DIM
MEM_HEADS = 4
REL_BUCKETS = 32
REL_MAX_DIST = 128
EPS = 1e-6
NEG_INF = -1e30
LAMBDA_INIT = 0.8 - 0.6 * math.exp(-0.3 * 0)
QK_SCALE = A_HEAD_DIM ** -0.5
LOG2E = math.log2(math.e)

LANES = 128
SUBLANES = 8
VMEM_LIMIT_BYTES = 56 * 1024 * 1024

BLK = MOBA_BLOCK
TILES_PER_STEP = 4
TM_PROJ = 1024
TM_FFN = 1024
TM_CROSS = 1024
FF_CHUNK = 256
PROJ_CHUNK = 512


def _dot(a, b):
    return jnp.dot(a, b, preferred_element_type=F32)


def _wdot(a, w):
    return jnp.dot(a, w.astype(BF16), preferred_element_type=F32)


def _dot_nt(a, b):
    return lax.dot_general(a, b, (((1,), (1,)), ((), ())), preferred_element_type=F32)


def _rms(x, g):
    return x * lax.rsqrt(jnp.mean(x * x, axis=-1, keepdims=True) + EPS) * g


def _params(n_axes):
    return pltpu.CompilerParams(dimension_semantics=("arbitrary",) * n_axes,
                                vmem_limit_bytes=VMEM_LIMIT_BYTES)


def _resident(shape):
    return pl.BlockSpec(shape, lambda *_: (0,) * len(shape), pipeline_mode=pl.Buffered(1))


def _rel_bucket_np(dist):
    n = np.maximum(dist, 0)
    max_exact = REL_BUCKETS // 2
    ratio = np.maximum(n, max_exact).astype(np.float32) / np.float32(max_exact)
    log_ratio = np.log(ratio) / np.float32(math.log(REL_MAX_DIST / max_exact))
    large = max_exact + (log_ratio * np.float32(REL_BUCKETS - max_exact)).astype(np.int32)
    large = np.minimum(large, REL_BUCKETS - 1)
    return np.where(n < max_exact, n, large).astype(np.int32)


HALF = BLK // 2


def _bucket_tiles():
    k = np.arange(HALF)[:, None]
    q = np.arange(HALF)[None, :]
    band = np.where(q - k >= 0, _rel_bucket_np(q - k), -1)
    corner = _rel_bucket_np(HALF + q - k)
    return np.stack([band, corner]).astype(np.int32)


def _fill_bias_tiles(tab_ref, idx_ref, o_ref, first_head):
    assert HALF == REL_MAX_DIST
    idx = idx_ref[...]
    zero = jnp.zeros((HALF, HALF), F32)
    lo, hi = slice(0, HALF), slice(HALF, BLK)
    for h in range(o_ref.shape[0]):
        far = tab_ref[REL_BUCKETS - 1, first_head + h]
        acc = jnp.zeros(idx.shape, F32)
        for b in range(REL_BUCKETS - 1):
            acc = jnp.where(idx == b, (tab_ref[b, first_head + h] - far) * LOG2E, acc)
        band = jnp.where(idx[0] < 0, NEG_INF, acc[0])
        corner = acc[1]
        o_ref[h, 0, lo, lo] = band
        o_ref[h, 0, lo, hi] = corner
        o_ref[h, 0, hi, lo] = jnp.full((HALF, HALF), NEG_INF, F32)
        o_ref[h, 0, hi, hi] = band
        o_ref[h, 1, lo, lo] = zero
        o_ref[h, 1, lo, hi] = zero
        o_ref[h, 1, hi, lo] = corner
        o_ref[h, 1, hi, hi] = zero


def _bias_inputs(table):
    idx = jnp.asarray(_bucket_tiles())
    return (table.astype(F32), idx), [pl.BlockSpec(memory_space=pltpu.SMEM), _resident(idx.shape)]


def _inproj_kernel(x_ref, g_ref, w_ref, o_ref, *, q_chunks):
    hb = _rms(x_ref[...], g_ref[...]).astype(BF16)
    for j in range(w_ref.shape[1] // PROJ_CHUNK):
        cols = slice(j * PROJ_CHUNK, (j + 1) * PROJ_CHUNK)
        acc = _wdot(hb, w_ref[:, cols])
        if j in q_chunks:
            acc = acc * (QK_SCALE * LOG2E)
        o_ref[:, cols] = acc.astype(BF16)


def _in_projection(x2d, g, w_bf16):
    t, d = x2d.shape
    n = w_bf16.shape[1]
    assert t % TM_PROJ == 0 and n % PROJ_CHUNK == 0
    q_chunks = (0, (3 * A_WIDTH) // PROJ_CHUNK)
    return pl.pallas_call(
        functools.partial(_inproj_kernel, q_chunks=q_chunks),
        grid=(t // TM_PROJ,),
        in_specs=[pl.BlockSpec((TM_PROJ, d), lambda i: (i, 0)),
                  _resident((1, d)),
                  _resident((d, n))],
        out_specs=pl.BlockSpec((TM_PROJ, n), lambda i: (i, 0)),
        out_shape=jax.ShapeDtypeStruct((t, n), BF16),
        compiler_params=_params(1),
        name="in_projection",
    )(x2d, g, w_bf16)


def _half_mask(shape, half):
    lane = lax.broadcasted_iota(jnp.int32, shape, 1)
    return lax.shift_right_logical(lane, 6) == half


def _block_rows(n):
    return slice(n * BLK, (n + 1) * BLK)


class _Stream(NamedTuple):
    j: int
    slot: int
    q: Callable
    k_tile: Callable
    v_tile: Callable
    bias_tile: Callable
    row_bias: Callable
    done: Callable


def _two_pass_attention(streams, s_ref, n_diag_first):
    lead = 3
    queries = [None] * len(streams)
    col_max = [None] * len(streams)
    results = [None] * len(streams)

    def blocks(i):
        return list(range(streams[i].j, -1, -1))

    def score_step(i, n):
        st = streams[i]
        if queries[i] is None:
            queries[i] = st.q()
        s = _dot_nt(st.k_tile(n), queries[i])
        if n == st.j:
            bias = st.bias_tile(0)
            top = s[:HALF] + bias[:HALF]
            right = s[HALF:, HALF:] + bias[HALF:, HALF:]
            s_ref[st.slot, n, :HALF, :] = top
            s_ref[st.slot, n, HALF:, HALF:] = right
            mx = jnp.max(top.reshape(HALF // SUBLANES, SUBLANES, BLK), axis=0)
            mx_right = jnp.max(right.reshape(HALF // SUBLANES, SUBLANES, HALF), axis=0)
            mx = jnp.concatenate([mx[:, :HALF], jnp.maximum(mx[:, HALF:], mx_right)], axis=1)
            if n == 0:
                mx = jnp.max(mx, axis=0, keepdims=True)
            col_max[i] = mx
            return
        if n == st.j - 1:
            near = s[HALF:, :HALF] + st.bias_tile(1)
            s = jnp.concatenate([s[:HALF], jnp.concatenate([near, s[HALF:, HALF:]], axis=1)], axis=0)
        s_ref[st.slot, n] = s
        mx = jnp.max(s.reshape(BLK // SUBLANES, SUBLANES, BLK), axis=0)
        if st.row_bias(n) is not None:
            mx = mx + st.row_bias(n)
        mx = mx if col_max[i] is None else jnp.maximum(col_max[i], mx)
        if n == 0:
            mx = jnp.max(mx, axis=0, keepdims=True)
        col_max[i] = mx

    def value_step(i, n):
        st = streams[i]
        m = col_max[i]
        if n == st.j:
            top = jnp.exp2(s_ref[st.slot, n, :HALF, :] - m)
            right = jnp.exp2(s_ref[st.slot, n, HALF:, HALF:] - m[:, HALF:])
            bottom = jnp.concatenate([jnp.zeros((HALF, HALF), F32), right], axis=1)
            p = jnp.concatenate([top, bottom], axis=0).astype(BF16)
        else:
            if st.row_bias(n) is not None:
                m = m - st.row_bias(n)
            p = jnp.exp2(s_ref[st.slot, n] - m).astype(BF16)
        pv = _dot(st.v_tile(n), p)
        results[i] = pv if results[i] is None else results[i] + pv

    early = max(lead, n_diag_first)
    for i in range(early):
        score_step(i, streams[i].j)
    for i in range(lead):
        for n in blocks(i)[1:]:
            score_step(i, n)
    for i in range(len(streams)):
        a = i + lead
        ahead = [] if a >= len(streams) else blocks(a)[1:] if a < early else blocks(a)
        mine = blocks(i)
        for k in range(max(len(mine), len(ahead))):
            if k < len(mine):
                value_step(i, mine[k])
            if k < len(ahead):
                score_step(a, ahead[k])
        streams[i].done(results[i])


def _over_denominator(num, den):
    r = num.shape[0] // SUBLANES
    return (num.reshape(r, SUBLANES, BLK) / den[None]).reshape(num.shape)


def _times_row(x, row8):
    r = x.shape[0] // SUBLANES
    return (x.reshape(r, SUBLANES, BLK) * row8[None]).reshape(x.shape)


ONES_ROWS = 16
MOBA_ONES_ROWS = 64


def _moba_streams(j, r, q_ref, k_ref, bias_ref, g_ref, o_ref, kmt_ref, vaug_ref, topk):
    rows = _block_rows(r)

    def head_query(h):
        qp = q_ref[rows, (h // 2) * LANES:(h // 2 + 1) * LANES]
        return jnp.where(_half_mask(qp.shape, h % 2), qp, jnp.zeros_like(qp))

    sel_rows = None
    if j > topk:
        kmt = kmt_ref[...]
        kmt_hi = kmt.astype(BF16)
        kmt_lo = (kmt - kmt_hi.astype(F32)).astype(BF16)
        q_all = q_ref[rows, :]
        gate = _dot_nt(kmt_hi, q_all) + _dot_nt(kmt_lo, q_all)
        slabs = [gate[n * A_HEADS:(n + 1) * A_HEADS, :] for n in range(j)]
        sel_rows = []
        for n in range(j):
            rank = jnp.zeros(slabs[n].shape, F32)
            for m in range(j):
                if m != n:
                    beats = (slabs[m] >= slabs[n]) if m < n else (slabs[m] > slabs[n])
                    rank = rank + jnp.where(beats, 1.0, 0.0)
            sel_rows.append(jnp.where(rank < topk, 0.0, NEG_INF))

    def selection_bias(h, n):
        if sel_rows is None or n >= j:
            return None
        return sel_rows[n][h:h + 1, :]

    heads = []

    def head_done(h, result):
        heads.append(_over_denominator(result[:A_HEAD_DIM], result[A_HEAD_DIM:A_HEAD_DIM + SUBLANES]))
        if len(heads) == A_HEADS:
            o = jnp.concatenate(heads, axis=0)
            inv = lax.rsqrt(jnp.mean(o * o, axis=0, keepdims=True) + EPS)
            g = g_ref[...]
            y = o * inv * jnp.concatenate([g, g], axis=1)
            o_ref[rows, :] = y.T.astype(BF16)

    return [_Stream(j=j, slot=h,
                    q=functools.partial(head_query, h),
                    k_tile=lambda n, h=h: k_ref[_block_rows(n), (h // 2) * LANES:(h // 2 + 1) * LANES],
                    v_tile=lambda n, h=h: vaug_ref[h, :, _block_rows(n)],
                    bias_tile=lambda which, h=h: bias_ref[h, 0] if which == 0 else bias_ref[h, 1, HALF:, :HALF],
                    row_bias=functools.partial(selection_bias, h),
                    done=functools.partial(head_done, h))
            for h in range(A_HEADS)]


def _moba_kernel(q_ref, k_ref, v_ref, tab_ref, idx_ref, g_ref, o_ref, bias_ref, kmt_ref, vaug_ref, s_ref,
                 *, topk, per_step):
    t = pl.program_id(1)

    @pl.when((pl.program_id(0) == 0) & (t == 0))
    def _first_step():
        _fill_bias_tiles(tab_ref, idx_ref, bias_ref, first_head=0)

    seq = k_ref.shape[0]
    nb = seq // BLK
    n_pairs = A_WIDTH // LANES

    def tiles(t_static):
        streams = []
        for r in range(per_step):
            streams += _moba_streams(per_step * t_static + r, r, q_ref, k_ref, bias_ref, g_ref, o_ref,
                                     kmt_ref, vaug_ref, topk)
        _two_pass_attention(streams, s_ref, n_diag_first=A_HEADS if per_step * t_static > topk else 0)

    @pl.when(t == 0)
    def _first_tiles_of_batch():
        row = lax.broadcasted_iota(jnp.int32, (A_HEADS, A_WIDTH), 0)
        lane = lax.broadcasted_iota(jnp.int32, (A_HEADS, A_WIDTH), 1)
        head_mask = lax.shift_right_logical(lane, 6) == row
        kmt_ref[...] = jnp.zeros(kmt_ref.shape, F32)
        ones = jnp.ones((A_HEADS, BLK), BF16)
        for n in range(nb):
            km = _dot(ones, k_ref[n * BLK:(n + 1) * BLK, :]) * (1.0 / BLK)
            kmt_ref[n * A_HEADS:(n + 1) * A_HEADS, :] = jnp.where(head_mask, km, 0.0)
        for p in range(n_pairs):
            vt = v_ref[:, p * LANES:(p + 1) * LANES].T
            for half in range(2):
                h = 2 * p + half
                vaug_ref[h, :A_HEAD_DIM, :] = vt[half * A_HEAD_DIM:(half + 1) * A_HEAD_DIM]
                vaug_ref[h, A_HEAD_DIM:, :] = jnp.ones((MOBA_ONES_ROWS, seq), BF16)
        tiles(0)

    for t_static in range(1, nb // per_step):
        pl.when(t == t_static)(functools.partial(tiles, t_static))


def _moba_attention(proj, table, g, batch, seq):
    nq = seq // BLK
    per_step = min(TILES_PER_STEP, nq)
    assert seq % (per_step * BLK) == 0 and nq <= 8
    topk = min(MOBA_TOPK, nq)
    steps = nq // per_step
    bias_operands, bias_specs = _bias_inputs(table)
    return pl.pallas_call(
        functools.partial(_moba_kernel, topk=topk, per_step=per_step),
        grid=(batch, steps),
        in_specs=[pl.BlockSpec((per_step * BLK, A_WIDTH), lambda b, t: (b * steps + t, 0)),
                  pl.BlockSpec((seq, A_WIDTH), lambda b, t: (b, 1)),
                  pl.BlockSpec((seq, A_WIDTH), lambda b, t: (b, 2)),
                  *bias_specs,
                  _resident(g.shape)],
        out_specs=pl.BlockSpec((per_step * BLK, A_WIDTH), lambda b, t: (b * steps + t, 0)),
        out_shape=jax.ShapeDtypeStruct((batch * seq, A_WIDTH), BF16),
        scratch_shapes=[pltpu.VMEM((A_HEADS, 2, BLK, BLK), F32),
                        pltpu.VMEM((8 * A_HEADS, A_WIDTH), F32),
                        pltpu.VMEM((A_HEADS, A_HEAD_DIM + MOBA_ONES_ROWS, seq), BF16),
                        pltpu.VMEM((A_HEADS, nq, BLK, BLK), F32)],
        compiler_params=_params(2),
        name="moba_attention",
    )(proj, proj, proj, *bias_operands, g)


def _diff_streams(j, r, q_ref, k_ref, bias_ref, lam8, g_ref, o_ref, vaug_ref):
    rows = _block_rows(r)

    def map_query(s):
        qp = q_ref[rows, (s // 2) * LANES:(s // 2 + 1) * LANES]
        return jnp.where(_half_mask(qp.shape, s % 2), qp, jnp.zeros_like(qp))

    acc = [None] * (2 * B_HEADS)

    def map_done(s, result):
        acc[s] = result
        if s % 2 == 1:
            h = s // 2
            a0, a1 = acc[s - 1], acc[s]
            n0, l0 = a0[:B_V_DIM], a0[B_V_DIM:B_V_DIM + SUBLANES]
            n1, l1 = a1[:B_V_DIM], a1[B_V_DIM:B_V_DIM + SUBLANES]
            o = _over_denominator(_times_row(n0, l1) - _times_row(n1, lam8 * l0), l0 * l1)
            inv = lax.rsqrt(jnp.mean(o * o, axis=0, keepdims=True) + EPS)
            g = g_ref[...]
            y = o * inv * jnp.concatenate([g, g], axis=1) * (1.0 - LAMBDA_INIT)
            o_ref[rows, h * B_V_DIM:(h + 1) * B_V_DIM] = y.T.astype(BF16)

    return [_Stream(j=j, slot=s,
                    q=functools.partial(map_query, s),
                    k_tile=lambda n, s=s: k_ref[_block_rows(n), (s // 2) * LANES:(s // 2 + 1) * LANES],
                    v_tile=lambda n, s=s: vaug_ref[s // 2, :, _block_rows(n)],
                    bias_tile=lambda which, s=s: (bias_ref[s // 2, 0] if which == 0
                                                  else bias_ref[s // 2, 1, HALF:, :HALF]),
                    row_bias=lambda n: None,
                    done=functools.partial(map_done, s))
            for s in range(2 * B_HEADS)]


def _diff_kernel(q_ref, k_ref, v_ref, tab_ref, idx_ref, lam_ref, g_ref, o_ref, bias_ref, vaug_ref, s_ref,
                 *, per_step):
    t = pl.program_id(1)

    @pl.when((pl.program_id(0) == 0) & (t == 0))
    def _first_step():
        _fill_bias_tiles(tab_ref, idx_ref, bias_ref, first_head=A_HEADS)

    seq = k_ref.shape[0]

    def tiles(t_static):
        lp = lam_ref[...]
        lam = (jnp.exp(jnp.sum(lp[0:1] * lp[1:2], axis=1, keepdims=True))
               - jnp.exp(jnp.sum(lp[2:3] * lp[3:4], axis=1, keepdims=True)) + LAMBDA_INIT)
        lam8 = jnp.broadcast_to(lam, (SUBLANES, BLK))
        streams = []
        for r in range(per_step):
            streams += _diff_streams(per_step * t_static + r, r, q_ref, k_ref, bias_ref, lam8, g_ref,
                                     o_ref, vaug_ref)
        _two_pass_attention(streams, s_ref, n_diag_first=0)

    @pl.when(t == 0)
    def _first_tiles_of_batch():
        for h in range(B_HEADS):
            vaug_ref[h, :B_V_DIM, :] = v_ref[:, h * B_V_DIM:(h + 1) * B_V_DIM].T
            vaug_ref[h, B_V_DIM:, :] = jnp.ones((ONES_ROWS, seq), BF16)
        tiles(0)

    for t_static in range(1, seq // (per_step * BLK)):
        pl.when(t == t_static)(functools.partial(tiles, t_static))


def _diff_attention(proj, table, lam, g, batch, seq):
    nq = seq // BLK
    per_step = min(TILES_PER_STEP, nq)
    assert seq % (per_step * BLK) == 0 and B_V_DIM == LANES
    steps = nq // per_step
    first = (3 * A_WIDTH) // B_WIDTH
    bias_operands, bias_specs = _bias_inputs(table)
    return pl.pallas_call(
        functools.partial(_diff_kernel, per_step=per_step),
        grid=(batch, steps),
        in_specs=[pl.BlockSpec((per_step * BLK, B_WIDTH), lambda b, t: (b * steps + t, first)),
                  pl.BlockSpec((seq, B_WIDTH), lambda b, t: (b, first + 1)),
                  pl.BlockSpec((seq, B_WIDTH), lambda b, t: (b, first + 2)),
                  *bias_specs,
                  _resident(lam.shape),
                  _resident(g.shape)],
        out_specs=pl.BlockSpec((per_step * BLK, B_WIDTH), lambda b, t: (b * steps + t, 0)),
        out_shape=jax.ShapeDtypeStruct((batch * seq, B_WIDTH), BF16),
        scratch_shapes=[pltpu.VMEM((B_HEADS, 2, BLK, BLK), F32),
                        pltpu.VMEM((B_HEADS, B_V_DIM + ONES_ROWS, seq), BF16),
                        pltpu.VMEM((2 * B_HEADS, nq, BLK, BLK), F32)],
        compiler_params=_params(2),
        name="diff_attention",
    )(proj, proj, proj, *bias_operands, lam, g)


def _cross_kernel(x_ref, oa_ref, ob_ref, wo_ref, g_ref, wq_ref, mem_ref, gm_ref, wk_ref, wv_ref, wco_ref,
                  o_ref, kc_ref, vc_ref, *, scale, per_seq):
    @pl.when(pl.program_id(0) % per_seq == 0)
    def _memory_keys_values():
        m = _rms(mem_ref[...], gm_ref[...]).astype(BF16)
        kc_ref[...] = _wdot(m, wk_ref[...]).astype(BF16)
        vc_ref[...] = _wdot(m, wv_ref[...]).astype(BF16)

    x1 = (x_ref[...] + _wdot(oa_ref[...], wo_ref[:A_WIDTH, :]) + _wdot(ob_ref[...], wo_ref[A_WIDTH:, :]))
    hb = _rms(x1, g_ref[...]).astype(BF16)
    q = (_wdot(hb, wq_ref[...]) * scale).astype(BF16)
    hd = q.shape[1] // MEM_HEADS
    heads = []
    for h in range(MEM_HEADS):
        cols = slice(h * hd, (h + 1) * hd)
        s = _dot_nt(q[:, cols], kc_ref[:, cols])
        p = jnp.exp2(s - jnp.max(s, axis=1, keepdims=True))
        l = jnp.sum(p, axis=1, keepdims=True)
        heads.append((_dot(p.astype(BF16), vc_ref[:, cols]) / l).astype(BF16))
    o = jnp.concatenate(heads, axis=1)
    o_ref[...] = x1 + _wdot(o, wco_ref[...])


def _outproj_cross(x2d, oa, ob, wo, g, wq, mem2d, gm, wk, wv, wco, seq, mem_len):
    t, d = x2d.shape
    per_seq = seq // TM_CROSS
    assert seq % TM_CROSS == 0
    n = wk.shape[1]
    hd = wq.shape[1] // MEM_HEADS
    scale = hd ** -0.5 * LOG2E
    return pl.pallas_call(
        functools.partial(_cross_kernel, scale=scale, per_seq=per_seq),
        grid=(t // TM_CROSS,),
        in_specs=[pl.BlockSpec((TM_CROSS, d), lambda i: (i, 0)),
                  pl.BlockSpec((TM_CROSS, A_WIDTH), lambda i: (i, 0)),
                  pl.BlockSpec((TM_CROSS, B_WIDTH), lambda i: (i, 0)),
                  _resident(wo.shape), _resident((1, d)), _resident(wq.shape),
                  pl.BlockSpec((mem_len, d), lambda i: (i // per_seq, 0)),
                  _resident((1, d)), _resident(wk.shape), _resident(wv.shape),
                  _resident(wco.shape)],
        out_specs=pl.BlockSpec((TM_CROSS, d), lambda i: (i, 0)),
        out_shape=jax.ShapeDtypeStruct((t, d), F32),
        scratch_shapes=[pltpu.VMEM((mem_len, n), BF16),
                        pltpu.VMEM((mem_len, n), BF16)],
        compiler_params=_params(1),
        name="outproj_cross_attention",
    )(x2d, oa, ob, wo, g, wq, mem2d, gm, wk, wv, wco)


def _ffn_kernel(x_ref, g_ref, wg_ref, wu_ref, wd_ref, gf_ref, o_ref):
    x = x_ref[...]
    hb = _rms(x, g_ref[...]).astype(BF16)
    o_ref[...] = x
    for c in range(wg_ref.shape[1] // FF_CHUNK):
        cols = slice(c * FF_CHUNK, (c + 1) * FF_CHUNK)
        a = (jax.nn.silu(_wdot(hb, wg_ref[:, cols])) * _wdot(hb, wu_ref[:, cols])).astype(BF16)
        o_ref[...] += _wdot(a, wd_ref[cols, :])
    o_ref[...] = _rms(o_ref[...], gf_ref[...])


def _swiglu_final(x2d, g, wg, wu, wd, gf):
    t, d = x2d.shape
    assert t % TM_FFN == 0 and wg.shape[1] % FF_CHUNK == 0
    return pl.pallas_call(
        _ffn_kernel,
        grid=(t // TM_FFN,),
        in_specs=[pl.BlockSpec((TM_FFN, d), lambda i: (i, 0)),
                  _resident((1, d)), _resident(wg.shape), _resident(wu.shape), _resident(wd.shape),
                  _resident((1, d))],
        out_specs=pl.BlockSpec((TM_FFN, d), lambda i: (i, 0)),
        out_shape=jax.ShapeDtypeStruct((t, d), F32),
        compiler_params=_params(1),
        name="swiglu_final_norm",
    )(x2d, g, wg, wu, wd, gf)


def kernel(x, mem, mix_norm_g, w_in, moba_out_g, diff_lambda, diff_subln_g, w_out, rel_bias_table,
           cross_norm_g, mem_norm_g, w_cq, w_ck, w_cv, w_co, ffn_norm_g, w_gate, w_up, w_down,
           final_norm_g):
    batch, seq, d = x.shape
    mem_len = mem.shape[1]
    assert mix_norm_g.shape[0] == 1, "single-layer trunk"
    x2d = x.reshape(batch * seq, d)
    mem2d = mem.reshape(batch * mem_len, d)
    row = lambda v: v.reshape(1, -1).astype(F32)
    wb = lambda w: w[0].astype(F32)

    proj = _in_projection(x2d, row(mix_norm_g[0]), wb(w_in))
    col = lambda v: jnp.broadcast_to(v.reshape(-1, 1).astype(F32), (v.size, LANES))
    oa = _moba_attention(proj, rel_bias_table, col(moba_out_g[0]), batch, seq)
    ob = _diff_attention(proj, rel_bias_table, diff_lambda[0].astype(F32), col(diff_subln_g[0]), batch, seq)
    x2 = _outproj_cross(x2d, oa, ob, wb(w_out), row(cross_norm_g[0]), wb(w_cq),
                        mem2d, row(mem_norm_g[0]), wb(w_ck), wb(w_cv), wb(w_co), seq, mem_len)
    out = _swiglu_final(x2, row(ffn_norm_g[0]), wb(w_gate), wb(w_up), wb(w_down), row(final_norm_g))
    return out.reshape(batch, seq, d)
```
